```python
import math
import jax, jax.numpy as jnp
from jax import lax
import numpy as np

D_MODEL = 1024
BATCH = 8
SEQ = 8192
DEPTH = 2

N_MIXERS = 2
N_FOURIER_GROUPS = 8
FOURIER_GROUP = D_MODEL // N_FOURIER_GROUPS
N_DIFF_HEADS = 8
DIFF_HEAD_DIM = D_MODEL // (2 * N_DIFF_HEADS)
DIFF_V_DIM = 2 * DIFF_HEAD_DIM
D_FF = -(-8 * D_MODEL // (3 * 256)) * 256
Q_BLOCK = 128
RMS_EPS = 1e-6
N_FOURIER_LAYERS = (DEPTH + 1) // 2
N_DIFF_LAYERS = DEPTH // 2

kernel_name = "hybrid_fnet_diffattn_alibi_encoder"


def rms_norm(x, g):
    xf = x.astype(jnp.float32)
    y = xf * lax.rsqrt(jnp.mean(xf * xf, axis=-1, keepdims=True) + RMS_EPS)
    return (y * g.astype(jnp.float32)).astype(x.dtype)


def alibi_slopes(n_heads):
    return jnp.asarray([2.0 ** (-8.0 * (h + 1) / n_heads) for h in range(n_heads)], dtype=jnp.float32)


def lambda_init_fn(layer_idx):
    return 0.8 - 0.6 * math.exp(-0.3 * layer_idx)


def fourier_mixer(xn, w_o):
    b, s, d = xn.shape
    xg = xn.reshape(b, s, N_FOURIER_GROUPS, FOURIER_GROUP).astype(jnp.float32)
    y = jnp.fft.fftn(xg, axes=(1, 3), norm="ortho").real
    y = y.reshape(b, s, d).astype(xn.dtype)
    return y @ w_o


def diff_attention_mixer(xn, w_qkv, lq1, lk1, lq2, lk2, subln_g, w_o, layer_idx):
    b, s, d = xn.shape
    h, dh, dv = N_DIFF_HEADS, DIFF_HEAD_DIM, DIFF_V_DIM
    qkv = xn @ w_qkv
    q = qkv[..., :d].reshape(b, s, h, 2, dh)
    k = qkv[..., d:2 * d].reshape(b, s, h, 2, dh)
    v = qkv[..., 2 * d:].reshape(b, s, h, dv)

    lam_init = lambda_init_fn(layer_idx)
    lam = (jnp.exp(jnp.sum(lq1.astype(jnp.float32) * lk1.astype(jnp.float32)))
           - jnp.exp(jnp.sum(lq2.astype(jnp.float32) * lk2.astype(jnp.float32)))
           + lam_init)

    scale = dh ** -0.5
    n_blk = s // Q_BLOCK
    qb = q.reshape(b, n_blk, Q_BLOCK, h, 2, dh).transpose(1, 0, 3, 4, 2, 5)
    kt = k.transpose(0, 2, 3, 1, 4)
    vt = v.transpose(0, 2, 1, 3)
    slopes = alibi_slopes(h)
    key_pos = jnp.arange(s, dtype=jnp.int32)
    starts = jnp.arange(n_blk, dtype=jnp.int32) * Q_BLOCK

    def block(args):
        q_blk, t0 = args
        scores = jnp.einsum('bhcqd,bhcsd->bhcqs', q_blk, kt).astype(jnp.float32) * scale
        q_pos = t0 + jnp.arange(Q_BLOCK, dtype=jnp.int32)
        dist = jnp.abs(q_pos[:, None] - key_pos[None, :]).astype(jnp.float32)
        bias = -slopes[:, None, None] * dist[None]
        p = jax.nn.softmax(scores + bias[None, :, None], axis=-1)
        a = p[:, :, 0] - lam * p[:, :, 1]
        return jnp.einsum('bhqs,bhse->bhqe', a.astype(vt.dtype), vt)

    o = lax.map(block, (qb, starts))
    o = o.transpose(1, 0, 3, 2, 4).reshape(b, s, h, dv)
    o = rms_norm(o, subln_g) * (1.0 - lam_init)
    return o.reshape(b, s, h * dv) @ w_o


def swiglu_ffn(xn, w_gate, w_up, w_down):
    return (jax.nn.silu(xn @ w_gate) * (xn @ w_up)) @ w_down


def setup_inputs(seed: int = 0) -> dict:
    key = jax.random.key(seed)
    ks = jax.random.split(key, 20)
    D, F = D_MODEL, D_FF

    def gain(k, shape):
        return 1.0 + 0.02 * jax.random.normal(k, shape, jnp.float32)

    def w(k, shape, fan_in):
        return jax.random.normal(k, shape, jnp.float32) * fan_in ** -0.5

    return {
        "x": jax.random.normal(ks[0], (BATCH, SEQ, D), jnp.float32),
        "norm_mix_pre": gain(ks[1], (DEPTH, D)),
        "norm_mix_post": gain(ks[2], (DEPTH, D)),
        "norm_ffn_pre": gain(ks[3], (DEPTH, D)),
        "norm_ffn_post": gain(ks[4], (DEPTH, D)),
        "fourier_w_o": w(ks[5], (N_FOURIER_LAYERS, D, D), D),
        "diff_w_qkv": w(ks[6], (N_DIFF_LAYERS, D, 3 * D), D),
        "diff_lambda_q1": 0.1 * jax.random.normal(ks[7], (N_DIFF_LAYERS, DIFF_HEAD_DIM), jnp.float32),
        "diff_lambda_k1": 0.1 * jax.random.normal(ks[8], (N_DIFF_LAYERS, DIFF_HEAD_DIM), jnp.float32),
        "diff_lambda_q2": 0.1 * jax.random.normal(ks[9], (N_DIFF_LAYERS, DIFF_HEAD_DIM), jnp.float32),
        "diff_lambda_k2": 0.1 * jax.random.normal(ks[10], (N_DIFF_LAYERS, DIFF_HEAD_DIM), jnp.float32),
        "diff_subln_g": gain(ks[11], (N_DIFF_LAYERS, DIFF_V_DIM)),
        "diff_w_o": w(ks[12], (N_DIFF_LAYERS, D, D), D),
        "ffn_w_gate": w(ks[13], (DEPTH, D, F), D),
        "ffn_w_up": w(ks[14], (DEPTH, D, F), D),
        "ffn_w_down": w(ks[15], (DEPTH, F, D), F),
    }


def reference(x, norm_mix_pre, norm_mix_post, norm_ffn_pre, norm_ffn_post,
              fourier_w_o, diff_w_qkv, diff_lambda_q1, diff_lambda_k1,
              diff_lambda_q2, diff_lambda_k2, diff_subln_g, diff_w_o,
              ffn_w_gate, ffn_w_up, ffn_w_down):
    h = x
    for i in range(DEPTH):
        xn = rms_norm(h, norm_mix_pre[i])
        j = i // N_MIXERS
        if i % N_MIXERS == 0:
            m = fourier_mixer(xn, fourier_w_o[j])
        else:
            m = diff_attention_mixer(xn, diff_w_qkv[j], diff_lambda_q1[j], diff_lambda_k1[j],
                                     diff_lambda_q2[j], diff_lambda_k2[j], diff_subln_g[j],
                                     diff_w_o[j], i)
        h = h + rms_norm(m, norm_mix_post[i])
        f = swiglu_ffn(rms_norm(h, norm_ffn_pre[i]), ffn_w_gate[i], ffn_w_up[i], ffn_w_down[i])
        h = h + rms_norm(f, norm_ffn_post[i])
    return h
```

```python
import functools
import math

import numpy as np
import jax
import jax.numpy as jnp
from jax import lax
from jax.experimental import pallas as pl
from jax.experimental.pallas import tpu as pltpu

N_FOURIER_GROUPS = 8
N_DIFF_HEADS = 8
RMS_EPS = 1e-6
N_MIXERS = 2

LANES = 128
VMEM_LIMIT_BYTES = 56 * 1024 * 1024

FFT_N2 = 128
FFT_S2_PER_STEP = 8
TOKEN_TILE = 512
FFN_CHUNK = 256
ATT_TQ = 256
ATT_CK = 1024
ATT_NSUB = ATT_CK // ATT_TQ
LOG2E = 1.4426950408889634


def _rms_scale(x):
    return lax.rsqrt(jnp.mean(x * x, axis=-1, keepdims=True) + RMS_EPS)


def _np_bf16(v):
    return np.asarray(v, np.float32).astype(jnp.bfloat16)


def _bf16_split3(v):
    v = np.asarray(v, np.float64)
    hi = _np_bf16(v).astype(np.float64)
    mid = _np_bf16(v - hi).astype(np.float64)
    lo = _np_bf16(v - hi - mid).astype(np.float64)
    return hi, mid, lo


def _fft_stage1_kernel(x_ref, g_ref, f1_ref, ar_ref, ai_ref, *, n1, d, nb):
    g = g_ref[...]
    f1 = f1_ref[...]
    for j in range(nb):
        xs = x_ref[:, j * d:(j + 1) * d]
        xn = (xs * _rms_scale(xs) * g).astype(jnp.bfloat16)
        a = jnp.dot(f1, xn, preferred_element_type=jnp.float32)
        ar_ref[:, j * d:(j + 1) * d] = a[:n1].astype(jnp.bfloat16)
        ai_ref[:, j * d:(j + 1) * d] = a[n1:].astype(jnp.bfloat16)


def _fft_stage2_kernel(ar_ref, ai_ref, gm_ref, cs_ref, wo_ref, gpost_ref, xres_ref, out_ref, *, n2, d, cg):
    a = jnp.concatenate([ar_ref[...], ai_ref[...]], axis=0)
    y = jnp.dot(gm_ref[...], a, preferred_element_type=jnp.float32)
    yr = y[:n2].astype(jnp.bfloat16)
    yi = y[n2:].astype(jnp.bfloat16)
    cs = cs_ref[...]
    cols = []
    for gi in range(d // cg):
        lhs = jnp.concatenate([yr[:, gi * cg:(gi + 1) * cg], yi[:, gi * cg:(gi + 1) * cg]], axis=1)
        cols.append(jnp.dot(lhs, cs, preferred_element_type=jnp.float32))
    z = jnp.concatenate(cols, axis=1).astype(jnp.bfloat16)
    m = jnp.dot(z, wo_ref[...], preferred_element_type=jnp.float32)
    out_ref[...] = xres_ref[...] + m * _rms_scale(m) * gpost_ref[...]


def _fourier_layer(x, g_pre, g_post, w_o):
    b, s, d = x.shape
    n2 = FFT_N2
    n1 = s // n2
    cg = d // N_FOURIER_GROUPS
    nb = FFT_S2_PER_STEP
    assert n1 * n2 == s and n2 % nb == 0

    s1 = np.arange(n1)
    th1 = 2.0 * np.pi * ((s1[:, None] * s1[None, :]) % n1) / n1
    f1 = np.concatenate([np.cos(th1), -np.sin(th1)], axis=0)
    s2 = np.arange(n2)
    kk = (np.arange(n1)[:, None, None] + n1 * np.arange(n2)[None, :, None])
    th2 = 2.0 * np.pi * ((kk * s2[None, None, :]) % s) / s
    gr, gi = np.cos(th2), -np.sin(th2)
    gm = np.concatenate([np.concatenate([gr, -gi], axis=2),
                         np.concatenate([gi, gr], axis=2)], axis=1)
    c = np.arange(cg)
    thc = 2.0 * np.pi * ((c[:, None] * c[None, :]) % cg) / cg
    norm = 1.0 / math.sqrt(s * cg)
    cs = np.concatenate([np.cos(thc), np.sin(thc)], axis=0) * norm

    f1 = jnp.asarray(_np_bf16(f1))
    gm = jnp.asarray(_np_bf16(gm))
    cs = jnp.asarray(_np_bf16(cs))

    x3 = x.reshape(b, n1, n2 * d)
    ar, ai = pl.pallas_call(
        functools.partial(_fft_stage1_kernel, n1=n1, d=d, nb=nb),
        grid=(b, n2 // nb),
        in_specs=[
            pl.BlockSpec((None, n1, nb * d), lambda i, j: (i, 0, j)),
            pl.BlockSpec((1, d), lambda i, j: (0, 0)),
            pl.BlockSpec((2 * n1, n1), lambda i, j: (0, 0)),
        ],
        out_specs=[
            pl.BlockSpec((None, n1, nb * d), lambda i, j: (i, 0, j)),
            pl.BlockSpec((None, n1, nb * d), lambda i, j: (i, 0, j)),
        ],
        out_shape=[jax.ShapeDtypeStruct((b, n1, n2 * d), jnp.bfloat16)] * 2,
        compiler_params=pltpu.CompilerParams(
            dimension_semantics=("parallel", "parallel"), vmem_limit_bytes=VMEM_LIMIT_BYTES),
        name="fft_stage1",
    )(x3, g_pre.reshape(1, d), f1)

    ar4 = ar.reshape(b, n1, n2, d)
    ai4 = ai.reshape(b, n1, n2, d)
    xk = x.reshape(b, n2, n1 * d)
    out = pl.pallas_call(
        functools.partial(_fft_stage2_kernel, n2=n2, d=d, cg=cg),
        grid=(b, n1),
        in_specs=[
            pl.BlockSpec((None, None, n2, d), lambda i, j: (i, j, 0, 0)),
            pl.BlockSpec((None, None, n2, d), lambda i, j: (i, j, 0, 0)),
            pl.BlockSpec((None, 2 * n2, 2 * n2), lambda i, j: (j, 0, 0)),
            pl.BlockSpec((2 * cg, cg), lambda i, j: (0, 0)),
            pl.BlockSpec((d, d), lambda i, j: (0, 0)),
            pl.BlockSpec((1, d), lambda i, j: (0, 0)),
            pl.BlockSpec((None, n2, d), lambda i, j: (i, 0, j)),
        ],
        out_specs=pl.BlockSpec((None, n2, d), lambda i, j: (i, 0, j)),
        out_shape=jax.ShapeDtypeStruct((b, n2, n1 * d), jnp.float32),
        compiler_params=pltpu.CompilerParams(
            dimension_semantics=("parallel", "parallel"), vmem_limit_bytes=VMEM_LIMIT_BYTES),
        name="fft_stage2",
    )(ar4, ai4, gm, cs, w_o.astype(jnp.bfloat16), g_post.reshape(1, d), xk)
    return out.reshape(b, s, d)


def _ffn_kernel(x_ref, gpre_ref, gpost_ref, wg_ref, wu_ref, wd_ref, out_ref, *, f, chunk):
    x = x_ref[...]
    xn = (x * _rms_scale(x) * gpre_ref[...]).astype(jnp.bfloat16)
    acc = jnp.zeros(x.shape, jnp.float32)
    for c0 in range(0, f, chunk):
        c1 = min(c0 + chunk, f)
        gt = jnp.dot(xn, wg_ref[:, c0:c1], preferred_element_type=jnp.float32)
        up = jnp.dot(xn, wu_ref[:, c0:c1], preferred_element_type=jnp.float32)
        hh = (gt * jax.nn.sigmoid(gt) * up).astype(jnp.bfloat16)
        acc = acc + jnp.dot(hh, wd_ref[c0:c1, :], preferred_element_type=jnp.float32)
    out_ref[...] = x + acc * _rms_scale(acc) * gpost_ref[...]


def _ffn_layer(h, g_pre, g_post, w_gate, w_up, w_down):
    b, s, d = h.shape
    f = w_gate.shape[1]
    t = b * s
    tm = min(TOKEN_TILE, t)
    assert t % tm == 0
    const = lambda i: (0, 0)
    out = pl.pallas_call(
        functools.partial(_ffn_kernel, f=f, chunk=FFN_CHUNK),
        grid=(t // tm,),
        in_specs=[
            pl.BlockSpec((tm, d), lambda i: (i, 0)),
            pl.BlockSpec((1, d), const),
            pl.BlockSpec((1, d), const),
            pl.BlockSpec((d, f), const, pipeline_mode=pl.Buffered(1)),
            pl.BlockSpec((d, f), const, pipeline_mode=pl.Buffered(1)),
            pl.BlockSpec((f, d), const, pipeline_mode=pl.Buffered(1)),
        ],
        out_specs=pl.BlockSpec((tm, d), lambda i: (i, 0)),
        out_shape=jax.ShapeDtypeStruct((t, d), jnp.float32),
        compiler_params=pltpu.CompilerParams(
            dimension_semantics=("parallel",), vmem_limit_bytes=VMEM_LIMIT_BYTES),
        name="swiglu_ffn",
    )(h.reshape(t, d), g_pre.reshape(1, d), g_post.reshape(1, d),
      w_gate.astype(jnp.bfloat16), w_up.astype(jnp.bfloat16), w_down.astype(jnp.bfloat16))
    return out.reshape(b, s, d)


def _qkv_kernel(x_ref, gpre_ref, w_ref, qa_ref, ka_ref, q_ref, k_ref, v_ref, *, d, nh, tm, qscale):
    x = x_ref[...]
    xn = (x * _rms_scale(x) * gpre_ref[...]).astype(jnp.bfloat16)
    dh2 = d // nh
    row = pl.program_id(1) * tm + lax.broadcasted_iota(jnp.int32, (tm, 1), 0)
    ir = (row & (ATT_TQ - 1)).astype(jnp.float32)
    jr = ir
    js = (row & (ATT_CK - ATT_TQ)).astype(jnp.float32)
    lane = lax.broadcasted_iota(jnp.int32, (1, dh2), 1)
    half = dh2 // 2
    q_all = jnp.dot(xn, w_ref[:, 0:d], preferred_element_type=jnp.float32) * qscale
    k_all = jnp.dot(xn, w_ref[:, d:2 * d], preferred_element_type=jnp.float32)
    v_all = jnp.dot(xn, w_ref[:, 2 * d:3 * d], preferred_element_type=jnp.float32)
    ones_col = jnp.where(lane == 0, 1.0, 0.0).astype(jnp.bfloat16)
    for h in range(nh):
        qh = q_all[:, h * dh2:(h + 1) * dh2]
        kh = k_all[:, h * dh2:(h + 1) * dh2]
        for comp in range(2):
            data = (lane < half) if comp == 0 else (lane >= half)
            qa = qa_ref[h, comp]
            ka = ka_ref[h, comp]
            q_aug = qa[0:1] + qa[1:2] * ir
            k_aug = ka[0:1] + ka[1:2] * jr + ka[2:3] * js
            q_ref[h, comp] = jnp.where(data, qh, q_aug).astype(jnp.bfloat16)
            k_ref[h, comp] = jnp.where(data, kh, k_aug).astype(jnp.bfloat16)
        vh = v_all[:, h * dh2:(h + 1) * dh2].astype(jnp.bfloat16)
        v_ref[h] = jnp.concatenate([vh, jnp.broadcast_to(ones_col, (tm, dh2))], axis=1)


def _aug_tables(nh, dh2, cvals):
    half = dh2 // 2
    qa = np.zeros((nh, 2, 2, dh2), np.float64)
    ka = np.zeros((nh, 2, 3, dh2), np.float64)
    for h in range(nh):
        pieces = _bf16_split3(cvals[h])
        for comp in range(2):
            base = half if comp == 0 else 0
            for p in range(3):
                qa[h, comp, 0, base + p] = pieces[p]
                ka[h, comp, 1, base + p] = 1.0
                qa[h, comp, 0, base + 3 + p] = pieces[p]
                ka[h, comp, 2, base + 3 + p] = 1.0
                qa[h, comp, 1, base + 6 + p] = -1.0
                ka[h, comp, 0, base + 6 + p] = pieces[p]
    return jnp.asarray(qa, jnp.float32), jnp.asarray(ka, jnp.float32)


def _attn_kernel(c_ref, q_ref, k_ref, v_ref, bias_ref, lam_ref, g_ref, o_ref, m_scr, acc_scr,
                 *, s, dv, lam_init):
    tq, ck, nsub = ATT_TQ, ATT_CK, ATT_NSUB
    nch = s // ck
    hd = pl.program_id(1)
    qi = pl.program_id(2)
    c = c_ref[hd]
    i0 = qi * tq
    cd = qi // nsub
    dsub = qi % nsub
    lane = lax.broadcasted_iota(jnp.int32, (1, LANES), 1)
    half = LANES // 2

    m_scr[...] = jnp.full(m_scr.shape, -1e30, jnp.float32)
    acc_scr[...] = jnp.zeros(acc_scr.shape, jnp.float32)

    q_left, q_right, q_diag = [], [], []
    for comp in range(2):
        data = (lane < half) if comp == 0 else (lane >= half)
        qc = q_ref[comp]
        q_left.append(qc)
        q_right.append(jnp.where(data, qc, -qc))
        q_diag.append(jnp.where(data, qc, jnp.zeros_like(qc)))

    def chunk(cj, mode):
        j0 = pl.multiple_of(cj * ck, ck)
        vc = v_ref[pl.ds(j0, ck), :]
        for comp in range(2):
            kc = k_ref[comp, pl.ds(j0, ck), :]
            qv = {"L": q_left, "R": q_right, "D": q_diag}[mode][comp]
            t = lax.dot_general(qv, kc, (((1,), (1,)), ((), ())),
                                preferred_element_type=jnp.float32)
            if mode == "L":
                kappa = -c * (i0 - j0).astype(jnp.float32)
            elif mode == "R":
                kappa = -c * (j0 - i0).astype(jnp.float32)
            else:
                off = pl.multiple_of((nsub - 1 - dsub) * tq, tq)
                t = t + bias_ref[:, pl.ds(off, ck)]
                kappa = jnp.float32(0.0)
            m_old = m_scr[comp]
            m_new = jnp.maximum(m_old, jnp.max(t, axis=-1, keepdims=True) + kappa)
            alpha = jnp.exp2(m_old - m_new)
            p = jnp.exp2(t - (m_new - kappa)).astype(jnp.bfloat16)
            acc_scr[comp] = alpha * acc_scr[comp] + jnp.dot(p, vc, preferred_element_type=jnp.float32)
            m_scr[comp] = m_new

    def left_body(cj, carry):
        chunk(cj, "L")
        return carry

    def right_body(cj, carry):
        chunk(cj, "R")
        return carry

    lax.fori_loop(0, cd, left_body, 0)
    chunk(cd, "D")
    lax.fori_loop(cd + 1, nch, right_body, 0)

    lam_p = lam_ref[...]
    lam = (jnp.exp(jnp.sum(lam_p[0:1] * lam_p[1:2], axis=-1, keepdims=True))
           - jnp.exp(jnp.sum(lam_p[2:3] * lam_p[3:4], axis=-1, keepdims=True)) + lam_init)
    a0 = acc_scr[0]
    a1 = acc_scr[1]
    o = a0[:, :dv] / a0[:, dv:dv + 1] - lam * (a1[:, :dv] / a1[:, dv:dv + 1])
    o = o * _rms_scale(o) * g_ref[...] * (1.0 - lam_init)
    o_ref[...] = o.astype(jnp.bfloat16)


def _proj_post_kernel(a_ref, w_ref, gpost_ref, xres_ref, out_ref):
    m = jnp.dot(a_ref[...], w_ref[...], preferred_element_type=jnp.float32)
    out_ref[...] = xres_ref[...] + m * _rms_scale(m) * gpost_ref[...]


def _diff_attention_layer(hin, g_pre, g_post, w_qkv, lq1, lk1, lq2, lk2, subln_g, w_o, layer_idx):
    b, s, d = hin.shape
    nh = N_DIFF_HEADS
    dh2 = d // nh
    dh = dh2 // 2
    dv = dh2
    assert dh2 == LANES and s % ATT_CK == 0
    lam_init = 0.8 - 0.6 * math.exp(-0.3 * layer_idx)
    slopes = np.asarray([2.0 ** (-8.0 * (i + 1) / nh) for i in range(nh)], np.float64)
    cvals = slopes * LOG2E
    qa, ka = _aug_tables(nh, dh2, cvals)
    c3 = np.sum(np.stack(_bf16_split3(cvals)), axis=0)
    qscale = dh ** -0.5 * LOG2E

    tm = min(TOKEN_TILE, s)
    const2 = lambda i, j: (0, 0)
    q, k, v = pl.pallas_call(
        functools.partial(_qkv_kernel, d=d, nh=nh, tm=tm, qscale=qscale),
        grid=(b, s // tm),
        in_specs=[
            pl.BlockSpec((None, tm, d), lambda i, j: (i, j, 0)),
            pl.BlockSpec((1, d), const2),
            pl.BlockSpec((d, 3 * d), const2, pipeline_mode=pl.Buffered(1)),
            pl.BlockSpec((nh, 2, 2, dh2), lambda i, j: (0, 0, 0, 0)),
            pl.BlockSpec((nh, 2, 3, dh2), lambda i, j: (0, 0, 0, 0)),
        ],
        out_specs=[
            pl.BlockSpec((None, nh, 2, tm, dh2), lambda i, j: (i, 0, 0, j, 0)),
            pl.BlockSpec((None, nh, 2, tm, dh2), lambda i, j: (i, 0, 0, j, 0)),
            pl.BlockSpec((None, nh, tm, 2 * dv), lambda i, j: (i, 0, j, 0)),
        ],
        out_shape=[
            jax.ShapeDtypeStruct((b, nh, 2, s, dh2), jnp.bfloat16),
            jax.ShapeDtypeStruct((b, nh, 2, s, dh2), jnp.bfloat16),
            jax.ShapeDtypeStruct((b, nh, s, 2 * dv), jnp.bfloat16),
        ],
        compiler_params=pltpu.CompilerParams(
            dimension_semantics=("parallel", "parallel"), vmem_limit_bytes=VMEM_LIMIT_BYTES),
        name="qkv_proj",
    )(hin, g_pre.reshape(1, d), w_qkv.astype(jnp.bfloat16), qa, ka)

    tq, ck, nsub = ATT_TQ, ATT_CK, ATT_NSUB
    wdt = ck + (nsub - 1) * tq
    dist = np.abs(np.arange(tq)[:, None] - np.arange(wdt)[None, :] + (nsub - 1) * tq)
    bias = jnp.asarray(-cvals[:, None, None] * dist[None], jnp.float32)
    lam_p = jnp.stack([lq1, lk1, lq2, lk2]).astype(jnp.float32)

    o = pl.pallas_call(
        functools.partial(_attn_kernel, s=s, dv=dv, lam_init=lam_init),
        grid=(b, nh, s // tq),
        in_specs=[
            pl.BlockSpec(memory_space=pltpu.SMEM),
            pl.BlockSpec((None, None, 2, tq, dh2), lambda i, h, j: (i, h, 0, j, 0)),
            pl.BlockSpec((None, None, 2, s, dh2), lambda i, h, j: (i, h, 0, 0, 0)),
            pl.BlockSpec((None, None, s, 2 * dv), lambda i, h, j: (i, h, 0, 0)),
            pl.BlockSpec((None, tq, wdt), lambda i, h, j: (h, 0, 0)),
            pl.BlockSpec((4, dh), lambda i, h, j: (0, 0)),
            pl.BlockSpec((1, dv), lambda i, h, j: (0, 0)),
        ],
        out_specs=pl.BlockSpec((None, tq, dv), lambda i, h, j: (i, j, h)),
        out_shape=jax.ShapeDtypeStruct((b, s, nh * dv), jnp.bfloat16),
        scratch_shapes=[
            pltpu.VMEM((2, tq, 1), jnp.float32),
            pltpu.VMEM((2, tq, 2 * dv), jnp.float32),
        ],
        compiler_params=pltpu.CompilerParams(
            dimension_semantics=("parallel", "parallel", "arbitrary"),
            vmem_limit_bytes=VMEM_LIMIT_BYTES),
        name="diff_attention",
    )(jnp.asarray(c3, jnp.float32), q, k, v, bias, lam_p, subln_g.reshape(1, dv))

    t = b * s
    tmo = min(TOKEN_TILE, t)
    const1 = lambda i: (0, 0)
    out = pl.pallas_call(
        _proj_post_kernel,
        grid=(t // tmo,),
        in_specs=[
            pl.BlockSpec((tmo, d), lambda i: (i, 0)),
            pl.BlockSpec((d, d), const1),
            pl.BlockSpec((1, d), const1),
            pl.BlockSpec((tmo, d), lambda i: (i, 0)),
        ],
        out_specs=pl.BlockSpec((tmo, d), lambda i: (i, 0)),
        out_shape=jax.ShapeDtypeStruct((t, d), jnp.float32),
        compiler_params=pltpu.CompilerParams(
            dimension_semantics=("parallel",), vmem_limit_bytes=VMEM_LIMIT_BYTES),
        name="attn_out_proj",
    )(o.reshape(t, d), w_o.astype(jnp.bfloat16), g_post.reshape(1, d), hin.reshape(t, d))
    return out.reshape(b, s, d)


def kernel(x, norm_mix_pre, norm_mix_post, norm_ffn_pre, norm_ffn_post, fourier_w_o, diff_w_qkv,
           diff_lambda_q1, diff_lambda_k1, diff_lambda_q2, diff_lambda_k2, diff_subln_g, diff_w_o,
           ffn_w_gate, ffn_w_up, ffn_w_down):
    depth = norm_mix_pre.shape[0]
    h = x
    for i in range(depth):
        j = i // N_MIXERS
        if i % N_MIXERS == 0:
            h = _fourier_layer(h, norm_mix_pre[i], norm_mix_post[i], fourier_w_o[j])
        else:
            h = _diff_attention_layer(h, norm_mix_pre[i], norm_mix_post[i], diff_w_qkv[j],
                                      diff_lambda_q1[j], diff_lambda_k1[j], diff_lambda_q2[j],
                                      diff_lambda_k2[j], diff_subln_g[j], diff_w_o[j], i)
        h = _ffn_layer(h, norm_ffn_pre[i], norm_ffn_post[i], ffn_w_gate[i], ffn_w_up[i], ffn_w_down[i])
    return h
```

```python
import functools
import math

import numpy as np
import jax
import jax.numpy as jnp
from jax import lax
from jax.experimental import pallas as pl
from jax.experimental.pallas import tpu as pltpu

N_FOURIER_GROUPS = 8
N_DIFF_HEADS = 8
RMS_EPS = 1e-6
N_MIXERS = 2

LANES = 128
VMEM_LIMIT_BYTES = 56 * 1024 * 1024

FFT_N2 = 128
FFT_S2_PER_STEP = 8
TOKEN_TILE = 512
FFN_CHUNK = 256
ATT_TQ = 256
ATT_CK = 1024
ATT_NSUB = ATT_CK // ATT_TQ
LOG2E = 1.4426950408889634


def _rms_scale(x):
    return lax.rsqrt(jnp.mean(x * x, axis=-1, keepdims=True) + RMS_EPS)


def _np_bf16(v):
    return np.asarray(v, np.float32).astype(jnp.bfloat16)


def _bf16_split3(v):
    v = np.asarray(v, np.float64)
    hi = _np_bf16(v).astype(np.float64)
    mid = _np_bf16(v - hi).astype(np.float64)
    lo = _np_bf16(v - hi - mid).astype(np.float64)
    return hi, mid, lo


def _fft_stage1_kernel(x_ref, g_ref, f1_ref, ar_ref, ai_ref, *, n1, d, nb):
    g = g_ref[...]
    f1 = f1_ref[...]
    for j in range(nb):
        xs = x_ref[:, j * d:(j + 1) * d]
        xn = (xs * _rms_scale(xs) * g).astype(jnp.bfloat16)
        a = jnp.dot(f1, xn, preferred_element_type=jnp.float32)
        ar_ref[:, j * d:(j + 1) * d] = a[:n1].astype(jnp.bfloat16)
        ai_ref[:, j * d:(j + 1) * d] = a[n1:].astype(jnp.bfloat16)


def _fft_stage2_kernel(ar_ref, ai_ref, gm_ref, cs_ref, wo_ref, gpost_ref, xres_ref, out_ref, *, n2, d, cg):
    a = jnp.concatenate([ar_ref[...], ai_ref[...]], axis=0)
    y = jnp.dot(gm_ref[...], a, preferred_element_type=jnp.float32)
    yr = y[:n2].astype(jnp.bfloat16)
    yi = y[n2:].astype(jnp.bfloat16)
    cs = cs_ref[...]
    cols = []
    for gi in range(d // cg):
        lhs = jnp.concatenate([yr[:, gi * cg:(gi + 1) * cg], yi[:, gi * cg:(gi + 1) * cg]], axis=1)
        cols.append(jnp.dot(lhs, cs, preferred_element_type=jnp.float32))
    z = jnp.concatenate(cols, axis=1).astype(jnp.bfloat16)
    m = jnp.dot(z, wo_ref[...], preferred_element_type=jnp.float32)
    out_ref[...] = xres_ref[...] + m * _rms_scale(m) * gpost_ref[...]


def _fourier_layer(x, g_pre, g_post, w_o):
    b, s, d = x.shape
    n2 = FFT_N2
    n1 = s // n2
    cg = d // N_FOURIER_GROUPS
    nb = FFT_S2_PER_STEP
    assert n1 * n2 == s and n2 % nb == 0

    s1 = np.arange(n1)
    th1 = 2.0 * np.pi * ((s1[:, None] * s1[None, :]) % n1) / n1
    f1 = np.concatenate([np.cos(th1), -np.sin(th1)], axis=0)
    s2 = np.arange(n2)
    kk = (np.arange(n1)[:, None, None] + n1 * np.arange(n2)[None, :, None])
    th2 = 2.0 * np.pi * ((kk * s2[None, None, :]) % s) / s
    gr, gi = np.cos(th2), -np.sin(th2)
    gm = np.concatenate([np.concatenate([gr, -gi], axis=2),
                         np.concatenate([gi, gr], axis=2)], axis=1)
    c = np.arange(cg)
    thc = 2.0 * np.pi * ((c[:, None] * c[None, :]) % cg) / cg
    norm = 1.0 / math.sqrt(s * cg)
    cs = np.concatenate([np.cos(thc), np.sin(thc)], axis=0) * norm

    f1 = jnp.asarray(_np_bf16(f1))
    gm = jnp.asarray(_np_bf16(gm))
    cs = jnp.asarray(_np_bf16(cs))

    x3 = x.reshape(b, n1, n2 * d)
    ar, ai = pl.pallas_call(
        functools.partial(_fft_stage1_kernel, n1=n1, d=d, nb=nb),
        grid=(b, n2 // nb),
        in_specs=[
            pl.BlockSpec((None, n1, nb * d), lambda i, j: (i, 0, j)),
            pl.BlockSpec((1, d), lambda i, j: (0, 0)),
            pl.BlockSpec((2 * n1, n1), lambda i, j: (0, 0)),
        ],
        out_specs=[
            pl.BlockSpec((None, n1, nb * d), lambda i, j: (i, 0, j)),
            pl.BlockSpec((None, n1, nb * d), lambda i, j: (i, 0, j)),
        ],
        out_shape=[jax.ShapeDtypeStruct((b, n1, n2 * d), jnp.bfloat16)] * 2,
        compiler_params=pltpu.CompilerParams(
            dimension_semantics=("parallel", "parallel"), vmem_limit_bytes=VMEM_LIMIT_BYTES),
        name="fft_stage1",
    )(x3, g_pre.reshape(1, d), f1)

    ar4 = ar.reshape(b, n1, n2, d)
    ai4 = ai.reshape(b, n1, n2, d)
    xk = x.reshape(b, n2, n1 * d)
    out = pl.pallas_call(
        functools.partial(_fft_stage2_kernel, n2=n2, d=d, cg=cg),
        grid=(b, n1),
        in_specs=[
            pl.BlockSpec((None, None, n2, d), lambda i, j: (i, j, 0, 0)),
            pl.BlockSpec((None, None, n2, d), lambda i, j: (i, j, 0, 0)),
            pl.BlockSpec((None, 2 * n2, 2 * n2), lambda i, j: (j, 0, 0)),
            pl.BlockSpec((2 * cg, cg), lambda i, j: (0, 0)),
            pl.BlockSpec((d, d), lambda i, j: (0, 0)),
            pl.BlockSpec((1, d), lambda i, j: (0, 0)),
            pl.BlockSpec((None, n2, d), lambda i, j: (i, 0, j)),
        ],
        out_specs=pl.BlockSpec((None, n2, d), lambda i, j: (i, 0, j)),
        out_shape=jax.ShapeDtypeStruct((b, n2, n1 * d), jnp.float32),
        compiler_params=pltpu.CompilerParams(
            dimension_semantics=("parallel", "parallel"), vmem_limit_bytes=VMEM_LIMIT_BYTES),
        name="fft_stage2",
    )(ar4, ai4, gm, cs, w_o.astype(jnp.bfloat16), g_post.reshape(1, d), xk)
    return out.reshape(b, s, d)


def _ffn_kernel(x_ref, gpre_ref, gpost_ref, wg_ref, wu_ref, wd_ref, out_ref, *, f, chunk):
    x = x_ref[...]
    xn = (x * _rms_scale(x) * gpre_ref[...]).astype(jnp.bfloat16)
    acc = jnp.zeros(x.shape, jnp.float32)
    for c0 in range(0, f, chunk):
        c1 = min(c0 + chunk, f)
        gt = jnp.dot(xn, wg_ref[:, c0:c1], preferred_element_type=jnp.float32)
        up = jnp.dot(xn, wu_ref[:, c0:c1], preferred_element_type=jnp.float32)
        hh = (gt * jax.nn.sigmoid(gt) * up).astype(jnp.bfloat16)
        acc = acc + jnp.dot(hh, wd_ref[c0:c1, :], preferred_element_type=jnp.float32)
    out_ref[...] = x + acc * _rms_scale(acc) * gpost_ref[...]


def _ffn_layer(h, g_pre, g_post, w_gate, w_up, w_down):
    b, s, d = h.shape
    f = w_gate.shape[1]
    t = b * s
    tm = min(TOKEN_TILE, t)
    assert t % tm == 0
    const = lambda i: (0, 0)
    out = pl.pallas_call(
        functools.partial(_ffn_kernel, f=f, chunk=FFN_CHUNK),
        grid=(t // tm,),
        in_specs=[
            pl.BlockSpec((tm, d), lambda i: (i, 0)),
            pl.BlockSpec((1, d), const),
            pl.BlockSpec((1, d), const),
            pl.BlockSpec((d, f), const, pipeline_mode=pl.Buffered(1)),
            pl.BlockSpec((d, f), const, pipeline_mode=pl.Buffered(1)),
            pl.BlockSpec((f, d), const, pipeline_mode=pl.Buffered(1)),
        ],
        out_specs=pl.BlockSpec((tm, d), lambda i: (i, 0)),
        out_shape=jax.ShapeDtypeStruct((t, d), jnp.float32),
        compiler_params=pltpu.CompilerParams(
            dimension_semantics=("parallel",), vmem_limit_bytes=VMEM_LIMIT_BYTES),
        name="swiglu_ffn",
    )(h.reshape(t, d), g_pre.reshape(1, d), g_post.reshape(1, d),
      w_gate.astype(jnp.bfloat16), w_up.astype(jnp.bfloat16), w_down.astype(jnp.bfloat16))
    return out.reshape(b, s, d)


def _qkv_kernel(x_ref, gpre_ref, w_ref, qa_ref, ka_ref, q_ref, k_ref, v_ref, *, d, nh, tm, qscale):
    x = x_ref[...]
    xn = (x * _rms_scale(x) * gpre_ref[...]).astype(jnp.bfloat16)
    dh2 = d // nh
    row = pl.program_id(1) * tm + lax.broadcasted_iota(jnp.int32, (tm, 1), 0)
    ir = (row & (ATT_TQ - 1)).astype(jnp.float32)
    jr = ir
    js = (row & (ATT_CK - ATT_TQ)).astype(jnp.float32)
    lane = lax.broadcasted_iota(jnp.int32, (1, dh2), 1)
    half = dh2 // 2
    q_all = jnp.dot(xn, w_ref[:, 0:d], preferred_element_type=jnp.float32) * qscale
    k_all = jnp.dot(xn, w_ref[:, d:2 * d], preferred_element_type=jnp.float32)
    v_all = jnp.dot(xn, w_ref[:, 2 * d:3 * d], preferred_element_type=jnp.float32)
    ones_col = jnp.where(lane == 0, 1.0, 0.0).astype(jnp.bfloat16)
    for h in range(nh):
        qh = q_all[:, h * dh2:(h + 1) * dh2]
        kh = k_all[:, h * dh2:(h + 1) * dh2]
        for comp in range(2):
            data = (lane < half) if comp == 0 else (lane >= half)
            qa = qa_ref[h, comp]
            ka = ka_ref[h, comp]
            q_aug = qa[0:1] + qa[1:2] * ir
            k_aug = ka[0:1] + ka[1:2] * jr + ka[2:3] * js
            q_ref[h, comp] = jnp.where(data, qh, q_aug).astype(jnp.bfloat16)
            k_ref[h, comp] = jnp.where(data, kh, k_aug).astype(jnp.bfloat16)
        vh = v_all[:, h * dh2:(h + 1) * dh2].astype(jnp.bfloat16)
        v_ref[h] = jnp.concatenate([vh, jnp.broadcast_to(ones_col, (tm, dh2))], axis=1)


def _aug_tables(nh, dh2, cvals):
    half = dh2 // 2
    qa = np.zeros((nh, 2, 2, dh2), np.float64)
    ka = np.zeros((nh, 2, 3, dh2), np.float64)
    for h in range(nh):
        pieces = _bf16_split3(cvals[h])
        for comp in range(2):
            base = half if comp == 0 else 0
            for p in range(3):
                qa[h, comp, 0, base + p] = pieces[p]
                ka[h, comp, 1, base + p] = 1.0
                qa[h, comp, 0, base + 3 + p] = pieces[p]
                ka[h, comp, 2, base + 3 + p] = 1.0
                qa[h, comp, 1, base + 6 + p] = -1.0
                ka[h, comp, 0, base + 6 + p] = pieces[p]
    return jnp.asarray(qa, jnp.float32), jnp.asarray(ka, jnp.float32)


def _attn_kernel(c_ref, q_ref, k_ref, v_ref, bias_ref, lam_ref, g_ref, o_ref, m_scr, acc_scr,
                 *, s, dv, lam_init):
    tq, ck, nsub = ATT_TQ, ATT_CK, ATT_NSUB
    nch = s // ck
    hd = pl.program_id(1)
    qi = pl.program_id(2)
    c = c_ref[hd]
    i0 = qi * tq
    cd = qi // nsub
    dsub = qi % nsub
    lane = lax.broadcasted_iota(jnp.int32, (1, LANES), 1)
    half = LANES // 2

    m_scr[...] = jnp.full(m_scr.shape, -1e30, jnp.float32)
    acc_scr[...] = jnp.zeros(acc_scr.shape, jnp.float32)

    q_left, q_right, q_diag = [], [], []
    for comp in range(2):
        data = (lane < half) if comp == 0 else (lane >= half)
        qc = q_ref[comp]
        q_left.append(qc)
        q_right.append(jnp.where(data, qc, -qc))
        q_diag.append(jnp.where(data, qc, jnp.zeros_like(qc)))

    def chunk(cj, mode):
        j0 = pl.multiple_of(cj * ck, ck)
        vc = v_ref[pl.ds(j0, ck), :]
        for comp in range(2):
            kc = k_ref[comp, pl.ds(j0, ck), :]
            if mode == "D":
                qv = q_diag[comp]
            else:
                qv = jnp.where(cj < cd, q_left[comp], q_right[comp])
            t = lax.dot_general(qv, kc, (((1,), (1,)), ((), ())),
                                preferred_element_type=jnp.float32)
            if mode == "D":
                off = pl.multiple_of((nsub - 1 - dsub) * tq, tq)
                t = t + bias_ref[:, pl.ds(off, ck)]
                kappa = jnp.float32(0.0)
            else:
                kappa = -c * jnp.abs(i0 - j0).astype(jnp.float32)
            m_old = m_scr[comp]
            m_new = jnp.maximum(m_old, jnp.max(t, axis=-1, keepdims=True) + kappa)
            alpha = jnp.exp2(m_old - m_new)
            p = jnp.exp2(t - (m_new - kappa)).astype(jnp.bfloat16)
            acc_scr[comp] = alpha * acc_scr[comp] + jnp.dot(p, vc, preferred_element_type=jnp.float32)
            m_scr[comp] = m_new

    chunk(cd, "D")
    for n in range(1, nch):
        cj = cd + n
        chunk(jnp.where(cj >= nch, cj - nch, cj), "G")

    lam_p = lam_ref[...]
    lam = (jnp.exp(jnp.sum(lam_p[0:1] * lam_p[1:2], axis=-1, keepdims=True))
           - jnp.exp(jnp.sum(lam_p[2:3] * lam_p[3:4], axis=-1, keepdims=True)) + lam_init)
    a0 = acc_scr[0]
    a1 = acc_scr[1]
    o = a0[:, :dv] / a0[:, dv:dv + 1] - lam * (a1[:, :dv] / a1[:, dv:dv + 1])
    o = o * _rms_scale(o) * g_ref[...] * (1.0 - lam_init)
    o_ref[...] = o.astype(jnp.bfloat16)


def _proj_post_kernel(a_ref, w_ref, gpost_ref, xres_ref, out_ref):
    m = jnp.dot(a_ref[...], w_ref[...], preferred_element_type=jnp.float32)
    out_ref[...] = xres_ref[...] + m * _rms_scale(m) * gpost_ref[...]


def _diff_attention_layer(hin, g_pre, g_post, w_qkv, lq1, lk1, lq2, lk2, subln_g, w_o, layer_idx):
    b, s, d = hin.shape
    nh = N_DIFF_HEADS
    dh2 = d // nh
    dh = dh2 // 2
    dv = dh2
    assert dh2 == LANES and s % ATT_CK == 0
    lam_init = 0.8 - 0.6 * math.exp(-0.3 * layer_idx)
    slopes = np.asarray([2.0 ** (-8.0 * (i + 1) / nh) for i in range(nh)], np.float64)
    cvals = slopes * LOG2E
    qa, ka = _aug_tables(nh, dh2, cvals)
    c3 = np.sum(np.stack(_bf16_split3(cvals)), axis=0)
    qscale = dh ** -0.5 * LOG2E

    tm = min(TOKEN_TILE, s)
    const2 = lambda i, j: (0, 0)
    q, k, v = pl.pallas_call(
        functools.partial(_qkv_kernel, d=d, nh=nh, tm=tm, qscale=qscale),
        grid=(b, s // tm),
        in_specs=[
            pl.BlockSpec((None, tm, d), lambda i, j: (i, j, 0)),
            pl.BlockSpec((1, d), const2),
            pl.BlockSpec((d, 3 * d), const2, pipeline_mode=pl.Buffered(1)),
            pl.BlockSpec((nh, 2, 2, dh2), lambda i, j: (0, 0, 0, 0)),
            pl.BlockSpec((nh, 2, 3, dh2), lambda i, j: (0, 0, 0, 0)),
        ],
        out_specs=[
            pl.BlockSpec((None, nh, 2, tm, dh2), lambda i, j: (i, 0, 0, j, 0)),
            pl.BlockSpec((None, nh, 2, tm, dh2), lambda i, j: (i, 0, 0, j, 0)),
            pl.BlockSpec((None, nh, tm, 2 * dv), lambda i, j: (i, 0, j, 0)),
        ],
        out_shape=[
            jax.ShapeDtypeStruct((b, nh, 2, s, dh2), jnp.bfloat16),
            jax.ShapeDtypeStruct((b, nh, 2, s, dh2), jnp.bfloat16),
            jax.ShapeDtypeStruct((b, nh, s, 2 * dv), jnp.bfloat16),
        ],
        compiler_params=pltpu.CompilerParams(
            dimension_semantics=("parallel", "parallel"), vmem_limit_bytes=VMEM_LIMIT_BYTES),
        name="qkv_proj",
    )(hin, g_pre.reshape(1, d), w_qkv.astype(jnp.bfloat16), qa, ka)

    tq, ck, nsub = ATT_TQ, ATT_CK, ATT_NSUB
    wdt = ck + (nsub - 1) * tq
    dist = np.abs(np.arange(tq)[:, None] - np.arange(wdt)[None, :] + (nsub - 1) * tq)
    bias = jnp.asarray(-cvals[:, None, None] * dist[None], jnp.float32)
    lam_p = jnp.stack([lq1, lk1, lq2, lk2]).astype(jnp.float32)

    o = pl.pallas_call(
        functools.partial(_attn_kernel, s=s, dv=dv, lam_init=lam_init),
        grid=(b, nh, s // tq),
        in_specs=[
            pl.BlockSpec(memory_space=pltpu.SMEM),
            pl.BlockSpec((None, None, 2, tq, dh2), lambda i, h, j: (i, h, 0, j, 0)),
            pl.BlockSpec((None, None, 2, s, dh2), lambda i, h, j: (i, h, 0, 0, 0)),
            pl.BlockSpec((None, None, s, 2 * dv), lambda i, h, j: (i, h, 0, 0)),
            pl.BlockSpec((None, tq, wdt), lambda i, h, j: (h, 0, 0)),
            pl.BlockSpec((4, dh), lambda i, h, j: (0, 0)),
            pl.BlockSpec((1, dv), lambda i, h, j: (0, 0)),
        ],
        out_specs=pl.BlockSpec((None, tq, dv), lambda i, h, j: (i, j, h)),
        out_shape=jax.ShapeDtypeStruct((b, s, nh * dv), jnp.bfloat16),
        scratch_shapes=[
            pltpu.VMEM((2, tq, 1), jnp.float32),
            pltpu.VMEM((2, tq, 2 * dv), jnp.float32),
        ],
        compiler_params=pltpu.CompilerParams(
            dimension_semantics=("parallel", "parallel", "arbitrary"),
            vmem_limit_bytes=VMEM_LIMIT_BYTES),
        name="diff_attention",
    )(jnp.asarray(c3, jnp.float32), q, k, v, bias, lam_p, subln_g.reshape(1, dv))

    t = b * s
    tmo = min(TOKEN_TILE, t)
    const1 = lambda i: (0, 0)
    out = pl.pallas_call(
        _proj_post_kernel,
        grid=(t // tmo,),
        in_specs=[
            pl.BlockSpec((tmo, d), lambda i: (i, 0)),
            pl.BlockSpec((d, d), const1),
            pl.BlockSpec((1, d), const1),
            pl.BlockSpec((tmo, d), lambda i: (i, 0)),
        ],
        out_specs=pl.BlockSpec((tmo, d), lambda i: (i, 0)),
        out_shape=jax.ShapeDtypeStruct((t, d), jnp.float32),
        compiler_params=pltpu.CompilerParams(
            dimension_semantics=("parallel",), vmem_limit_bytes=VMEM_LIMIT_BYTES),
        name="attn_out_proj",
    )(o.reshape(t, d), w_o.astype(jnp.bfloat16), g_post.reshape(1, d), hin.reshape(t, d))
    return out.reshape(b, s, d)


def kernel(x, norm_mix_pre, norm_mix_post, norm_ffn_pre, norm_ffn_post, fourier_w_o, diff_w_qkv,
           diff_lambda_q1, diff_lambda_k1, diff_lambda_q2, diff_lambda_k2, diff_subln_g, diff_w_o,
           ffn_w_gate, ffn_w_up, ffn_w_down):
    depth = norm_mix_pre.shape[0]
    h = x
    for i in range(depth):
        j = i // N_MIXERS
        if i % N_MIXERS == 0:
            h = _fourier_layer(h, norm_mix_pre[i], norm_mix_post[i], fourier_w_o[j])
        else:
            h = _diff_attention_layer(h, norm_mix_pre[i], norm_mix_post[i], diff_w_qkv[j],
                                      diff_lambda_q1[j], diff_lambda_k1[j], diff_lambda_q2[j],
                                      diff_lambda_k2[j], diff_subln_g[j], diff_w_o[j], i)
        h = _ffn_layer(h, norm_ffn_pre[i], norm_ffn_post[i], ffn_w_gate[i], ffn_w_up[i], ffn_w_down[i])
    return h
```

```python
import functools
import math

import numpy as np
import jax
import jax.numpy as jnp
from jax import lax
from jax.experimental import pallas as pl
from jax.experimental.pallas import tpu as pltpu

N_FOURIER_GROUPS = 8
N_DIFF_HEADS = 8
RMS_EPS = 1e-6
N_MIXERS = 2

LANES = 128
VMEM_LIMIT_BYTES = 56 * 1024 * 1024

FFT_N2 = 128
FFT_S2_PER_STEP = 16
FFT_K1_PER_STEP = 8
TOKEN_TILE = 512
FFN_CHUNK = 256
ATT_TQ = 512
ATT_CK = 1024
ATT_NSUB = ATT_CK // ATT_TQ
POS_SPLIT = 256
LOG2E = 1.4426950408889634


def _rms_scale(x):
    return lax.rsqrt(jnp.mean(x * x, axis=-1, keepdims=True) + RMS_EPS)


def _np_bf16(v):
    return np.asarray(v, np.float32).astype(jnp.bfloat16)


def _bf16_split3(v):
    v = np.asarray(v, np.float64)
    hi = _np_bf16(v).astype(np.float64)
    mid = _np_bf16(v - hi).astype(np.float64)
    lo = _np_bf16(v - hi - mid).astype(np.float64)
    return hi, mid, lo


def _fft_stage1_kernel(x_ref, g_ref, f1k_ref, ar_ref, ai_ref, *, n1, d, nb):
    x = x_ref[...].reshape(n1 * nb, d)
    xn = (x * _rms_scale(x) * g_ref[...]).astype(jnp.bfloat16)
    a = jnp.dot(f1k_ref[...], xn, preferred_element_type=jnp.float32)
    ar_ref[...] = a[:n1 * nb].astype(jnp.bfloat16).reshape(n1, nb, d)
    ai_ref[...] = a[n1 * nb:].astype(jnp.bfloat16).reshape(n1, nb, d)


def _fft_stage2_kernel(ar_ref, ai_ref, gm_ref, cs_ref, perm_ref, wo_ref, gpost_ref, xres_ref, out_ref,
                       *, n2, d, cg, kb):
    cs = cs_ref[...]
    zs = []
    for i in range(kb):
        a = jnp.concatenate([ar_ref[i], ai_ref[i]], axis=0)
        y = jnp.dot(gm_ref[i], a, preferred_element_type=jnp.float32)
        yr = y[:n2].astype(jnp.bfloat16)
        yi = y[n2:].astype(jnp.bfloat16)
        cols = []
        for gi in range(d // cg):
            lhs = jnp.concatenate([yr[:, gi * cg:(gi + 1) * cg], yi[:, gi * cg:(gi + 1) * cg]], axis=1)
            cols.append(jnp.dot(lhs, cs, preferred_element_type=jnp.float32))
        zs.append(jnp.concatenate(cols, axis=1).astype(jnp.bfloat16))
    z = jnp.concatenate(zs, axis=0)
    z = jnp.dot(perm_ref[...], z, preferred_element_type=jnp.float32).astype(jnp.bfloat16)
    m = jnp.dot(z, wo_ref[...], preferred_element_type=jnp.float32)
    r = m * _rms_scale(m) * gpost_ref[...]
    out_ref[...] = xres_ref[...] + r.reshape(n2, kb, d)


def _fourier_layer(x, g_pre, g_post, w_o):
    b, s, d = x.shape
    n2 = FFT_N2
    n1 = s // n2
    cg = d // N_FOURIER_GROUPS
    nb = FFT_S2_PER_STEP
    assert n1 * n2 == s and n2 % nb == 0

    s1 = np.arange(n1)
    th1 = 2.0 * np.pi * ((s1[:, None] * s1[None, :]) % n1) / n1
    f1 = np.concatenate([np.cos(th1), -np.sin(th1)], axis=0)
    s2 = np.arange(n2)
    kk = (np.arange(n1)[:, None, None] + n1 * np.arange(n2)[None, :, None])
    th2 = 2.0 * np.pi * ((kk * s2[None, None, :]) % s) / s
    gr, gi = np.cos(th2), -np.sin(th2)
    gm = np.concatenate([np.concatenate([gr, -gi], axis=2),
                         np.concatenate([gi, gr], axis=2)], axis=1)
    c = np.arange(cg)
    thc = 2.0 * np.pi * ((c[:, None] * c[None, :]) % cg) / cg
    norm = 1.0 / math.sqrt(s * cg)
    cs = np.concatenate([np.cos(thc), np.sin(thc)], axis=0) * norm

    f1k = jnp.asarray(_np_bf16(np.kron(f1, np.eye(nb))))
    gm = jnp.asarray(_np_bf16(gm))
    cs = jnp.asarray(_np_bf16(cs))

    x4 = x.reshape(b, n1, n2, d)
    ar, ai = pl.pallas_call(
        functools.partial(_fft_stage1_kernel, n1=n1, d=d, nb=nb),
        grid=(b, n2 // nb),
        in_specs=[
            pl.BlockSpec((None, n1, nb, d), lambda i, j: (i, 0, j, 0)),
            pl.BlockSpec((1, d), lambda i, j: (0, 0)),
            pl.BlockSpec((2 * n1 * nb, n1 * nb), lambda i, j: (0, 0), pipeline_mode=pl.Buffered(1)),
        ],
        out_specs=[
            pl.BlockSpec((None, n1, nb, d), lambda i, j: (i, 0, j, 0)),
            pl.BlockSpec((None, n1, nb, d), lambda i, j: (i, 0, j, 0)),
        ],
        out_shape=[jax.ShapeDtypeStruct((b, n1, n2, d), jnp.bfloat16)] * 2,
        compiler_params=pltpu.CompilerParams(
            dimension_semantics=("parallel", "parallel"), vmem_limit_bytes=VMEM_LIMIT_BYTES),
        name="fft_stage1",
    )(x4, g_pre.reshape(1, d), f1k)

    kb = min(FFT_K1_PER_STEP, n1)
    assert n1 % kb == 0
    src = (np.arange(kb)[None, :] * n2 + np.arange(n2)[:, None]).reshape(-1)
    perm = np.zeros((kb * n2, kb * n2), np.float32)
    perm[np.arange(kb * n2), src] = 1.0
    perm = jnp.asarray(_np_bf16(perm))
    xk = x.reshape(b, n2, n1, d)
    out = pl.pallas_call(
        functools.partial(_fft_stage2_kernel, n2=n2, d=d, cg=cg, kb=kb),
        grid=(b, n1 // kb),
        in_specs=[
            pl.BlockSpec((None, kb, n2, d), lambda i, j: (i, j, 0, 0)),
            pl.BlockSpec((None, kb, n2, d), lambda i, j: (i, j, 0, 0)),
            pl.BlockSpec((kb, 2 * n2, 2 * n2), lambda i, j: (j, 0, 0)),
            pl.BlockSpec((2 * cg, cg), lambda i, j: (0, 0)),
            pl.BlockSpec((kb * n2, kb * n2), lambda i, j: (0, 0)),
            pl.BlockSpec((d, d), lambda i, j: (0, 0)),
            pl.BlockSpec((1, d), lambda i, j: (0, 0)),
            pl.BlockSpec((None, n2, kb, d), lambda i, j: (i, 0, j, 0)),
        ],
        out_specs=pl.BlockSpec((None, n2, kb, d), lambda i, j: (i, 0, j, 0)),
        out_shape=jax.ShapeDtypeStruct((b, n2, n1, d), jnp.float32),
        compiler_params=pltpu.CompilerParams(
            dimension_semantics=("parallel", "parallel"), vmem_limit_bytes=VMEM_LIMIT_BYTES),
        name="fft_stage2",
    )(ar, ai, gm, cs, perm, w_o.astype(jnp.bfloat16), g_post.reshape(1, d), xk)
    return out.reshape(b, s, d)


def _ffn_kernel(x_ref, gpre_ref, gpost_ref, wg_ref, wu_ref, wd_ref, out_ref, *, f, chunk):
    x = x_ref[...]
    xn = (x * _rms_scale(x) * gpre_ref[...]).astype(jnp.bfloat16)
    acc = jnp.zeros(x.shape, jnp.float32)
    for c0 in range(0, f, chunk):
        c1 = min(c0 + chunk, f)
        gt = jnp.dot(xn, wg_ref[:, c0:c1], preferred_element_type=jnp.float32)
        up = jnp.dot(xn, wu_ref[:, c0:c1], preferred_element_type=jnp.float32)
        hh = (gt * jax.nn.sigmoid(gt) * up).astype(jnp.bfloat16)
        acc = acc + jnp.dot(hh, wd_ref[c0:c1, :], preferred_element_type=jnp.float32)
    out_ref[...] = x + acc * _rms_scale(acc) * gpost_ref[...]


def _ffn_layer(h, g_pre, g_post, w_gate, w_up, w_down):
    b, s, d = h.shape
    f = w_gate.shape[1]
    t = b * s
    tm = min(TOKEN_TILE, t)
    assert t % tm == 0
    const = lambda i: (0, 0)
    out = pl.pallas_call(
        functools.partial(_ffn_kernel, f=f, chunk=FFN_CHUNK),
        grid=(t // tm,),
        in_specs=[
            pl.BlockSpec((tm, d), lambda i: (i, 0)),
            pl.BlockSpec((1, d), const),
            pl.BlockSpec((1, d), const),
            pl.BlockSpec((d, f), const, pipeline_mode=pl.Buffered(1)),
            pl.BlockSpec((d, f), const, pipeline_mode=pl.Buffered(1)),
            pl.BlockSpec((f, d), const, pipeline_mode=pl.Buffered(1)),
        ],
        out_specs=pl.BlockSpec((tm, d), lambda i: (i, 0)),
        out_shape=jax.ShapeDtypeStruct((t, d), jnp.float32),
        compiler_params=pltpu.CompilerParams(
            dimension_semantics=("parallel",), vmem_limit_bytes=VMEM_LIMIT_BYTES),
        name="swiglu_ffn",
    )(h.reshape(t, d), g_pre.reshape(1, d), g_post.reshape(1, d),
      w_gate.astype(jnp.bfloat16), w_up.astype(jnp.bfloat16), w_down.astype(jnp.bfloat16))
    return out.reshape(b, s, d)


def _qkv_kernel(x_ref, gpre_ref, w_ref, qa_ref, ka_ref, q_ref, k_ref, v_ref, *, d, nh, tm, qscale):
    x = x_ref[...]
    xn = (x * _rms_scale(x) * gpre_ref[...]).astype(jnp.bfloat16)
    dh2 = d // nh
    row = pl.program_id(1) * tm + lax.broadcasted_iota(jnp.int32, (tm, 1), 0)
    pos_lo = (row & (POS_SPLIT - 1)).astype(jnp.float32)
    q_hi = (row & (ATT_TQ - POS_SPLIT)).astype(jnp.float32)
    k_hi = (row & (ATT_CK - POS_SPLIT)).astype(jnp.float32)
    lane = lax.broadcasted_iota(jnp.int32, (1, dh2), 1)
    half = dh2 // 2
    q_all = jnp.dot(xn, w_ref[:, 0:d], preferred_element_type=jnp.float32) * qscale
    k_all = jnp.dot(xn, w_ref[:, d:2 * d], preferred_element_type=jnp.float32)
    v_all = jnp.dot(xn, w_ref[:, 2 * d:3 * d], preferred_element_type=jnp.float32)
    ones_col = jnp.where(lane == 0, 1.0, 0.0).astype(jnp.bfloat16)
    for h in range(nh):
        qh = q_all[:, h * dh2:(h + 1) * dh2]
        kh = k_all[:, h * dh2:(h + 1) * dh2]
        for comp in range(2):
            data = (lane < half) if comp == 0 else (lane >= half)
            qa = qa_ref[h, comp]
            ka = ka_ref[h, comp]
            q_aug = qa[0:1] + qa[1:2] * pos_lo + qa[2:3] * q_hi
            k_aug = ka[0:1] + ka[1:2] * pos_lo + ka[2:3] * k_hi
            q_ref[h, comp] = jnp.where(data, qh, q_aug).astype(jnp.bfloat16)
            k_ref[h, comp] = jnp.where(data, kh, k_aug).astype(jnp.bfloat16)
        vh = v_all[:, h * dh2:(h + 1) * dh2].astype(jnp.bfloat16)
        v_ref[h] = jnp.concatenate([vh, jnp.broadcast_to(ones_col, (tm, dh2))], axis=1)


def _aug_tables(nh, dh2, cvals):
    half = dh2 // 2
    qa = np.zeros((nh, 2, 3, dh2), np.float64)
    ka = np.zeros((nh, 2, 3, dh2), np.float64)
    for h in range(nh):
        pieces = _bf16_split3(cvals[h])
        for comp in range(2):
            base = half if comp == 0 else 0
            for p in range(3):
                for part in (1, 2):
                    qa[h, comp, 0, base + 3 * (part - 1) + p] = pieces[p]
                    ka[h, comp, part, base + 3 * (part - 1) + p] = 1.0
                    qa[h, comp, part, base + 6 + 3 * (part - 1) + p] = -1.0
                    ka[h, comp, 0, base + 6 + 3 * (part - 1) + p] = pieces[p]
    return jnp.asarray(qa, jnp.float32), jnp.asarray(ka, jnp.float32)


def _attn_kernel(c_ref, q_ref, k_ref, v_ref, bias_ref, lam_ref, g_ref, o_ref, m_scr, acc_scr,
                 *, s, dv, lam_init):
    tq, ck, nsub = ATT_TQ, ATT_CK, ATT_NSUB
    nch = s // ck
    hd = pl.program_id(1)
    qi = pl.program_id(2)
    c = c_ref[hd]
    i0 = qi * tq
    cd = qi // nsub
    dsub = qi % nsub
    lane = lax.broadcasted_iota(jnp.int32, (1, LANES), 1)
    half = LANES // 2

    m_scr[...] = jnp.full(m_scr.shape, -1e30, jnp.float32)
    acc_scr[...] = jnp.zeros(acc_scr.shape, jnp.float32)

    q_left, q_right, q_diag = [], [], []
    for comp in range(2):
        data = (lane < half) if comp == 0 else (lane >= half)
        qc = q_ref[comp]
        q_left.append(qc)
        q_right.append(jnp.where(data, qc, -qc))
        q_diag.append(jnp.where(data, qc, jnp.zeros_like(qc)))

    def chunk(cj, mode):
        j0 = pl.multiple_of(cj * ck, ck)
        vc = v_ref[pl.ds(j0, ck), :]
        for comp in range(2):
            kc = k_ref[comp, pl.ds(j0, ck), :]
            if mode == "D":
                qv = q_diag[comp]
            else:
                qv = jnp.where(cj < cd, q_left[comp], q_right[comp])
            t = lax.dot_general(qv, kc, (((1,), (1,)), ((), ())),
                                preferred_element_type=jnp.float32)
            if mode == "D":
                off = pl.multiple_of((nsub - 1 - dsub) * tq, tq)
                t = t + bias_ref[:, pl.ds(off, ck)]
                kappa = jnp.float32(0.0)
            else:
                kappa = -c * jnp.abs(i0 - j0).astype(jnp.float32)
            m_old = m_scr[comp]
            m_new = jnp.maximum(m_old, jnp.max(t, axis=-1, keepdims=True) + kappa)
            alpha = jnp.exp2(m_old - m_new)
            p = jnp.exp2(t - (m_new - kappa)).astype(jnp.bfloat16)
            acc_scr[comp] = alpha * acc_scr[comp] + jnp.dot(p, vc, preferred_element_type=jnp.float32)
            m_scr[comp] = m_new

    chunk(cd, "D")
    for n in range(1, nch):
        cj = cd + n
        chunk(jnp.where(cj >= nch, cj - nch, cj), "G")

    lam_p = lam_ref[...]
    lam = (jnp.exp(jnp.sum(lam_p[0:1] * lam_p[1:2], axis=-1, keepdims=True))
           - jnp.exp(jnp.sum(lam_p[2:3] * lam_p[3:4], axis=-1, keepdims=True)) + lam_init)
    a0 = acc_scr[0]
    a1 = acc_scr[1]
    o = a0[:, :dv] / a0[:, dv:dv + 1] - lam * (a1[:, :dv] / a1[:, dv:dv + 1])
    o = o * _rms_scale(o) * g_ref[...] * (1.0 - lam_init)
    o_ref[...] = o.astype(jnp.bfloat16)


def _proj_post_kernel(a_ref, w_ref, gpost_ref, xres_ref, out_ref):
    m = jnp.dot(a_ref[...], w_ref[...], preferred_element_type=jnp.float32)
    out_ref[...] = xres_ref[...] + m * _rms_scale(m) * gpost_ref[...]


def _diff_attention_layer(hin, g_pre, g_post, w_qkv, lq1, lk1, lq2, lk2, subln_g, w_o, layer_idx):
    b, s, d = hin.shape
    nh = N_DIFF_HEADS
    dh2 = d // nh
    dh = dh2 // 2
    dv = dh2
    assert dh2 == LANES and s % ATT_CK == 0
    lam_init = 0.8 - 0.6 * math.exp(-0.3 * layer_idx)
    slopes = np.asarray([2.0 ** (-8.0 * (i + 1) / nh) for i in range(nh)], np.float64)
    cvals = slopes * LOG2E
    qa, ka = _aug_tables(nh, dh2, cvals)
    c3 = np.sum(np.stack(_bf16_split3(cvals)), axis=0)
    qscale = dh ** -0.5 * LOG2E

    tm = min(TOKEN_TILE, s)
    const2 = lambda i, j: (0, 0)
    q, k, v = pl.pallas_call(
        functools.partial(_qkv_kernel, d=d, nh=nh, tm=tm, qscale=qscale),
        grid=(b, s // tm),
        in_specs=[
            pl.BlockSpec((None, tm, d), lambda i, j: (i, j, 0)),
            pl.BlockSpec((1, d), const2),
            pl.BlockSpec((d, 3 * d), const2, pipeline_mode=pl.Buffered(1)),
            pl.BlockSpec((nh, 2, 3, dh2), lambda i, j: (0, 0, 0, 0)),
            pl.BlockSpec((nh, 2, 3, dh2), lambda i, j: (0, 0, 0, 0)),
        ],
        out_specs=[
            pl.BlockSpec((None, nh, 2, tm, dh2), lambda i, j: (i, 0, 0, j, 0)),
            pl.BlockSpec((None, nh, 2, tm, dh2), lambda i, j: (i, 0, 0, j, 0)),
            pl.BlockSpec((None, nh, tm, 2 * dv), lambda i, j: (i, 0, j, 0)),
        ],
        out_shape=[
            jax.ShapeDtypeStruct((b, nh, 2, s, dh2), jnp.bfloat16),
            jax.ShapeDtypeStruct((b, nh, 2, s, dh2), jnp.bfloat16),
            jax.ShapeDtypeStruct((b, nh, s, 2 * dv), jnp.bfloat16),
        ],
        compiler_params=pltpu.CompilerParams(
            dimension_semantics=("parallel", "parallel"), vmem_limit_bytes=VMEM_LIMIT_BYTES),
        name="qkv_proj",
    )(hin, g_pre.reshape(1, d), w_qkv.astype(jnp.bfloat16), qa, ka)

    tq, ck, nsub = ATT_TQ, ATT_CK, ATT_NSUB
    wdt = ck + (nsub - 1) * tq
    dist = np.abs(np.arange(tq)[:, None] - np.arange(wdt)[None, :] + (nsub - 1) * tq)
    bias = jnp.asarray(-cvals[:, None, None] * dist[None], jnp.float32)
    lam_p = jnp.stack([lq1, lk1, lq2, lk2]).astype(jnp.float32)

    o = pl.pallas_call(
        functools.partial(_attn_kernel, s=s, dv=dv, lam_init=lam_init),
        grid=(b, nh, s // tq),
        in_specs=[
            pl.BlockSpec(memory_space=pltpu.SMEM),
            pl.BlockSpec((None, None, 2, tq, dh2), lambda i, h, j: (i, h, 0, j, 0)),
            pl.BlockSpec((None, None, 2, s, dh2), lambda i, h, j: (i, h, 0, 0, 0)),
            pl.BlockSpec((None, None, s, 2 * dv), lambda i, h, j: (i, h, 0, 0)),
            pl.BlockSpec((None, tq, wdt), lambda i, h, j: (h, 0, 0)),
            pl.BlockSpec((4, dh), lambda i, h, j: (0, 0)),
            pl.BlockSpec((1, dv), lambda i, h, j: (0, 0)),
        ],
        out_specs=pl.BlockSpec((None, tq, dv), lambda i, h, j: (i, j, h)),
        out_shape=jax.ShapeDtypeStruct((b, s, nh * dv), jnp.bfloat16),
        scratch_shapes=[
            pltpu.VMEM((2, tq, 1), jnp.float32),
            pltpu.VMEM((2, tq, 2 * dv), jnp.float32),
        ],
        compiler_params=pltpu.CompilerParams(
            dimension_semantics=("parallel", "parallel", "arbitrary"),
            vmem_limit_bytes=VMEM_LIMIT_BYTES),
        name="diff_attention",
    )(jnp.asarray(c3, jnp.float32), q, k, v, bias, lam_p, subln_g.reshape(1, dv))

    t = b * s
    tmo = min(TOKEN_TILE, t)
    const1 = lambda i: (0, 0)
    out = pl.pallas_call(
        _proj_post_kernel,
        grid=(t // tmo,),
        in_specs=[
            pl.BlockSpec((tmo, d), lambda i: (i, 0)),
            pl.BlockSpec((d, d), const1),
            pl.BlockSpec((1, d), const1),
            pl.BlockSpec((tmo, d), lambda i: (i, 0)),
        ],
        out_specs=pl.BlockSpec((tmo, d), lambda i: (i, 0)),
        out_shape=jax.ShapeDtypeStruct((t, d), jnp.float32),
        compiler_params=pltpu.CompilerParams(
            dimension_semantics=("parallel",), vmem_limit_bytes=VMEM_LIMIT_BYTES),
        name="attn_out_proj",
    )(o.reshape(t, d), w_o.astype(jnp.bfloat16), g_post.reshape(1, d), hin.reshape(t, d))
    return out.reshape(b, s, d)


def kernel(x, norm_mix_pre, norm_mix_post, norm_ffn_pre, norm_ffn_post, fourier_w_o, diff_w_qkv,
           diff_lambda_q1, diff_lambda_k1, diff_lambda_q2, diff_lambda_k2, diff_subln_g, diff_w_o,
           ffn_w_gate, ffn_w_up, ffn_w_down):
    depth = norm_mix_pre.shape[0]
    h = x
    for i in range(depth):
        j = i // N_MIXERS
        if i % N_MIXERS == 0:
            h = _fourier_layer(h, norm_mix_pre[i], norm_mix_post[i], fourier_w_o[j])
        else:
            h = _diff_attention_layer(h, norm_mix_pre[i], norm_mix_post[i], diff_w_qkv[j],
                                      diff_lambda_q1[j], diff_lambda_k1[j], diff_lambda_q2[j],
                                      diff_lambda_k2[j], diff_subln_g[j], diff_w_o[j], i)
        h = _ffn_layer(h, norm_ffn_pre[i], norm_ffn_post[i], ffn_w_gate[i], ffn_w_up[i], ffn_w_down[i])
    return h
```

```python
import functools
import math

import numpy as np
import jax
import jax.numpy as jnp
from jax import lax
from jax.experimental import pallas as pl
from jax.experimental.pallas import tpu as pltpu

N_FOURIER_GROUPS = 8
N_DIFF_HEADS = 8
RMS_EPS = 1e-6
N_MIXERS = 2

LANES = 128
VMEM_LIMIT_BYTES = 56 * 1024 * 1024

FFT_N2 = 128
FFT_S2_PER_STEP = 16
FFT_K1_PER_STEP = 8
TOKEN_TILE = 512
FFN_CHUNK = 256
ATT_TQ = 512
ATT_CK = 2048
ATT_NSUB = ATT_CK // ATT_TQ
POS_SPLIT = 256
LOG2E = 1.4426950408889634


def _rms_scale(x):
    return lax.rsqrt(jnp.mean(x * x, axis=-1, keepdims=True) + RMS_EPS)


def _np_bf16(v):
    return np.asarray(v, np.float32).astype(jnp.bfloat16)


def _bf16_split3(v):
    v = np.asarray(v, np.float64)
    hi = _np_bf16(v).astype(np.float64)
    mid = _np_bf16(v - hi).astype(np.float64)
    lo = _np_bf16(v - hi - mid).astype(np.float64)
    return hi, mid, lo


def _fft_stage1_kernel(x_ref, g_ref, f1k_ref, ar_ref, ai_ref, *, n1, d, nb):
    x = x_ref[...].reshape(n1 * nb, d)
    xn = (x * _rms_scale(x) * g_ref[...]).astype(jnp.bfloat16)
    a = jnp.dot(f1k_ref[...], xn, preferred_element_type=jnp.float32)
    ar_ref[...] = a[:n1 * nb].astype(jnp.bfloat16).reshape(n1, nb, d)
    ai_ref[...] = a[n1 * nb:].astype(jnp.bfloat16).reshape(n1, nb, d)


def _fft_stage2_kernel(ar_ref, ai_ref, gm_ref, cs_ref, perm_ref, wo_ref, gpost_ref, xres_ref, out_ref,
                       *, n2, d, cg, kb):
    cs = cs_ref[...]
    zs = []
    for i in range(kb):
        a = jnp.concatenate([ar_ref[i], ai_ref[i]], axis=0)
        y = jnp.dot(gm_ref[i], a, preferred_element_type=jnp.float32)
        yr = y[:n2].astype(jnp.bfloat16)
        yi = y[n2:].astype(jnp.bfloat16)
        cols = []
        for gi in range(d // cg):
            lhs = jnp.concatenate([yr[:, gi * cg:(gi + 1) * cg], yi[:, gi * cg:(gi + 1) * cg]], axis=1)
            cols.append(jnp.dot(lhs, cs, preferred_element_type=jnp.float32))
        zs.append(jnp.concatenate(cols, axis=1).astype(jnp.bfloat16))
    z = jnp.concatenate(zs, axis=0)
    z = jnp.dot(perm_ref[...], z, preferred_element_type=jnp.float32).astype(jnp.bfloat16)
    m = jnp.dot(z, wo_ref[...], preferred_element_type=jnp.float32)
    r = m * _rms_scale(m) * gpost_ref[...]
    out_ref[...] = xres_ref[...] + r.reshape(n2, kb, d)


def _fourier_layer(x, g_pre, g_post, w_o):
    b, s, d = x.shape
    n2 = FFT_N2
    n1 = s // n2
    cg = d // N_FOURIER_GROUPS
    nb = FFT_S2_PER_STEP
    assert n1 * n2 == s and n2 % nb == 0

    s1 = np.arange(n1)
    th1 = 2.0 * np.pi * ((s1[:, None] * s1[None, :]) % n1) / n1
    f1 = np.concatenate([np.cos(th1), -np.sin(th1)], axis=0)
    s2 = np.arange(n2)
    kk = (np.arange(n1)[:, None, None] + n1 * np.arange(n2)[None, :, None])
    th2 = 2.0 * np.pi * ((kk * s2[None, None, :]) % s) / s
    gr, gi = np.cos(th2), -np.sin(th2)
    gm = np.concatenate([np.concatenate([gr, -gi], axis=2),
                         np.concatenate([gi, gr], axis=2)], axis=1)
    c = np.arange(cg)
    thc = 2.0 * np.pi * ((c[:, None] * c[None, :]) % cg) / cg
    norm = 1.0 / math.sqrt(s * cg)
    cs = np.concatenate([np.cos(thc), np.sin(thc)], axis=0) * norm

    f1k = jnp.asarray(_np_bf16(np.kron(f1, np.eye(nb))))
    gm = jnp.asarray(_np_bf16(gm))
    cs = jnp.asarray(_np_bf16(cs))

    x4 = x.reshape(b, n1, n2, d)
    ar, ai = pl.pallas_call(
        functools.partial(_fft_stage1_kernel, n1=n1, d=d, nb=nb),
        grid=(b, n2 // nb),
        in_specs=[
            pl.BlockSpec((None, n1, nb, d), lambda i, j: (i, 0, j, 0)),
            pl.BlockSpec((1, d), lambda i, j: (0, 0)),
            pl.BlockSpec((2 * n1 * nb, n1 * nb), lambda i, j: (0, 0), pipeline_mode=pl.Buffered(1)),
        ],
        out_specs=[
            pl.BlockSpec((None, n1, nb, d), lambda i, j: (i, 0, j, 0)),
            pl.BlockSpec((None, n1, nb, d), lambda i, j: (i, 0, j, 0)),
        ],
        out_shape=[jax.ShapeDtypeStruct((b, n1, n2, d), jnp.bfloat16)] * 2,
        compiler_params=pltpu.CompilerParams(
            dimension_semantics=("parallel", "parallel"), vmem_limit_bytes=VMEM_LIMIT_BYTES),
        name="fft_stage1",
    )(x4, g_pre.reshape(1, d), f1k)

    kb = min(FFT_K1_PER_STEP, n1)
    assert n1 % kb == 0
    src = (np.arange(kb)[None, :] * n2 + np.arange(n2)[:, None]).reshape(-1)
    perm = np.zeros((kb * n2, kb * n2), np.float32)
    perm[np.arange(kb * n2), src] = 1.0
    perm = jnp.asarray(_np_bf16(perm))
    xk = x.reshape(b, n2, n1, d)
    out = pl.pallas_call(
        functools.partial(_fft_stage2_kernel, n2=n2, d=d, cg=cg, kb=kb),
        grid=(b, n1 // kb),
        in_specs=[
            pl.BlockSpec((None, kb, n2, d), lambda i, j: (i, j, 0, 0)),
            pl.BlockSpec((None, kb, n2, d), lambda i, j: (i, j, 0, 0)),
            pl.BlockSpec((kb, 2 * n2, 2 * n2), lambda i, j: (j, 0, 0)),
            pl.BlockSpec((2 * cg, cg), lambda i, j: (0, 0)),
            pl.BlockSpec((kb * n2, kb * n2), lambda i, j: (0, 0)),
            pl.BlockSpec((d, d), lambda i, j: (0, 0)),
            pl.BlockSpec((1, d), lambda i, j: (0, 0)),
            pl.BlockSpec((None, n2, kb, d), lambda i, j: (i, 0, j, 0)),
        ],
        out_specs=pl.BlockSpec((None, n2, kb, d), lambda i, j: (i, 0, j, 0)),
        out_shape=jax.ShapeDtypeStruct((b, n2, n1, d), jnp.float32),
        compiler_params=pltpu.CompilerParams(
            dimension_semantics=("parallel", "parallel"), vmem_limit_bytes=VMEM_LIMIT_BYTES),
        name="fft_stage2",
    )(ar, ai, gm, cs, perm, w_o.astype(jnp.bfloat16), g_post.reshape(1, d), xk)
    return out.reshape(b, s, d)


def _ffn_kernel(x_ref, gpre_ref, gpost_ref, wg_ref, wu_ref, wd_ref, out_ref, *, f, chunk):
    x = x_ref[...]
    xn = (x * _rms_scale(x) * gpre_ref[...]).astype(jnp.bfloat16)
    acc = jnp.zeros(x.shape, jnp.float32)
    for c0 in range(0, f, chunk):
        c1 = min(c0 + chunk, f)
        gt = jnp.dot(xn, wg_ref[:, c0:c1], preferred_element_type=jnp.float32)
        up = jnp.dot(xn, wu_ref[:, c0:c1], preferred_element_type=jnp.float32)
        hh = (gt * jax.nn.sigmoid(gt) * up).astype(jnp.bfloat16)
        acc = acc + jnp.dot(hh, wd_ref[c0:c1, :], preferred_element_type=jnp.float32)
    out_ref[...] = x + acc * _rms_scale(acc) * gpost_ref[...]


def _ffn_layer(h, g_pre, g_post, w_gate, w_up, w_down):
    b, s, d = h.shape
    f = w_gate.shape[1]
    t = b * s
    tm = min(TOKEN_TILE, t)
    assert t % tm == 0
    const = lambda i: (0, 0)
    out = pl.pallas_call(
        functools.partial(_ffn_kernel, f=f, chunk=FFN_CHUNK),
        grid=(t // tm,),
        in_specs=[
            pl.BlockSpec((tm, d), lambda i: (i, 0)),
            pl.BlockSpec((1, d), const),
            pl.BlockSpec((1, d), const),
            pl.BlockSpec((d, f), const, pipeline_mode=pl.Buffered(1)),
            pl.BlockSpec((d, f), const, pipeline_mode=pl.Buffered(1)),
            pl.BlockSpec((f, d), const, pipeline_mode=pl.Buffered(1)),
        ],
        out_specs=pl.BlockSpec((tm, d), lambda i: (i, 0)),
        out_shape=jax.ShapeDtypeStruct((t, d), jnp.float32),
        compiler_params=pltpu.CompilerParams(
            dimension_semantics=("parallel",), vmem_limit_bytes=VMEM_LIMIT_BYTES),
        name="swiglu_ffn",
    )(h.reshape(t, d), g_pre.reshape(1, d), g_post.reshape(1, d),
      w_gate.astype(jnp.bfloat16), w_up.astype(jnp.bfloat16), w_down.astype(jnp.bfloat16))
    return out.reshape(b, s, d)


def _qkv_kernel(x_ref, gpre_ref, w_ref, qa_ref, ka_ref, q_ref, k_ref, v_ref, *, d, nh, tm, qscale):
    x = x_ref[...]
    xn = (x * _rms_scale(x) * gpre_ref[...]).astype(jnp.bfloat16)
    dh2 = d // nh
    row = pl.program_id(1) * tm + lax.broadcasted_iota(jnp.int32, (tm, 1), 0)
    pos_lo = (row & (POS_SPLIT - 1)).astype(jnp.float32)
    q_hi = (row & (ATT_TQ - POS_SPLIT)).astype(jnp.float32)
    k_hi = (row & (ATT_CK - POS_SPLIT)).astype(jnp.float32)
    lane = lax.broadcasted_iota(jnp.int32, (1, dh2), 1)
    half = dh2 // 2
    q_all = jnp.dot(xn, w_ref[:, 0:d], preferred_element_type=jnp.float32) * qscale
    k_all = jnp.dot(xn, w_ref[:, d:2 * d], preferred_element_type=jnp.float32)
    v_all = jnp.dot(xn, w_ref[:, 2 * d:3 * d], preferred_element_type=jnp.float32)
    ones_col = jnp.where(lane == 0, 1.0, 0.0).astype(jnp.bfloat16)
    for h in range(nh):
        qh = q_all[:, h * dh2:(h + 1) * dh2]
        kh = k_all[:, h * dh2:(h + 1) * dh2]
        for comp in range(2):
            data = (lane < half) if comp == 0 else (lane >= half)
            qa = qa_ref[h, comp]
            ka = ka_ref[h, comp]
            q_aug = qa[0:1] + qa[1:2] * pos_lo + qa[2:3] * q_hi
            k_aug = ka[0:1] + ka[1:2] * pos_lo + ka[2:3] * k_hi
            q_ref[h, comp] = jnp.where(data, qh, q_aug).astype(jnp.bfloat16)
            k_ref[h, comp] = jnp.where(data, kh, k_aug).astype(jnp.bfloat16)
        vh = v_all[:, h * dh2:(h + 1) * dh2].astype(jnp.bfloat16)
        v_ref[h] = jnp.concatenate([vh, jnp.broadcast_to(ones_col, (tm, dh2))], axis=1)


def _aug_tables(nh, dh2, cvals):
    half = dh2 // 2
    qa = np.zeros((nh, 2, 3, dh2), np.float64)
    ka = np.zeros((nh, 2, 3, dh2), np.float64)
    for h in range(nh):
        pieces = _bf16_split3(cvals[h])
        for comp in range(2):
            base = half if comp == 0 else 0
            for p in range(3):
                for part in (1, 2):
                    qa[h, comp, 0, base + 3 * (part - 1) + p] = pieces[p]
                    ka[h, comp, part, base + 3 * (part - 1) + p] = 1.0
                    qa[h, comp, part, base + 6 + 3 * (part - 1) + p] = -1.0
                    ka[h, comp, 0, base + 6 + 3 * (part - 1) + p] = pieces[p]
    return jnp.asarray(qa, jnp.float32), jnp.asarray(ka, jnp.float32)


def _attn_kernel(c_ref, q_ref, k_ref, v_ref, bias_ref, lam_ref, g_ref, o_ref, m_scr, acc_scr,
                 *, s, dv, lam_init):
    tq, ck, nsub = ATT_TQ, ATT_CK, ATT_NSUB
    nch = s // ck
    hd = pl.program_id(1)
    qi = pl.program_id(2)
    c = c_ref[hd]
    i0 = qi * tq
    cd = qi // nsub
    dsub = qi % nsub
    lane = lax.broadcasted_iota(jnp.int32, (1, LANES), 1)
    half = LANES // 2

    m_scr[...] = jnp.full(m_scr.shape, -1e30, jnp.float32)
    acc_scr[...] = jnp.zeros(acc_scr.shape, jnp.float32)

    q_left, q_right, q_diag = [], [], []
    for comp in range(2):
        data = (lane < half) if comp == 0 else (lane >= half)
        qc = q_ref[comp]
        q_left.append(qc)
        q_right.append(jnp.where(data, qc, -qc))
        q_diag.append(jnp.where(data, qc, jnp.zeros_like(qc)))

    def scores(cj, mode):
        j0 = pl.multiple_of(cj * ck, ck)
        ts = []
        for comp in range(2):
            kc = k_ref[comp, pl.ds(j0, ck), :]
            if mode == "D":
                qv = q_diag[comp]
            else:
                qv = jnp.where(cj < cd, q_left[comp], q_right[comp])
            t = lax.dot_general(qv, kc, (((1,), (1,)), ((), ())),
                                preferred_element_type=jnp.float32)
            if mode == "D":
                off = pl.multiple_of((nsub - 1 - dsub) * tq, tq)
                t = t + bias_ref[:, pl.ds(off, ck)]
            ts.append(t)
        if mode == "D":
            kappa = jnp.float32(0.0)
        else:
            kappa = -c * jnp.abs(i0 - j0).astype(jnp.float32)
        return j0, ts, kappa

    def accumulate(j0, ts, kappa):
        vc = v_ref[pl.ds(j0, ck), :]
        for comp in range(2):
            t = ts[comp]
            m_old = m_scr[comp]
            m_new = jnp.maximum(m_old, jnp.max(t, axis=-1, keepdims=True) + kappa)
            alpha = jnp.exp2(m_old - m_new)
            p = jnp.exp2(t - (m_new - kappa)).astype(jnp.bfloat16)
            acc_scr[comp] = alpha * acc_scr[comp] + jnp.dot(p, vc, preferred_element_type=jnp.float32)
            m_scr[comp] = m_new

    pending = scores(cd, "D")
    for n in range(1, nch):
        cj = cd + n
        nxt = scores(jnp.where(cj >= nch, cj - nch, cj), "G")
        accumulate(*pending)
        pending = nxt
    accumulate(*pending)

    lam_p = lam_ref[...]
    lam = (jnp.exp(jnp.sum(lam_p[0:1] * lam_p[1:2], axis=-1, keepdims=True))
           - jnp.exp(jnp.sum(lam_p[2:3] * lam_p[3:4], axis=-1, keepdims=True)) + lam_init)
    a0 = acc_scr[0]
    a1 = acc_scr[1]
    o = a0[:, :dv] / a0[:, dv:dv + 1] - lam * (a1[:, :dv] / a1[:, dv:dv + 1])
    o = o * _rms_scale(o) * g_ref[...] * (1.0 - lam_init)
    o_ref[...] = o.astype(jnp.bfloat16)


def _proj_post_kernel(a_ref, w_ref, gpost_ref, xres_ref, out_ref):
    m = jnp.dot(a_ref[...], w_ref[...], preferred_element_type=jnp.float32)
    out_ref[...] = xres_ref[...] + m * _rms_scale(m) * gpost_ref[...]


def _diff_attention_layer(hin, g_pre, g_post, w_qkv, lq1, lk1, lq2, lk2, subln_g, w_o, layer_idx):
    b, s, d = hin.shape
    nh = N_DIFF_HEADS
    dh2 = d // nh
    dh = dh2 // 2
    dv = dh2
    assert dh2 == LANES and s % ATT_CK == 0
    lam_init = 0.8 - 0.6 * math.exp(-0.3 * layer_idx)
    slopes = np.asarray([2.0 ** (-8.0 * (i + 1) / nh) for i in range(nh)], np.float64)
    cvals = slopes * LOG2E
    qa, ka = _aug_tables(nh, dh2, cvals)
    c3 = np.sum(np.stack(_bf16_split3(cvals)), axis=0)
    qscale = dh ** -0.5 * LOG2E

    tm = min(TOKEN_TILE, s)
    const2 = lambda i, j: (0, 0)
    q, k, v = pl.pallas_call(
        functools.partial(_qkv_kernel, d=d, nh=nh, tm=tm, qscale=qscale),
        grid=(b, s // tm),
        in_specs=[
            pl.BlockSpec((None, tm, d), lambda i, j: (i, j, 0)),
            pl.BlockSpec((1, d), const2),
            pl.BlockSpec((d, 3 * d), const2, pipeline_mode=pl.Buffered(1)),
            pl.BlockSpec((nh, 2, 3, dh2), lambda i, j: (0, 0, 0, 0)),
            pl.BlockSpec((nh, 2, 3, dh2), lambda i, j: (0, 0, 0, 0)),
        ],
        out_specs=[
            pl.BlockSpec((None, nh, 2, tm, dh2), lambda i, j: (i, 0, 0, j, 0)),
            pl.BlockSpec((None, nh, 2, tm, dh2), lambda i, j: (i, 0, 0, j, 0)),
            pl.BlockSpec((None, nh, tm, 2 * dv), lambda i, j: (i, 0, j, 0)),
        ],
        out_shape=[
            jax.ShapeDtypeStruct((b, nh, 2, s, dh2), jnp.bfloat16),
            jax.ShapeDtypeStruct((b, nh, 2, s, dh2), jnp.bfloat16),
            jax.ShapeDtypeStruct((b, nh, s, 2 * dv), jnp.bfloat16),
        ],
        compiler_params=pltpu.CompilerParams(
            dimension_semantics=("parallel", "parallel"), vmem_limit_bytes=VMEM_LIMIT_BYTES),
        name="qkv_proj",
    )(hin, g_pre.reshape(1, d), w_qkv.astype(jnp.bfloat16), qa, ka)

    tq, ck, nsub = ATT_TQ, ATT_CK, ATT_NSUB
    wdt = ck + (nsub - 1) * tq
    dist = np.abs(np.arange(tq)[:, None] - np.arange(wdt)[None, :] + (nsub - 1) * tq)
    bias = jnp.asarray(-cvals[:, None, None] * dist[None], jnp.float32)
    lam_p = jnp.stack([lq1, lk1, lq2, lk2]).astype(jnp.float32)

    o = pl.pallas_call(
        functools.partial(_attn_kernel, s=s, dv=dv, lam_init=lam_init),
        grid=(b, nh, s // tq),
        in_specs=[
            pl.BlockSpec(memory_space=pltpu.SMEM),
            pl.BlockSpec((None, None, 2, tq, dh2), lambda i, h, j: (i, h, 0, j, 0)),
            pl.BlockSpec((None, None, 2, s, dh2), lambda i, h, j: (i, h, 0, 0, 0)),
            pl.BlockSpec((None, None, s, 2 * dv), lambda i, h, j: (i, h, 0, 0)),
            pl.BlockSpec((None, tq, wdt), lambda i, h, j: (h, 0, 0)),
            pl.BlockSpec((4, dh), lambda i, h, j: (0, 0)),
            pl.BlockSpec((1, dv), lambda i, h, j: (0, 0)),
        ],
        out_specs=pl.BlockSpec((None, tq, dv), lambda i, h, j: (i, j, h)),
        out_shape=jax.ShapeDtypeStruct((b, s, nh * dv), jnp.bfloat16),
        scratch_shapes=[
            pltpu.VMEM((2, tq, 1), jnp.float32),
            pltpu.VMEM((2, tq, 2 * dv), jnp.float32),
        ],
        compiler_params=pltpu.CompilerParams(
            dimension_semantics=("parallel", "parallel", "arbitrary"),
            vmem_limit_bytes=VMEM_LIMIT_BYTES),
        name="diff_attention",
    )(jnp.asarray(c3, jnp.float32), q, k, v, bias, lam_p, subln_g.reshape(1, dv))

    t = b * s
    tmo = min(TOKEN_TILE, t)
    const1 = lambda i: (0, 0)
    out = pl.pallas_call(
        _proj_post_kernel,
        grid=(t // tmo,),
        in_specs=[
            pl.BlockSpec((tmo, d), lambda i: (i, 0)),
            pl.BlockSpec((d, d), const1),
            pl.BlockSpec((1, d), const1),
            pl.BlockSpec((tmo, d), lambda i: (i, 0)),
        ],
        out_specs=pl.BlockSpec((tmo, d), lambda i: (i, 0)),
        out_shape=jax.ShapeDtypeStruct((t, d), jnp.float32),
        compiler_params=pltpu.CompilerParams(
            dimension_semantics=("parallel",), vmem_limit_bytes=VMEM_LIMIT_BYTES),
        name="attn_out_proj",
    )(o.reshape(t, d), w_o.astype(jnp.bfloat16), g_post.reshape(1, d), hin.reshape(t, d))
    return out.reshape(b, s, d)


def kernel(x, norm_mix_pre, norm_mix_post, norm_ffn_pre, norm_ffn_post, fourier_w_o, diff_w_qkv,
           diff_lambda_q1, diff_lambda_k1, diff_lambda_q2, diff_lambda_k2, diff_subln_g, diff_w_o,
           ffn_w_gate, ffn_w_up, ffn_w_down):
    depth = norm_mix_pre.shape[0]
    h = x
    for i in range(depth):
        j = i // N_MIXERS
        if i % N_MIXERS == 0:
            h = _fourier_layer(h, norm_mix_pre[i], norm_mix_post[i], fourier_w_o[j])
        else:
            h = _diff_attention_layer(h, norm_mix_pre[i], norm_mix_post[i], diff_w_qkv[j],
                                      diff_lambda_q1[j], diff_lambda_k1[j], diff_lambda_q2[j],
                                      diff_lambda_k2[j], diff_subln_g[j], diff_w_o[j], i)
        h = _ffn_layer(h, norm_ffn_pre[i], norm_ffn_post[i], ffn_w_gate[i], ffn_w_up[i], ffn_w_down[i])
    return h
```

```python
import functools
import math

import numpy as np
import jax
import jax.numpy as jnp
from jax import lax
from jax.experimental import pallas as pl
from jax.experimental.pallas import tpu as pltpu

N_FOURIER_GROUPS = 8
N_DIFF_HEADS = 8
RMS_EPS = 1e-6
N_MIXERS = 2

LANES = 128
VMEM_LIMIT_BYTES = 56 * 1024 * 1024

FFT_N2 = 128
FFT_S2_PER_STEP = 16
FFT_K1_PER_STEP = 8
TOKEN_TILE = 512
FFN_CHUNK = 256
ATT_TQ = 512
ATT_CK = 2048
ATT_NSUB = ATT_CK // ATT_TQ
POS_SPLIT = 256
LOG2E = 1.4426950408889634
SKIP_EXP2_ZERO = 153.0
SKIP_SLACK = 1.01


def _rms_scale(x):
    return lax.rsqrt(jnp.mean(x * x, axis=-1, keepdims=True) + RMS_EPS)


def _np_bf16(v):
    return np.asarray(v, np.float32).astype(jnp.bfloat16)


def _bf16_split3(v):
    v = np.asarray(v, np.float64)
    hi = _np_bf16(v).astype(np.float64)
    mid = _np_bf16(v - hi).astype(np.float64)
    lo = _np_bf16(v - hi - mid).astype(np.float64)
    return hi, mid, lo


def _fft_stage1_kernel(x_ref, g_ref, f1k_ref, ar_ref, ai_ref, *, n1, d, nb):
    x = x_ref[...].reshape(n1 * nb, d)
    xn = (x * _rms_scale(x) * g_ref[...]).astype(jnp.bfloat16)
    a = jnp.dot(f1k_ref[...], xn, preferred_element_type=jnp.float32)
    ar_ref[...] = a[:n1 * nb].astype(jnp.bfloat16).reshape(n1, nb, d)
    ai_ref[...] = a[n1 * nb:].astype(jnp.bfloat16).reshape(n1, nb, d)


def _fft_stage2_kernel(ar_ref, ai_ref, gm_ref, cs_ref, perm_ref, wo_ref, gpost_ref, xres_ref, out_ref,
                       *, n2, d, cg, kb):
    cs = cs_ref[...]
    zs = []
    for i in range(kb):
        a = jnp.concatenate([ar_ref[i], ai_ref[i]], axis=0)
        y = jnp.dot(gm_ref[i], a, preferred_element_type=jnp.float32)
        yr = y[:n2].astype(jnp.bfloat16)
        yi = y[n2:].astype(jnp.bfloat16)
        cols = []
        for gi in range(d // cg):
            lhs = jnp.concatenate([yr[:, gi * cg:(gi + 1) * cg], yi[:, gi * cg:(gi + 1) * cg]], axis=1)
            cols.append(jnp.dot(lhs, cs, preferred_element_type=jnp.float32))
        zs.append(jnp.concatenate(cols, axis=1).astype(jnp.bfloat16))
    z = jnp.concatenate(zs, axis=0)
    z = jnp.dot(perm_ref[...], z, preferred_element_type=jnp.float32).astype(jnp.bfloat16)
    m = jnp.dot(z, wo_ref[...], preferred_element_type=jnp.float32)
    r = m * _rms_scale(m) * gpost_ref[...]
    out_ref[...] = xres_ref[...] + r.reshape(n2, kb, d)


def _fourier_layer(x, g_pre, g_post, w_o):
    b, s, d = x.shape
    n2 = FFT_N2
    n1 = s // n2
    cg = d // N_FOURIER_GROUPS
    nb = FFT_S2_PER_STEP
    assert n1 * n2 == s and n2 % nb == 0

    s1 = np.arange(n1)
    th1 = 2.0 * np.pi * ((s1[:, None] * s1[None, :]) % n1) / n1
    f1 = np.concatenate([np.cos(th1), -np.sin(th1)], axis=0)
    s2 = np.arange(n2)
    kk = (np.arange(n1)[:, None, None] + n1 * np.arange(n2)[None, :, None])
    th2 = 2.0 * np.pi * ((kk * s2[None, None, :]) % s) / s
    gr, gi = np.cos(th2), -np.sin(th2)
    gm = np.concatenate([np.concatenate([gr, -gi], axis=2),
                         np.concatenate([gi, gr], axis=2)], axis=1)
    c = np.arange(cg)
    thc = 2.0 * np.pi * ((c[:, None] * c[None, :]) % cg) / cg
    norm = 1.0 / math.sqrt(s * cg)
    cs = np.concatenate([np.cos(thc), np.sin(thc)], axis=0) * norm

    f1k = jnp.asarray(_np_bf16(np.kron(f1, np.eye(nb))))
    gm = jnp.asarray(_np_bf16(gm))
    cs = jnp.asarray(_np_bf16(cs))

    x4 = x.reshape(b, n1, n2, d)
    ar, ai = pl.pallas_call(
        functools.partial(_fft_stage1_kernel, n1=n1, d=d, nb=nb),
        grid=(b, n2 // nb),
        in_specs=[
            pl.BlockSpec((None, n1, nb, d), lambda i, j: (i, 0, j, 0)),
            pl.BlockSpec((1, d), lambda i, j: (0, 0)),
            pl.BlockSpec((2 * n1 * nb, n1 * nb), lambda i, j: (0, 0), pipeline_mode=pl.Buffered(1)),
        ],
        out_specs=[
            pl.BlockSpec((None, n1, nb, d), lambda i, j: (i, 0, j, 0)),
            pl.BlockSpec((None, n1, nb, d), lambda i, j: (i, 0, j, 0)),
        ],
        out_shape=[jax.ShapeDtypeStruct((b, n1, n2, d), jnp.bfloat16)] * 2,
        compiler_params=pltpu.CompilerParams(
            dimension_semantics=("parallel", "parallel"), vmem_limit_bytes=VMEM_LIMIT_BYTES),
        name="fft_stage1",
    )(x4, g_pre.reshape(1, d), f1k)

    kb = min(FFT_K1_PER_STEP, n1)
    assert n1 % kb == 0
    src = (np.arange(kb)[None, :] * n2 + np.arange(n2)[:, None]).reshape(-1)
    perm = np.zeros((kb * n2, kb * n2), np.float32)
    perm[np.arange(kb * n2), src] = 1.0
    perm = jnp.asarray(_np_bf16(perm))
    xk = x.reshape(b, n2, n1, d)
    out = pl.pallas_call(
        functools.partial(_fft_stage2_kernel, n2=n2, d=d, cg=cg, kb=kb),
        grid=(b, n1 // kb),
        in_specs=[
            pl.BlockSpec((None, kb, n2, d), lambda i, j: (i, j, 0, 0)),
            pl.BlockSpec((None, kb, n2, d), lambda i, j: (i, j, 0, 0)),
            pl.BlockSpec((kb, 2 * n2, 2 * n2), lambda i, j: (j, 0, 0)),
            pl.BlockSpec((2 * cg, cg), lambda i, j: (0, 0)),
            pl.BlockSpec((kb * n2, kb * n2), lambda i, j: (0, 0)),
            pl.BlockSpec((d, d), lambda i, j: (0, 0)),
            pl.BlockSpec((1, d), lambda i, j: (0, 0)),
            pl.BlockSpec((None, n2, kb, d), lambda i, j: (i, 0, j, 0)),
        ],
        out_specs=pl.BlockSpec((None, n2, kb, d), lambda i, j: (i, 0, j, 0)),
        out_shape=jax.ShapeDtypeStruct((b, n2, n1, d), jnp.float32),
        compiler_params=pltpu.CompilerParams(
            dimension_semantics=("parallel", "parallel"), vmem_limit_bytes=VMEM_LIMIT_BYTES),
        name="fft_stage2",
    )(ar, ai, gm, cs, perm, w_o.astype(jnp.bfloat16), g_post.reshape(1, d), xk)
    return out.reshape(b, s, d)


def _ffn_kernel(x_ref, gpre_ref, gpost_ref, wg_ref, wu_ref, wd_ref, out_ref, *, f, chunk):
    x = x_ref[...]
    xn = (x * _rms_scale(x) * gpre_ref[...]).astype(jnp.bfloat16)
    acc = jnp.zeros(x.shape, jnp.float32)
    for c0 in range(0, f, chunk):
        c1 = min(c0 + chunk, f)
        gt = jnp.dot(xn, wg_ref[:, c0:c1], preferred_element_type=jnp.float32)
        up = jnp.dot(xn, wu_ref[:, c0:c1], preferred_element_type=jnp.float32)
        hh = (gt * jax.nn.sigmoid(gt) * up).astype(jnp.bfloat16)
        acc = acc + jnp.dot(hh, wd_ref[c0:c1, :], preferred_element_type=jnp.float32)
    out_ref[...] = x + acc * _rms_scale(acc) * gpost_ref[...]


def _ffn_layer(h, g_pre, g_post, w_gate, w_up, w_down):
    b, s, d = h.shape
    f = w_gate.shape[1]
    t = b * s
    tm = min(TOKEN_TILE, t)
    assert t % tm == 0
    const = lambda i: (0, 0)
    out = pl.pallas_call(
        functools.partial(_ffn_kernel, f=f, chunk=FFN_CHUNK),
        grid=(t // tm,),
        in_specs=[
            pl.BlockSpec((tm, d), lambda i: (i, 0)),
            pl.BlockSpec((1, d), const),
            pl.BlockSpec((1, d), const),
            pl.BlockSpec((d, f), const, pipeline_mode=pl.Buffered(1)),
            pl.BlockSpec((d, f), const, pipeline_mode=pl.Buffered(1)),
            pl.BlockSpec((f, d), const, pipeline_mode=pl.Buffered(1)),
        ],
        out_specs=pl.BlockSpec((tm, d), lambda i: (i, 0)),
        out_shape=jax.ShapeDtypeStruct((t, d), jnp.float32),
        compiler_params=pltpu.CompilerParams(
            dimension_semantics=("parallel",), vmem_limit_bytes=VMEM_LIMIT_BYTES),
        name="swiglu_ffn",
    )(h.reshape(t, d), g_pre.reshape(1, d), g_post.reshape(1, d),
      w_gate.astype(jnp.bfloat16), w_up.astype(jnp.bfloat16), w_down.astype(jnp.bfloat16))
    return out.reshape(b, s, d)


def _qkv_kernel(x_ref, gpre_ref, w_ref, qa_ref, ka_ref, q_ref, k_ref, v_ref, *, d, nh, tm, qscale):
    x = x_ref[...]
    xn = (x * _rms_scale(x) * gpre_ref[...]).astype(jnp.bfloat16)
    dh2 = d // nh
    row = pl.program_id(1) * tm + lax.broadcasted_iota(jnp.int32, (tm, 1), 0)
    pos_lo = (row & (POS_SPLIT - 1)).astype(jnp.float32)
    q_hi = (row & (ATT_TQ - POS_SPLIT)).astype(jnp.float32)
    k_hi = (row & (ATT_CK - POS_SPLIT)).astype(jnp.float32)
    lane = lax.broadcasted_iota(jnp.int32, (1, dh2), 1)
    half = dh2 // 2
    q_all = jnp.dot(xn, w_ref[:, 0:d], preferred_element_type=jnp.float32) * qscale
    k_all = jnp.dot(xn, w_ref[:, d:2 * d], preferred_element_type=jnp.float32)
    v_all = jnp.dot(xn, w_ref[:, 2 * d:3 * d], preferred_element_type=jnp.float32)
    ones_col = jnp.where(lane == 0, 1.0, 0.0).astype(jnp.bfloat16)
    for h in range(nh):
        qh = q_all[:, h * dh2:(h + 1) * dh2]
        kh = k_all[:, h * dh2:(h + 1) * dh2]
        for comp in range(2):
            data = (lane < half) if comp == 0 else (lane >= half)
            qa = qa_ref[h, comp]
            ka = ka_ref[h, comp]
            q_aug = qa[0:1] + qa[1:2] * pos_lo + qa[2:3] * q_hi
            k_aug = ka[0:1] + ka[1:2] * pos_lo + ka[2:3] * k_hi
            q_ref[h, comp] = jnp.where(data, qh, q_aug).astype(jnp.bfloat16)
            k_ref[h, comp] = jnp.where(data, kh, k_aug).astype(jnp.bfloat16)
        vh = v_all[:, h * dh2:(h + 1) * dh2].astype(jnp.bfloat16)
        v_ref[h] = jnp.concatenate([vh, jnp.broadcast_to(ones_col, (tm, dh2))], axis=1)


def _aug_tables(nh, dh2, cvals):
    half = dh2 // 2
    qa = np.zeros((nh, 2, 3, dh2), np.float64)
    ka = np.zeros((nh, 2, 3, dh2), np.float64)
    for h in range(nh):
        pieces = _bf16_split3(cvals[h])
        for comp in range(2):
            base = half if comp == 0 else 0
            for p in range(3):
                for part in (1, 2):
                    qa[h, comp, 0, base + 3 * (part - 1) + p] = pieces[p]
                    ka[h, comp, part, base + 3 * (part - 1) + p] = 1.0
                    qa[h, comp, part, base + 6 + 3 * (part - 1) + p] = -1.0
                    ka[h, comp, 0, base + 6 + 3 * (part - 1) + p] = pieces[p]
    return jnp.asarray(qa, jnp.float32), jnp.asarray(ka, jnp.float32)


def _attn_kernel(c_ref, q_ref, k_ref, v_ref, bias_ref, lam_ref, g_ref, o_ref, m_scr, acc_scr, ksq_scr,
                 *, s, dv, lam_init):
    tq, ck, nsub = ATT_TQ, ATT_CK, ATT_NSUB
    nch = s // ck
    hd = pl.program_id(1)
    qi = pl.program_id(2)
    c = c_ref[hd]
    i0 = qi * tq
    cd = qi // nsub
    dsub = qi % nsub
    lane = lax.broadcasted_iota(jnp.int32, (1, LANES), 1)
    half = LANES // 2

    m_scr[...] = jnp.full(m_scr.shape, -1e30, jnp.float32)
    acc_scr[...] = jnp.zeros(acc_scr.shape, jnp.float32)

    @pl.when(qi == 0)
    def _():
        for comp in range(2):
            data = (lane < half) if comp == 0 else (lane >= half)
            best = jnp.zeros((tq, 1), jnp.float32)
            for r0 in range(0, s, tq):
                kd = jnp.where(data, k_ref[comp, r0:r0 + tq, :].astype(jnp.float32), 0.0)
                best = jnp.maximum(best, jnp.sum(kd * kd, axis=-1, keepdims=True))
            ksq_scr[comp] = jnp.broadcast_to(jnp.max(best, axis=0, keepdims=True), (1, LANES))

    q_left, q_right, q_diag = [], [], []
    u2 = jnp.zeros((1, 1), jnp.float32)
    for comp in range(2):
        data = (lane < half) if comp == 0 else (lane >= half)
        qc = q_ref[comp]
        q_left.append(qc)
        q_right.append(jnp.where(data, qc, -qc))
        q_diag.append(jnp.where(data, qc, jnp.zeros_like(qc)))
        qd = q_diag[comp].astype(jnp.float32)
        qsq = jnp.max(jnp.sum(qd * qd, axis=-1, keepdims=True), axis=0, keepdims=True)
        u2 = jnp.maximum(u2, qsq * ksq_scr[comp][:, :1])
    thr = jnp.max(2.0 * SKIP_SLACK * jnp.sqrt(u2) + SKIP_EXP2_ZERO)

    def scores(cj, mode):
        j0 = pl.multiple_of(cj * ck, ck)
        ts = []
        for comp in range(2):
            kc = k_ref[comp, pl.ds(j0, ck), :]
            if mode == "D":
                qv = q_diag[comp]
            else:
                qv = jnp.where(cj < cd, q_left[comp], q_right[comp])
            t = lax.dot_general(qv, kc, (((1,), (1,)), ((), ())),
                                preferred_element_type=jnp.float32)
            if mode == "D":
                off = pl.multiple_of((nsub - 1 - dsub) * tq, tq)
                t = t + bias_ref[:, pl.ds(off, ck)]
            ts.append(t)
        if mode == "D":
            kappa = jnp.float32(0.0)
        else:
            kappa = -c * jnp.abs(i0 - j0).astype(jnp.float32)
        return j0, ts, kappa

    def accumulate(j0, ts, kappa):
        vc = v_ref[pl.ds(j0, ck), :]
        for comp in range(2):
            t = ts[comp]
            m_old = m_scr[comp]
            m_new = jnp.maximum(m_old, jnp.max(t, axis=-1, keepdims=True) + kappa)
            alpha = jnp.exp2(m_old - m_new)
            p = jnp.exp2(t - (m_new - kappa)).astype(jnp.bfloat16)
            acc_scr[comp] = alpha * acc_scr[comp] + jnp.dot(p, vc, preferred_element_type=jnp.float32)
            m_scr[comp] = m_new

    off = i0 - cd * ck
    n_left = jnp.int32(0)
    n_right = jnp.int32(0)
    for n in range(1, nch):
        d_left = (off + (n - 1) * ck + 1).astype(jnp.float32)
        d_right = (n * ck - off - tq + 1).astype(jnp.float32)
        n_left += ((n <= cd) & (c * d_left < thr)).astype(jnp.int32)
        n_right += ((cd + n <= nch - 1) & (c * d_right < thr)).astype(jnp.int32)
    n_need = 1 + n_left + n_right

    def sweep(width):
        def run():
            lo = jnp.minimum(cd - n_left, nch - width)
            pending = scores(cd, "D")
            for n in range(1, width):
                r = cd - lo + n
                nxt = scores(lo + jnp.where(r >= width, r - width, r), "G")
                accumulate(*pending)
                pending = nxt
            accumulate(*pending)
        return run

    def dispatch(width):
        if width == nch:
            sweep(nch)()
        else:
            lax.cond(n_need <= width, sweep(width), lambda: dispatch(width + 1))

    dispatch(1)

    lam_p = lam_ref[...]
    lam = (jnp.exp(jnp.sum(lam_p[0:1] * lam_p[1:2], axis=-1, keepdims=True))
           - jnp.exp(jnp.sum(lam_p[2:3] * lam_p[3:4], axis=-1, keepdims=True)) + lam_init)
    a0 = acc_scr[0]
    a1 = acc_scr[1]
    o = a0[:, :dv] / a0[:, dv:dv + 1] - lam * (a1[:, :dv] / a1[:, dv:dv + 1])
    o = o * _rms_scale(o) * g_ref[...] * (1.0 - lam_init)
    o_ref[...] = o.astype(jnp.bfloat16)


def _proj_post_kernel(a_ref, w_ref, gpost_ref, xres_ref, out_ref):
    m = jnp.dot(a_ref[...], w_ref[...], preferred_element_type=jnp.float32)
    out_ref[...] = xres_ref[...] + m * _rms_scale(m) * gpost_ref[...]


def _diff_attention_layer(hin, g_pre, g_post, w_qkv, lq1, lk1, lq2, lk2, subln_g, w_o, layer_idx):
    b, s, d = hin.shape
    nh = N_DIFF_HEADS
    dh2 = d // nh
    dh = dh2 // 2
    dv = dh2
    assert dh2 == LANES and s % ATT_CK == 0
    lam_init = 0.8 - 0.6 * math.exp(-0.3 * layer_idx)
    slopes = np.asarray([2.0 ** (-8.0 * (i + 1) / nh) for i in range(nh)], np.float64)
    cvals = slopes * LOG2E
    qa, ka = _aug_tables(nh, dh2, cvals)
    c3 = np.sum(np.stack(_bf16_split3(cvals)), axis=0)
    qscale = dh ** -0.5 * LOG2E

    tm = min(TOKEN_TILE, s)
    const2 = lambda i, j: (0, 0)
    q, k, v = pl.pallas_call(
        functools.partial(_qkv_kernel, d=d, nh=nh, tm=tm, qscale=qscale),
        grid=(b, s // tm),
        in_specs=[
            pl.BlockSpec((None, tm, d), lambda i, j: (i, j, 0)),
            pl.BlockSpec((1, d), const2),
            pl.BlockSpec((d, 3 * d), const2, pipeline_mode=pl.Buffered(1)),
            pl.BlockSpec((nh, 2, 3, dh2), lambda i, j: (0, 0, 0, 0)),
            pl.BlockSpec((nh, 2, 3, dh2), lambda i, j: (0, 0, 0, 0)),
        ],
        out_specs=[
            pl.BlockSpec((None, nh, 2, tm, dh2), lambda i, j: (i, 0, 0, j, 0)),
            pl.BlockSpec((None, nh, 2, tm, dh2), lambda i, j: (i, 0, 0, j, 0)),
            pl.BlockSpec((None, nh, tm, 2 * dv), lambda i, j: (i, 0, j, 0)),
        ],
        out_shape=[
            jax.ShapeDtypeStruct((b, nh, 2, s, dh2), jnp.bfloat16),
            jax.ShapeDtypeStruct((b, nh, 2, s, dh2), jnp.bfloat16),
            jax.ShapeDtypeStruct((b, nh, s, 2 * dv), jnp.bfloat16),
        ],
        compiler_params=pltpu.CompilerParams(
            dimension_semantics=("parallel", "parallel"), vmem_limit_bytes=VMEM_LIMIT_BYTES),
        name="qkv_proj",
    )(hin, g_pre.reshape(1, d), w_qkv.astype(jnp.bfloat16), qa, ka)

    tq, ck, nsub = ATT_TQ, ATT_CK, ATT_NSUB
    wdt = ck + (nsub - 1) * tq
    dist = np.abs(np.arange(tq)[:, None] - np.arange(wdt)[None, :] + (nsub - 1) * tq)
    bias = jnp.asarray(-cvals[:, None, None] * dist[None], jnp.float32)
    lam_p = jnp.stack([lq1, lk1, lq2, lk2]).astype(jnp.float32)

    o = pl.pallas_call(
        functools.partial(_attn_kernel, s=s, dv=dv, lam_init=lam_init),
        grid=(b, nh, s // tq),
        in_specs=[
            pl.BlockSpec(memory_space=pltpu.SMEM),
            pl.BlockSpec((None, None, 2, tq, dh2), lambda i, h, j: (i, h, 0, j, 0)),
            pl.BlockSpec((None, None, 2, s, dh2), lambda i, h, j: (i, h, 0, 0, 0)),
            pl.BlockSpec((None, None, s, 2 * dv), lambda i, h, j: (i, h, 0, 0)),
            pl.BlockSpec((None, tq, wdt), lambda i, h, j: (h, 0, 0)),
            pl.BlockSpec((4, dh), lambda i, h, j: (0, 0)),
            pl.BlockSpec((1, dv), lambda i, h, j: (0, 0)),
        ],
        out_specs=pl.BlockSpec((None, tq, dv), lambda i, h, j: (i, j, h)),
        out_shape=jax.ShapeDtypeStruct((b, s, nh * dv), jnp.bfloat16),
        scratch_shapes=[
            pltpu.VMEM((2, tq, 1), jnp.float32),
            pltpu.VMEM((2, tq, 2 * dv), jnp.float32),
            pltpu.VMEM((2, 1, LANES), jnp.float32),
        ],
        compiler_params=pltpu.CompilerParams(
            dimension_semantics=("parallel", "parallel", "arbitrary"),
            vmem_limit_bytes=VMEM_LIMIT_BYTES),
        name="diff_attention",
    )(jnp.asarray(c3, jnp.float32), q, k, v, bias, lam_p, subln_g.reshape(1, dv))

    t = b * s
    tmo = min(TOKEN_TILE, t)
    const1 = lambda i: (0, 0)
    out = pl.pallas_call(
        _proj_post_kernel,
        grid=(t // tmo,),
        in_specs=[
            pl.BlockSpec((tmo, d), lambda i: (i, 0)),
            pl.BlockSpec((d, d), const1),
            pl.BlockSpec((1, d), const1),
            pl.BlockSpec((tmo, d), lambda i: (i, 0)),
        ],
        out_specs=pl.BlockSpec((tmo, d), lambda i: (i, 0)),
        out_shape=jax.ShapeDtypeStruct((t, d), jnp.float32),
        compiler_params=pltpu.CompilerParams(
            dimension_semantics=("parallel",), vmem_limit_bytes=VMEM_LIMIT_BYTES),
        name="attn_out_proj",
    )(o.reshape(t, d), w_o.astype(jnp.bfloat16), g_post.reshape(1, d), hin.reshape(t, d))
    return out.reshape(b, s, d)


def kernel(x, norm_mix_pre, norm_mix_post, norm_ffn_pre, norm_ffn_post, fourier_w_o, diff_w_qkv,
           diff_lambda_q1, diff_lambda_k1, diff_lambda_q2, diff_lambda_k2, diff_subln_g, diff_w_o,
           ffn_w_gate, ffn_w_up, ffn_w_down):
    depth = norm_mix_pre.shape[0]
    h = x
    for i in range(depth):
        j = i // N_MIXERS
        if i % N_MIXERS == 0:
            h = _fourier_layer(h, norm_mix_pre[i], norm_mix_post[i], fourier_w_o[j])
        else:
            h = _diff_attention_layer(h, norm_mix_pre[i], norm_mix_post[i], diff_w_qkv[j],
                                      diff_lambda_q1[j], diff_lambda_k1[j], diff_lambda_q2[j],
                                      diff_lambda_k2[j], diff_subln_g[j], diff_w_o[j], i)
        h = _ffn_layer(h, norm_ffn_pre[i], norm_ffn_post[i], ffn_w_gate[i], ffn_w_up[i], ffn_w_down[i])
    return h
```

```python
import functools
import math

import numpy as np
import jax
import jax.numpy as jnp
from jax import lax
from jax.experimental import pallas as pl
from jax.experimental.pallas import tpu as pltpu

N_FOURIER_GROUPS = 8
N_DIFF_HEADS = 8
RMS_EPS = 1e-6
N_MIXERS = 2

LANES = 128
VMEM_LIMIT_BYTES = 56 * 1024 * 1024

FFT_N2 = 128
FFT_S2_PER_STEP = 16
FFT_K1_PER_STEP = 8
TOKEN_TILE = 512
FFN_CHUNK = 256
ATT_TQ = 512
ATT_CK = 2048
POS_SPLIT = 256
LOG2E = 1.4426950408889634
SKIP_EXP2_ZERO = 153.0
SKIP_SLACK = 1.01


def _rms_scale(x):
    return lax.rsqrt(jnp.mean(x * x, axis=-1, keepdims=True) + RMS_EPS)


def _np_bf16(v):
    return np.asarray(v, np.float32).astype(jnp.bfloat16)


def _bf16_split3(v):
    v = np.asarray(v, np.float64)
    hi = _np_bf16(v).astype(np.float64)
    mid = _np_bf16(v - hi).astype(np.float64)
    lo = _np_bf16(v - hi - mid).astype(np.float64)
    return hi, mid, lo


def _fft_stage1_kernel(x_ref, g_ref, f1k_ref, ar_ref, ai_ref, *, n1, d, nb):
    x = x_ref[...].reshape(n1 * nb, d)
    xn = (x * _rms_scale(x) * g_ref[...]).astype(jnp.bfloat16)
    a = jnp.dot(f1k_ref[...], xn, preferred_element_type=jnp.float32)
    ar_ref[...] = a[:n1 * nb].astype(jnp.bfloat16).reshape(n1, nb, d)
    ai_ref[...] = a[n1 * nb:].astype(jnp.bfloat16).reshape(n1, nb, d)


def _fft_stage2_kernel(ar_ref, ai_ref, gm_ref, cs_ref, perm_ref, wo_ref, gpost_ref, xres_ref, out_ref,
                       *, n2, d, cg, kb):
    cs = cs_ref[...]
    zs = []
    for i in range(kb):
        a = jnp.concatenate([ar_ref[i], ai_ref[i]], axis=0)
        y = jnp.dot(gm_ref[i], a, preferred_element_type=jnp.float32)
        yr = y[:n2].astype(jnp.bfloat16)
        yi = y[n2:].astype(jnp.bfloat16)
        cols = []
        for gi in range(d // cg):
            lhs = jnp.concatenate([yr[:, gi * cg:(gi + 1) * cg], yi[:, gi * cg:(gi + 1) * cg]], axis=1)
            cols.append(jnp.dot(lhs, cs, preferred_element_type=jnp.float32))
        zs.append(jnp.concatenate(cols, axis=1).astype(jnp.bfloat16))
    z = jnp.concatenate(zs, axis=0)
    z = jnp.dot(perm_ref[...], z, preferred_element_type=jnp.float32).astype(jnp.bfloat16)
    m = jnp.dot(z, wo_ref[...], preferred_element_type=jnp.float32)
    r = m * _rms_scale(m) * gpost_ref[...]
    out_ref[...] = xres_ref[...] + r.reshape(n2, kb, d)


def _fourier_layer(x, g_pre, g_post, w_o):
    b, s, d = x.shape
    n2 = FFT_N2
    n1 = s // n2
    cg = d // N_FOURIER_GROUPS
    nb = FFT_S2_PER_STEP
    assert n1 * n2 == s and n2 % nb == 0

    s1 = np.arange(n1)
    th1 = 2.0 * np.pi * ((s1[:, None] * s1[None, :]) % n1) / n1
    f1 = np.concatenate([np.cos(th1), -np.sin(th1)], axis=0)
    s2 = np.arange(n2)
    kk = (np.arange(n1)[:, None, None] + n1 * np.arange(n2)[None, :, None])
    th2 = 2.0 * np.pi * ((kk * s2[None, None, :]) % s) / s
    gr, gi = np.cos(th2), -np.sin(th2)
    gm = np.concatenate([np.concatenate([gr, -gi], axis=2),
                         np.concatenate([gi, gr], axis=2)], axis=1)
    c = np.arange(cg)
    thc = 2.0 * np.pi * ((c[:, None] * c[None, :]) % cg) / cg
    norm = 1.0 / math.sqrt(s * cg)
    cs = np.concatenate([np.cos(thc), np.sin(thc)], axis=0) * norm

    f1k = jnp.asarray(_np_bf16(np.kron(f1, np.eye(nb))))
    gm = jnp.asarray(_np_bf16(gm))
    cs = jnp.asarray(_np_bf16(cs))

    x4 = x.reshape(b, n1, n2, d)
    ar, ai = pl.pallas_call(
        functools.partial(_fft_stage1_kernel, n1=n1, d=d, nb=nb),
        grid=(b, n2 // nb),
        in_specs=[
            pl.BlockSpec((None, n1, nb, d), lambda i, j: (i, 0, j, 0)),
            pl.BlockSpec((1, d), lambda i, j: (0, 0)),
            pl.BlockSpec((2 * n1 * nb, n1 * nb), lambda i, j: (0, 0), pipeline_mode=pl.Buffered(1)),
        ],
        out_specs=[
            pl.BlockSpec((None, n1, nb, d), lambda i, j: (i, 0, j, 0)),
            pl.BlockSpec((None, n1, nb, d), lambda i, j: (i, 0, j, 0)),
        ],
        out_shape=[jax.ShapeDtypeStruct((b, n1, n2, d), jnp.bfloat16)] * 2,
        compiler_params=pltpu.CompilerParams(
            dimension_semantics=("parallel", "parallel"), vmem_limit_bytes=VMEM_LIMIT_BYTES),
        name="fft_stage1",
    )(x4, g_pre.reshape(1, d), f1k)

    kb = min(FFT_K1_PER_STEP, n1)
    assert n1 % kb == 0
    src = (np.arange(kb)[None, :] * n2 + np.arange(n2)[:, None]).reshape(-1)
    perm = np.zeros((kb * n2, kb * n2), np.float32)
    perm[np.arange(kb * n2), src] = 1.0
    perm = jnp.asarray(_np_bf16(perm))
    xk = x.reshape(b, n2, n1, d)
    out = pl.pallas_call(
        functools.partial(_fft_stage2_kernel, n2=n2, d=d, cg=cg, kb=kb),
        grid=(b, n1 // kb),
        in_specs=[
            pl.BlockSpec((None, kb, n2, d), lambda i, j: (i, j, 0, 0)),
            pl.BlockSpec((None, kb, n2, d), lambda i, j: (i, j, 0, 0)),
            pl.BlockSpec((kb, 2 * n2, 2 * n2), lambda i, j: (j, 0, 0)),
            pl.BlockSpec((2 * cg, cg), lambda i, j: (0, 0)),
            pl.BlockSpec((kb * n2, kb * n2), lambda i, j: (0, 0)),
            pl.BlockSpec((d, d), lambda i, j: (0, 0)),
            pl.BlockSpec((1, d), lambda i, j: (0, 0)),
            pl.BlockSpec((None, n2, kb, d), lambda i, j: (i, 0, j, 0)),
        ],
        out_specs=pl.BlockSpec((None, n2, kb, d), lambda i, j: (i, 0, j, 0)),
        out_shape=jax.ShapeDtypeStruct((b, n2, n1, d), jnp.float32),
        compiler_params=pltpu.CompilerParams(
            dimension_semantics=("parallel", "parallel"), vmem_limit_bytes=VMEM_LIMIT_BYTES),
        name="fft_stage2",
    )(ar, ai, gm, cs, perm, w_o.astype(jnp.bfloat16), g_post.reshape(1, d), xk)
    return out.reshape(b, s, d)


def _ffn_kernel(x_ref, gpre_ref, gpost_ref, wg_ref, wu_ref, wd_ref, out_ref, *, f, chunk):
    x = x_ref[...]
    xn = (x * _rms_scale(x) * gpre_ref[...]).astype(jnp.bfloat16)
    acc = jnp.zeros(x.shape, jnp.float32)
    for c0 in range(0, f, chunk):
        c1 = min(c0 + chunk, f)
        gt = jnp.dot(xn, wg_ref[:, c0:c1], preferred_element_type=jnp.float32)
        up = jnp.dot(xn, wu_ref[:, c0:c1], preferred_element_type=jnp.float32)
        hh = (gt * jax.nn.sigmoid(gt) * up).astype(jnp.bfloat16)
        acc = acc + jnp.dot(hh, wd_ref[c0:c1, :], preferred_element_type=jnp.float32)
    out_ref[...] = x + acc * _rms_scale(acc) * gpost_ref[...]


def _ffn_layer(h, g_pre, g_post, w_gate, w_up, w_down):
    b, s, d = h.shape
    f = w_gate.shape[1]
    t = b * s
    tm = min(TOKEN_TILE, t)
    assert t % tm == 0
    const = lambda i: (0, 0)
    out = pl.pallas_call(
        functools.partial(_ffn_kernel, f=f, chunk=FFN_CHUNK),
        grid=(t // tm,),
        in_specs=[
            pl.BlockSpec((tm, d), lambda i: (i, 0)),
            pl.BlockSpec((1, d), const),
            pl.BlockSpec((1, d), const),
            pl.BlockSpec((d, f), const, pipeline_mode=pl.Buffered(1)),
            pl.BlockSpec((d, f), const, pipeline_mode=pl.Buffered(1)),
            pl.BlockSpec((f, d), const, pipeline_mode=pl.Buffered(1)),
        ],
        out_specs=pl.BlockSpec((tm, d), lambda i: (i, 0)),
        out_shape=jax.ShapeDtypeStruct((t, d), jnp.float32),
        compiler_params=pltpu.CompilerParams(
            dimension_semantics=("parallel",), vmem_limit_bytes=VMEM_LIMIT_BYTES),
        name="swiglu_ffn",
    )(h.reshape(t, d), g_pre.reshape(1, d), g_post.reshape(1, d),
      w_gate.astype(jnp.bfloat16), w_up.astype(jnp.bfloat16), w_down.astype(jnp.bfloat16))
    return out.reshape(b, s, d)


def _qkv_kernel(x_ref, gpre_ref, w_ref, qa_ref, ka_ref, q_ref, k_ref, v_ref, *, d, nh, tm, qscale):
    x = x_ref[...]
    xn = (x * _rms_scale(x) * gpre_ref[...]).astype(jnp.bfloat16)
    dh2 = d // nh
    row = pl.program_id(1) * tm + lax.broadcasted_iota(jnp.int32, (tm, 1), 0)
    pos_lo = (row & (POS_SPLIT - 1)).astype(jnp.float32)
    pos_hi = (row & (-POS_SPLIT)).astype(jnp.float32)
    lane = lax.broadcasted_iota(jnp.int32, (1, dh2), 1)
    half = dh2 // 2
    q_all = jnp.dot(xn, w_ref[:, 0:d], preferred_element_type=jnp.float32) * qscale
    k_all = jnp.dot(xn, w_ref[:, d:2 * d], preferred_element_type=jnp.float32)
    v_all = jnp.dot(xn, w_ref[:, 2 * d:3 * d], preferred_element_type=jnp.float32)
    ones_col = jnp.where(lane == 0, 1.0, 0.0).astype(jnp.bfloat16)
    for h in range(nh):
        qh = q_all[:, h * dh2:(h + 1) * dh2]
        kh = k_all[:, h * dh2:(h + 1) * dh2]
        for comp in range(2):
            data = (lane < half) if comp == 0 else (lane >= half)
            qa = qa_ref[h, comp]
            ka = ka_ref[h, comp]
            q_aug = qa[0:1] + qa[1:2] * pos_lo + qa[2:3] * pos_hi
            k_aug = ka[0:1] + ka[1:2] * pos_lo + ka[2:3] * pos_hi
            q_ref[h, comp] = jnp.where(data, qh, q_aug).astype(jnp.bfloat16)
            k_ref[h, comp] = jnp.where(data, kh, k_aug).astype(jnp.bfloat16)
        vh = v_all[:, h * dh2:(h + 1) * dh2].astype(jnp.bfloat16)
        v_ref[h] = jnp.concatenate([vh, jnp.broadcast_to(ones_col, (tm, dh2))], axis=1)


def _aug_tables(nh, dh2, cvals):
    half = dh2 // 2
    qa = np.zeros((nh, 2, 3, dh2), np.float64)
    ka = np.zeros((nh, 2, 3, dh2), np.float64)
    for h in range(nh):
        pieces = _bf16_split3(cvals[h])
        for comp in range(2):
            base = half if comp == 0 else 0
            for p in range(3):
                for part in (1, 2):
                    qa[h, comp, 0, base + 3 * (part - 1) + p] = pieces[p]
                    ka[h, comp, part, base + 3 * (part - 1) + p] = 1.0
                    qa[h, comp, part, base + 6 + 3 * (part - 1) + p] = -1.0
                    ka[h, comp, 0, base + 6 + 3 * (part - 1) + p] = pieces[p]
    return jnp.asarray(qa, jnp.float32), jnp.asarray(ka, jnp.float32)


def _attn_kernel(invc_ref, q_ref, k_ref, v_ref, bias_ref, lam_ref, g_ref, o_ref, m_scr, acc_scr, ksq_scr,
                 *, s, dv, lam_init):
    tq, ck = ATT_TQ, ATT_CK
    nch = s // ck
    hd = pl.program_id(1)
    qi = pl.program_id(2)
    i0 = qi * tq
    lane = lax.broadcasted_iota(jnp.int32, (1, LANES), 1)
    half = LANES // 2

    m_scr[...] = jnp.full(m_scr.shape, -1e30, jnp.float32)
    acc_scr[...] = jnp.zeros(acc_scr.shape, jnp.float32)

    @pl.when(qi == 0)
    def _():
        for comp in range(2):
            data = (lane < half) if comp == 0 else (lane >= half)
            best = jnp.zeros((tq, 1), jnp.float32)
            for r0 in range(0, s, tq):
                kd = jnp.where(data, k_ref[comp, r0:r0 + tq, :].astype(jnp.float32), 0.0)
                best = jnp.maximum(best, jnp.sum(kd * kd, axis=-1, keepdims=True))
            ksq_scr[comp] = jnp.broadcast_to(jnp.max(best, axis=0, keepdims=True), (1, LANES))

    q_left, q_right, q_diag = [], [], []
    u2 = jnp.zeros((1, 1), jnp.float32)
    for comp in range(2):
        data = (lane < half) if comp == 0 else (lane >= half)
        qc = q_ref[comp]
        q_left.append(qc)
        q_right.append(jnp.where(data, qc, -qc))
        q_diag.append(jnp.where(data, qc, jnp.zeros_like(qc)))
        qd = q_diag[comp].astype(jnp.float32)
        qsq = jnp.max(jnp.sum(qd * qd, axis=-1, keepdims=True), axis=0, keepdims=True)
        u2 = jnp.maximum(u2, qsq * ksq_scr[comp][:, :1])
    thr = jnp.max(2.0 * SKIP_SLACK * jnp.sqrt(u2) + SKIP_EXP2_ZERO)

    def scores(j0, table_off):
        j0 = pl.multiple_of(j0, POS_SPLIT)
        ts = []
        for comp in range(2):
            kc = k_ref[comp, pl.ds(j0, ck), :]
            if table_off is None:
                qv = jnp.where(j0 < i0, q_left[comp], q_right[comp])
            else:
                qv = q_diag[comp]
            t = lax.dot_general(qv, kc, (((1,), (1,)), ((), ())),
                                preferred_element_type=jnp.float32)
            if table_off is not None:
                t = t + bias_ref[:, pl.ds(pl.multiple_of(table_off, POS_SPLIT), ck)]
            ts.append(t)
        return j0, ts

    def accumulate(j0, ts):
        vc = v_ref[pl.ds(j0, ck), :]
        for comp in range(2):
            t = ts[comp]
            m_old = m_scr[comp]
            m_new = jnp.maximum(m_old, jnp.max(t, axis=-1, keepdims=True))
            alpha = jnp.exp2(m_old - m_new)
            p = jnp.exp2(t - m_new).astype(jnp.bfloat16)
            acc_scr[comp] = alpha * acc_scr[comp] + jnp.dot(p, vc, preferred_element_type=jnp.float32)
            m_scr[comp] = m_new

    reach = jnp.minimum(thr * invc_ref[hd], float(s)).astype(jnp.int32) + 1
    lo_key = jnp.maximum(i0 - reach, 0)
    hi_key = jnp.minimum(i0 + tq + reach, s)
    start0 = lo_key & (-POS_SPLIT)
    start0 = jnp.where(((i0 - start0) & (ck - 1)) + tq > ck, start0 - POS_SPLIT, start0)
    width_need = (hi_key - start0 + (ck - 1)) // ck

    def sweep(width):
        def run():
            start = jnp.minimum(start0, s - width * ck)
            rel = i0 - start
            own = rel // ck
            pending = scores(start + own * ck, (ck - tq) - (rel & (ck - 1)))
            for n in range(1, width):
                r = own + n
                nxt = scores(start + jnp.where(r >= width, r - width, r) * ck, None)
                accumulate(*pending)
                pending = nxt
            accumulate(*pending)
        return run

    def dispatch(width):
        if width == nch:
            sweep(nch)()
        else:
            lax.cond(width_need <= width, sweep(width), lambda: dispatch(width + 1))

    dispatch(1)

    lam_p = lam_ref[...]
    lam = (jnp.exp(jnp.sum(lam_p[0:1] * lam_p[1:2], axis=-1, keepdims=True))
           - jnp.exp(jnp.sum(lam_p[2:3] * lam_p[3:4], axis=-1, keepdims=True)) + lam_init)
    a0 = acc_scr[0]
    a1 = acc_scr[1]
    o = a0[:, :dv] / a0[:, dv:dv + 1] - lam * (a1[:, :dv] / a1[:, dv:dv + 1])
    o = o * _rms_scale(o) * g_ref[...] * (1.0 - lam_init)
    o_ref[...] = o.astype(jnp.bfloat16)


def _proj_post_kernel(a_ref, w_ref, gpost_ref, xres_ref, out_ref):
    m = jnp.dot(a_ref[...], w_ref[...], preferred_element_type=jnp.float32)
    out_ref[...] = xres_ref[...] + m * _rms_scale(m) * gpost_ref[...]


def _diff_attention_layer(hin, g_pre, g_post, w_qkv, lq1, lk1, lq2, lk2, subln_g, w_o, layer_idx):
    b, s, d = hin.shape
    nh = N_DIFF_HEADS
    dh2 = d // nh
    dh = dh2 // 2
    dv = dh2
    assert dh2 == LANES and s % ATT_CK == 0
    lam_init = 0.8 - 0.6 * math.exp(-0.3 * layer_idx)
    slopes = np.asarray([2.0 ** (-8.0 * (i + 1) / nh) for i in range(nh)], np.float64)
    cvals = slopes * LOG2E
    qa, ka = _aug_tables(nh, dh2, cvals)
    qscale = dh ** -0.5 * LOG2E

    tm = min(TOKEN_TILE, s)
    const2 = lambda i, j: (0, 0)
    q, k, v = pl.pallas_call(
        functools.partial(_qkv_kernel, d=d, nh=nh, tm=tm, qscale=qscale),
        grid=(b, s // tm),
        in_specs=[
            pl.BlockSpec((None, tm, d), lambda i, j: (i, j, 0)),
            pl.BlockSpec((1, d), const2),
            pl.BlockSpec((d, 3 * d), const2, pipeline_mode=pl.Buffered(1)),
            pl.BlockSpec((nh, 2, 3, dh2), lambda i, j: (0, 0, 0, 0)),
            pl.BlockSpec((nh, 2, 3, dh2), lambda i, j: (0, 0, 0, 0)),
        ],
        out_specs=[
            pl.BlockSpec((None, nh, 2, tm, dh2), lambda i, j: (i, 0, 0, j, 0)),
            pl.BlockSpec((None, nh, 2, tm, dh2), lambda i, j: (i, 0, 0, j, 0)),
            pl.BlockSpec((None, nh, tm, 2 * dv), lambda i, j: (i, 0, j, 0)),
        ],
        out_shape=[
            jax.ShapeDtypeStruct((b, nh, 2, s, dh2), jnp.bfloat16),
            jax.ShapeDtypeStruct((b, nh, 2, s, dh2), jnp.bfloat16),
            jax.ShapeDtypeStruct((b, nh, s, 2 * dv), jnp.bfloat16),
        ],
        compiler_params=pltpu.CompilerParams(
            dimension_semantics=("parallel", "parallel"), vmem_limit_bytes=VMEM_LIMIT_BYTES),
        name="qkv_proj",
    )(hin, g_pre.reshape(1, d), w_qkv.astype(jnp.bfloat16), qa, ka)

    tq, ck = ATT_TQ, ATT_CK
    wdt = ck + (ck - tq)
    dist = np.abs(np.arange(tq)[:, None] - np.arange(wdt)[None, :] + (ck - tq))
    bias = jnp.asarray(-cvals[:, None, None] * dist[None], jnp.float32)
    lam_p = jnp.stack([lq1, lk1, lq2, lk2]).astype(jnp.float32)

    o = pl.pallas_call(
        functools.partial(_attn_kernel, s=s, dv=dv, lam_init=lam_init),
        grid=(b, nh, s // tq),
        in_specs=[
            pl.BlockSpec(memory_space=pltpu.SMEM),
            pl.BlockSpec((None, None, 2, tq, dh2), lambda i, h, j: (i, h, 0, j, 0)),
            pl.BlockSpec((None, None, 2, s, dh2), lambda i, h, j: (i, h, 0, 0, 0)),
            pl.BlockSpec((None, None, s, 2 * dv), lambda i, h, j: (i, h, 0, 0)),
            pl.BlockSpec((None, tq, wdt), lambda i, h, j: (h, 0, 0)),
            pl.BlockSpec((4, dh), lambda i, h, j: (0, 0)),
            pl.BlockSpec((1, dv), lambda i, h, j: (0, 0)),
        ],
        out_specs=pl.BlockSpec((None, tq, dv), lambda i, h, j: (i, j, h)),
        out_shape=jax.ShapeDtypeStruct((b, s, nh * dv), jnp.bfloat16),
        scratch_shapes=[
            pltpu.VMEM((2, tq, 1), jnp.float32),
            pltpu.VMEM((2, tq, 2 * dv), jnp.float32),
            pltpu.VMEM((2, 1, LANES), jnp.float32),
        ],
        compiler_params=pltpu.CompilerParams(
            dimension_semantics=("parallel", "parallel", "arbitrary"),
            vmem_limit_bytes=VMEM_LIMIT_BYTES),
        name="diff_attention",
    )(jnp.asarray(1.0 / cvals, jnp.float32), q, k, v, bias, lam_p, subln_g.reshape(1, dv))

    t = b * s
    tmo = min(TOKEN_TILE, t)
    const1 = lambda i: (0, 0)
    out = pl.pallas_call(
        _proj_post_kernel,
        grid=(t // tmo,),
        in_specs=[
            pl.BlockSpec((tmo, d), lambda i: (i, 0)),
            pl.BlockSpec((d, d), const1),
            pl.BlockSpec((1, d), const1),
            pl.BlockSpec((tmo, d), lambda i: (i, 0)),
        ],
        out_specs=pl.BlockSpec((tmo, d), lambda i: (i, 0)),
        out_shape=jax.ShapeDtypeStruct((t, d), jnp.float32),
        compiler_params=pltpu.CompilerParams(
            dimension_semantics=("parallel",), vmem_limit_bytes=VMEM_LIMIT_BYTES),
        name="attn_out_proj",
    )(o.reshape(t, d), w_o.astype(jnp.bfloat16), g_post.reshape(1, d), hin.reshape(t, d))
    return out.reshape(b, s, d)


def kernel(x, norm_mix_pre, norm_mix_post, norm_ffn_pre, norm_ffn_post, fourier_w_o, diff_w_qkv,
           diff_lambda_q1, diff_lambda_k1, diff_lambda_q2, diff_lambda_k2, diff_subln_g, diff_w_o,
           ffn_w_gate, ffn_w_up, ffn_w_down):
    depth = norm_mix_pre.shape[0]
    h = x
    for i in range(depth):
        j = i // N_MIXERS
        if i % N_MIXERS == 0:
            h = _fourier_layer(h, norm_mix_pre[i], norm_mix_post[i], fourier_w_o[j])
        else:
            h = _diff_attention_layer(h, norm_mix_pre[i], norm_mix_post[i], diff_w_qkv[j],
                                      diff_lambda_q1[j], diff_lambda_k1[j], diff_lambda_q2[j],
                                      diff_lambda_k2[j], diff_subln_g[j], diff_w_o[j], i)
        h = _ffn_layer(h, norm_ffn_pre[i], norm_ffn_post[i], ffn_w_gate[i], ffn_w_up[i], ffn_w_down[i])
    return h
```

```python
import functools
import math

import numpy as np
import jax
import jax.numpy as jnp
from jax import lax
from jax.experimental import pallas as pl
from jax.experimental.pallas import tpu as pltpu

N_FOURIER_GROUPS = 8
N_DIFF_HEADS = 8
RMS_EPS = 1e-6
N_MIXERS = 2

LANES = 128
VMEM_LIMIT_BYTES = 56 * 1024 * 1024

FFT_N2 = 128
FFT_S2_PER_STEP = 16
FFT_K1_PER_STEP = 8
TOKEN_TILE = 512
FFN_CHUNK = 256
ATT_TQ = 512
ATT_CK = 2048
POS_SPLIT = 256
LOG2E = 1.4426950408889634
SKIP_EXP2_ZERO = 153.0
SKIP_SLACK = 1.01


def _rms_scale(x):
    return lax.rsqrt(jnp.mean(x * x, axis=-1, keepdims=True) + RMS_EPS)


def _np_bf16(v):
    return np.asarray(v, np.float32).astype(jnp.bfloat16)


def _bf16_split3(v):
    v = np.asarray(v, np.float64)
    hi = _np_bf16(v).astype(np.float64)
    mid = _np_bf16(v - hi).astype(np.float64)
    lo = _np_bf16(v - hi - mid).astype(np.float64)
    return hi, mid, lo


def _fft_stage1_kernel(x_ref, g_ref, f1k_ref, ar_ref, ai_ref, *, n1, d, nb):
    x = x_ref[...].reshape(n1 * nb, d)
    xn = (x * _rms_scale(x) * g_ref[...]).astype(jnp.bfloat16)
    a = jnp.dot(f1k_ref[...], xn, preferred_element_type=jnp.float32)
    ar_ref[...] = a[:n1 * nb].astype(jnp.bfloat16).reshape(n1, nb, d)
    ai_ref[...] = a[n1 * nb:].astype(jnp.bfloat16).reshape(n1, nb, d)


def _fft_stage2_kernel(ar_ref, ai_ref, gm_ref, cs_ref, perm_ref, wo_ref, gpost_ref, xres_ref, out_ref,
                       *, n2, d, cg, kb):
    cs = cs_ref[...]
    zs = []
    for i in range(kb):
        a = jnp.concatenate([ar_ref[i], ai_ref[i]], axis=0)
        y = jnp.dot(gm_ref[i], a, preferred_element_type=jnp.float32)
        yr = y[:n2].astype(jnp.bfloat16)
        yi = y[n2:].astype(jnp.bfloat16)
        cols = []
        for gi in range(d // cg):
            lhs = jnp.concatenate([yr[:, gi * cg:(gi + 1) * cg], yi[:, gi * cg:(gi + 1) * cg]], axis=1)
            cols.append(jnp.dot(lhs, cs, preferred_element_type=jnp.float32))
        zs.append(jnp.concatenate(cols, axis=1).astype(jnp.bfloat16))
    z = jnp.concatenate(zs, axis=0)
    z = jnp.dot(perm_ref[...], z, preferred_element_type=jnp.float32).astype(jnp.bfloat16)
    m = jnp.dot(z, wo_ref[...], preferred_element_type=jnp.float32)
    r = m * _rms_scale(m) * gpost_ref[...]
    out_ref[...] = xres_ref[...] + r.reshape(n2, kb, d)


def _fourier_layer(x, g_pre, g_post, w_o):
    b, s, d = x.shape
    n2 = FFT_N2
    n1 = s // n2
    cg = d // N_FOURIER_GROUPS
    nb = FFT_S2_PER_STEP
    assert n1 * n2 == s and n2 % nb == 0

    s1 = np.arange(n1)
    th1 = 2.0 * np.pi * ((s1[:, None] * s1[None, :]) % n1) / n1
    f1 = np.concatenate([np.cos(th1), -np.sin(th1)], axis=0)
    s2 = np.arange(n2)
    kk = (np.arange(n1)[:, None, None] + n1 * np.arange(n2)[None, :, None])
    th2 = 2.0 * np.pi * ((kk * s2[None, None, :]) % s) / s
    gr, gi = np.cos(th2), -np.sin(th2)
    gm = np.concatenate([np.concatenate([gr, -gi], axis=2),
                         np.concatenate([gi, gr], axis=2)], axis=1)
    c = np.arange(cg)
    thc = 2.0 * np.pi * ((c[:, None] * c[None, :]) % cg) / cg
    norm = 1.0 / math.sqrt(s * cg)
    cs = np.concatenate([np.cos(thc), np.sin(thc)], axis=0) * norm

    f1k = jnp.asarray(_np_bf16(np.kron(f1, np.eye(nb))))
    gm = jnp.asarray(_np_bf16(gm))
    cs = jnp.asarray(_np_bf16(cs))

    x4 = x.reshape(b, n1, n2, d)
    ar, ai = pl.pallas_call(
        functools.partial(_fft_stage1_kernel, n1=n1, d=d, nb=nb),
        grid=(b, n2 // nb),
        in_specs=[
            pl.BlockSpec((None, n1, nb, d), lambda i, j: (i, 0, j, 0)),
            pl.BlockSpec((1, d), lambda i, j: (0, 0)),
            pl.BlockSpec((2 * n1 * nb, n1 * nb), lambda i, j: (0, 0), pipeline_mode=pl.Buffered(1)),
        ],
        out_specs=[
            pl.BlockSpec((None, n1, nb, d), lambda i, j: (i, 0, j, 0)),
            pl.BlockSpec((None, n1, nb, d), lambda i, j: (i, 0, j, 0)),
        ],
        out_shape=[jax.ShapeDtypeStruct((b, n1, n2, d), jnp.bfloat16)] * 2,
        compiler_params=pltpu.CompilerParams(
            dimension_semantics=("parallel", "parallel"), vmem_limit_bytes=VMEM_LIMIT_BYTES),
        name="fft_stage1",
    )(x4, g_pre.reshape(1, d), f1k)

    kb = min(FFT_K1_PER_STEP, n1)
    assert n1 % kb == 0
    src = (np.arange(kb)[None, :] * n2 + np.arange(n2)[:, None]).reshape(-1)
    perm = np.zeros((kb * n2, kb * n2), np.float32)
    perm[np.arange(kb * n2), src] = 1.0
    perm = jnp.asarray(_np_bf16(perm))
    xk = x.reshape(b, n2, n1, d)
    out = pl.pallas_call(
        functools.partial(_fft_stage2_kernel, n2=n2, d=d, cg=cg, kb=kb),
        grid=(b, n1 // kb),
        in_specs=[
            pl.BlockSpec((None, kb, n2, d), lambda i, j: (i, j, 0, 0)),
            pl.BlockSpec((None, kb, n2, d), lambda i, j: (i, j, 0, 0)),
            pl.BlockSpec((kb, 2 * n2, 2 * n2), lambda i, j: (j, 0, 0)),
            pl.BlockSpec((2 * cg, cg), lambda i, j: (0, 0)),
            pl.BlockSpec((kb * n2, kb * n2), lambda i, j: (0, 0)),
            pl.BlockSpec((d, d), lambda i, j: (0, 0)),
            pl.BlockSpec((1, d), lambda i, j: (0, 0)),
            pl.BlockSpec((None, n2, kb, d), lambda i, j: (i, 0, j, 0)),
        ],
        out_specs=pl.BlockSpec((None, n2, kb, d), lambda i, j: (i, 0, j, 0)),
        out_shape=jax.ShapeDtypeStruct((b, n2, n1, d), jnp.float32),
        compiler_params=pltpu.CompilerParams(
            dimension_semantics=("parallel", "parallel"), vmem_limit_bytes=VMEM_LIMIT_BYTES),
        name="fft_stage2",
    )(ar, ai, gm, cs, perm, w_o.astype(jnp.bfloat16), g_post.reshape(1, d), xk)
    return out.reshape(b, s, d)


def _ffn_kernel(*refs, f, chunk, mixer_proj):
    if mixer_proj:
        a_ref, wo_ref, gmix_ref, x_ref, gpre_ref, gpost_ref, wg_ref, wu_ref, wd_ref, out_ref = refs
        m = jnp.dot(a_ref[...], wo_ref[...], preferred_element_type=jnp.float32)
        x = x_ref[...] + m * _rms_scale(m) * gmix_ref[...]
    else:
        x_ref, gpre_ref, gpost_ref, wg_ref, wu_ref, wd_ref, out_ref = refs
        x = x_ref[...]
    xn = (x * _rms_scale(x) * gpre_ref[...]).astype(jnp.bfloat16)
    acc = jnp.zeros(x.shape, jnp.float32)
    for c0 in range(0, f, chunk):
        c1 = min(c0 + chunk, f)
        gt = jnp.dot(xn, wg_ref[:, c0:c1], preferred_element_type=jnp.float32)
        up = jnp.dot(xn, wu_ref[:, c0:c1], preferred_element_type=jnp.float32)
        hh = (gt * jax.nn.sigmoid(gt) * up).astype(jnp.bfloat16)
        acc = acc + jnp.dot(hh, wd_ref[c0:c1, :], preferred_element_type=jnp.float32)
    out_ref[...] = x + acc * _rms_scale(acc) * gpost_ref[...]


def _ffn_layer(h, g_pre, g_post, w_gate, w_up, w_down, mixer=None):
    b, s, d = h.shape
    f = w_gate.shape[1]
    t = b * s
    tm = min(TOKEN_TILE, t)
    assert t % tm == 0
    const = lambda i: (0, 0)
    rows = pl.BlockSpec((tm, d), lambda i: (i, 0))
    vec = pl.BlockSpec((1, d), const)
    in_specs = [rows, vec, vec,
                pl.BlockSpec((d, f), const, pipeline_mode=pl.Buffered(1)),
                pl.BlockSpec((d, f), const, pipeline_mode=pl.Buffered(1)),
                pl.BlockSpec((f, d), const, pipeline_mode=pl.Buffered(1))]
    args = [h.reshape(t, d), g_pre.reshape(1, d), g_post.reshape(1, d),
            w_gate.astype(jnp.bfloat16), w_up.astype(jnp.bfloat16), w_down.astype(jnp.bfloat16)]
    if mixer is not None:
        a, w_o, g_mix = mixer
        in_specs = [rows, pl.BlockSpec((d, d), const, pipeline_mode=pl.Buffered(1)), vec] + in_specs
        args = [a.reshape(t, d), w_o.astype(jnp.bfloat16), g_mix.reshape(1, d)] + args
    out = pl.pallas_call(
        functools.partial(_ffn_kernel, f=f, chunk=FFN_CHUNK, mixer_proj=mixer is not None),
        grid=(t // tm,),
        in_specs=in_specs,
        out_specs=rows,
        out_shape=jax.ShapeDtypeStruct((t, d), jnp.float32),
        compiler_params=pltpu.CompilerParams(
            dimension_semantics=("parallel",), vmem_limit_bytes=VMEM_LIMIT_BYTES),
        name="swiglu_ffn",
    )(*args)
    return out.reshape(b, s, d)


def _qkv_kernel(x_ref, gpre_ref, w_ref, qa_ref, ka_ref, q_ref, k_ref, v_ref, kn_ref,
                *, d, nh, tm, qscale):
    x = x_ref[...]
    xn = (x * _rms_scale(x) * gpre_ref[...]).astype(jnp.bfloat16)
    dh2 = d // nh
    row = pl.program_id(1) * tm + lax.broadcasted_iota(jnp.int32, (tm, 1), 0)
    pos_lo = (row & (POS_SPLIT - 1)).astype(jnp.float32)
    pos_hi = (row & (-POS_SPLIT)).astype(jnp.float32)
    lane = lax.broadcasted_iota(jnp.int32, (1, dh2), 1)
    half = dh2 // 2
    q_all = jnp.dot(xn, w_ref[:, 0:d], preferred_element_type=jnp.float32) * qscale
    k_all = jnp.dot(xn, w_ref[:, d:2 * d], preferred_element_type=jnp.float32)
    v_all = jnp.dot(xn, w_ref[:, 2 * d:3 * d], preferred_element_type=jnp.float32)
    ones_col = jnp.where(lane == 0, 1.0, 0.0).astype(jnp.bfloat16)
    lane_col = lax.broadcasted_iota(jnp.int32, (dh2, 1), 0)
    comp_sum = jnp.where((lane_col >= half).astype(jnp.int32) == lane, 1.0, 0.0).astype(jnp.bfloat16)
    for h in range(nh):
        qh = q_all[:, h * dh2:(h + 1) * dh2]
        kh = k_all[:, h * dh2:(h + 1) * dh2]
        kb = kh.astype(jnp.bfloat16).astype(jnp.float32)
        ksq = jnp.dot((kb * kb).astype(jnp.bfloat16), comp_sum, preferred_element_type=jnp.float32)
        kn_ref[h] = jnp.max(ksq, axis=0, keepdims=True)
        for comp in range(2):
            data = (lane < half) if comp == 0 else (lane >= half)
            qa = qa_ref[h, comp]
            ka = ka_ref[h, comp]
            q_aug = qa[0:1] + qa[1:2] * pos_lo + qa[2:3] * pos_hi
            k_aug = ka[0:1] + ka[1:2] * pos_lo + ka[2:3] * pos_hi
            q_ref[h, comp] = jnp.where(data, qh, q_aug).astype(jnp.bfloat16)
            k_ref[h, comp] = jnp.where(data, kh, k_aug).astype(jnp.bfloat16)
        vh = v_all[:, h * dh2:(h + 1) * dh2].astype(jnp.bfloat16)
        v_ref[h] = jnp.concatenate([vh, jnp.broadcast_to(ones_col, (tm, dh2))], axis=1)


def _aug_tables(nh, dh2, cvals):
    half = dh2 // 2
    qa = np.zeros((nh, 2, 3, dh2), np.float64)
    ka = np.zeros((nh, 2, 3, dh2), np.float64)
    for h in range(nh):
        pieces = _bf16_split3(cvals[h])
        for comp in range(2):
            base = half if comp == 0 else 0
            for p in range(3):
                for part in (1, 2):
                    qa[h, comp, 0, base + 3 * (part - 1) + p] = pieces[p]
                    ka[h, comp, part, base + 3 * (part - 1) + p] = 1.0
                    qa[h, comp, part, base + 6 + 3 * (part - 1) + p] = -1.0
                    ka[h, comp, 0, base + 6 + 3 * (part - 1) + p] = pieces[p]
    return jnp.asarray(qa, jnp.float32), jnp.asarray(ka, jnp.float32)


def _attn_kernel(invc_ref, q_ref, k_ref, v_ref, kn_ref, bias_ref, lam_ref, g_ref, o_ref, m_scr, acc_scr,
                 *, s, dv, lam_init):
    tq, ck = ATT_TQ, ATT_CK
    nch = s // ck
    hd = pl.program_id(1)
    qi = pl.program_id(2)
    i0 = qi * tq
    lane = lax.broadcasted_iota(jnp.int32, (1, LANES), 1)
    half = LANES // 2

    ksq = jnp.max(kn_ref[...], axis=0)
    q_left, q_right, q_diag = [], [], []
    u2 = jnp.zeros((1, 1), jnp.float32)
    for comp in range(2):
        data = (lane < half) if comp == 0 else (lane >= half)
        qc = q_ref[comp]
        q_left.append(qc)
        q_right.append(jnp.where(data, qc, -qc))
        q_diag.append(jnp.where(data, qc, jnp.zeros_like(qc)))
        qd = q_diag[comp].astype(jnp.float32)
        qsq = jnp.max(jnp.sum(qd * qd, axis=-1, keepdims=True), axis=0, keepdims=True)
        u2 = jnp.maximum(u2, qsq * ksq[:, comp:comp + 1])
    thr = jnp.max(2.0 * SKIP_SLACK * jnp.sqrt(u2) + SKIP_EXP2_ZERO)

    def scores(j0, table_off):
        j0 = pl.multiple_of(j0, POS_SPLIT)
        ts = []
        for comp in range(2):
            kc = k_ref[comp, pl.ds(j0, ck), :]
            if table_off is None:
                qv = jnp.where(j0 < i0, q_left[comp], q_right[comp])
            else:
                qv = q_diag[comp]
            t = lax.dot_general(qv, kc, (((1,), (1,)), ((), ())),
                                preferred_element_type=jnp.float32)
            if table_off is not None:
                t = t + bias_ref[:, pl.ds(pl.multiple_of(table_off, POS_SPLIT), ck)]
            ts.append(t)
        return j0, ts

    def accumulate(j0, ts, first):
        vc = v_ref[pl.ds(j0, ck), :]
        for comp in range(2):
            t = ts[comp]
            m_new = jnp.max(t, axis=-1, keepdims=True)
            if not first:
                m_old = m_scr[comp]
                m_new = jnp.maximum(m_old, m_new)
            p = jnp.exp2(t - m_new).astype(jnp.bfloat16)
            pv = jnp.dot(p, vc, preferred_element_type=jnp.float32)
            if first:
                acc_scr[comp] = pv
            else:
                acc_scr[comp] = jnp.exp2(m_old - m_new) * acc_scr[comp] + pv
            m_scr[comp] = m_new

    reach = jnp.minimum(thr * invc_ref[hd], float(s)).astype(jnp.int32) + 1
    lo_key = jnp.maximum(i0 - reach, 0)
    hi_key = jnp.minimum(i0 + tq + reach, s)
    start0 = lo_key & (-POS_SPLIT)
    start0 = jnp.where(((i0 - start0) & (ck - 1)) + tq > ck, start0 - POS_SPLIT, start0)
    width_need = (hi_key - start0 + (ck - 1)) // ck

    def sweep(width):
        def run():
            start = jnp.minimum(start0, s - width * ck)
            rel = i0 - start
            own = rel // ck
            pending = scores(start + own * ck, (ck - tq) - (rel & (ck - 1)))
            for n in range(1, width):
                r = own + n
                nxt = scores(start + jnp.where(r >= width, r - width, r) * ck, None)
                accumulate(*pending, first=(n == 1))
                pending = nxt
            accumulate(*pending, first=(width == 1))
        return run

    def dispatch(width):
        if width == nch:
            sweep(nch)()
        else:
            lax.cond(width_need <= width, sweep(width), lambda: dispatch(width + 1))

    dispatch(1)

    lam_p = lam_ref[...]
    lam = (jnp.exp(jnp.sum(lam_p[0:1] * lam_p[1:2], axis=-1, keepdims=True))
           - jnp.exp(jnp.sum(lam_p[2:3] * lam_p[3:4], axis=-1, keepdims=True)) + lam_init)
    a0 = acc_scr[0]
    a1 = acc_scr[1]
    o = a0[:, :dv] / a0[:, dv:dv + 1] - lam * (a1[:, :dv] / a1[:, dv:dv + 1])
    o = o * _rms_scale(o) * g_ref[...] * (1.0 - lam_init)
    o_ref[...] = o.astype(jnp.bfloat16)


def _diff_attention_mixer(hin, g_pre, w_qkv, lq1, lk1, lq2, lk2, subln_g, layer_idx):
    b, s, d = hin.shape
    nh = N_DIFF_HEADS
    dh2 = d // nh
    dh = dh2 // 2
    dv = dh2
    assert dh2 == LANES and s % ATT_CK == 0
    lam_init = 0.8 - 0.6 * math.exp(-0.3 * layer_idx)
    slopes = np.asarray([2.0 ** (-8.0 * (i + 1) / nh) for i in range(nh)], np.float64)
    cvals = slopes * LOG2E
    qa, ka = _aug_tables(nh, dh2, cvals)
    qscale = dh ** -0.5 * LOG2E

    tm = min(TOKEN_TILE, s)
    const2 = lambda i, j: (0, 0)
    nt = s // tm
    q, k, v, kn = pl.pallas_call(
        functools.partial(_qkv_kernel, d=d, nh=nh, tm=tm, qscale=qscale),
        grid=(b, s // tm),
        in_specs=[
            pl.BlockSpec((None, tm, d), lambda i, j: (i, j, 0)),
            pl.BlockSpec((1, d), const2),
            pl.BlockSpec((d, 3 * d), const2, pipeline_mode=pl.Buffered(1)),
            pl.BlockSpec((nh, 2, 3, dh2), lambda i, j: (0, 0, 0, 0)),
            pl.BlockSpec((nh, 2, 3, dh2), lambda i, j: (0, 0, 0, 0)),
        ],
        out_specs=[
            pl.BlockSpec((None, nh, 2, tm, dh2), lambda i, j: (i, 0, 0, j, 0)),
            pl.BlockSpec((None, nh, 2, tm, dh2), lambda i, j: (i, 0, 0, j, 0)),
            pl.BlockSpec((None, nh, tm, 2 * dv), lambda i, j: (i, 0, j, 0)),
            pl.BlockSpec((None, nh, None, 1, dh2), lambda i, j: (i, 0, j, 0, 0)),
        ],
        out_shape=[
            jax.ShapeDtypeStruct((b, nh, 2, s, dh2), jnp.bfloat16),
            jax.ShapeDtypeStruct((b, nh, 2, s, dh2), jnp.bfloat16),
            jax.ShapeDtypeStruct((b, nh, s, 2 * dv), jnp.bfloat16),
            jax.ShapeDtypeStruct((b, nh, nt, 1, dh2), jnp.float32),
        ],
        compiler_params=pltpu.CompilerParams(
            dimension_semantics=("parallel", "parallel"), vmem_limit_bytes=VMEM_LIMIT_BYTES),
        name="qkv_proj",
    )(hin, g_pre.reshape(1, d), w_qkv.astype(jnp.bfloat16), qa, ka)

    tq, ck = ATT_TQ, ATT_CK
    wdt = ck + (ck - tq)
    dist = np.abs(np.arange(tq)[:, None] - np.arange(wdt)[None, :] + (ck - tq))
    bias = jnp.asarray(-cvals[:, None, None] * dist[None], jnp.float32)
    lam_p = jnp.stack([lq1, lk1, lq2, lk2]).astype(jnp.float32)

    o = pl.pallas_call(
        functools.partial(_attn_kernel, s=s, dv=dv, lam_init=lam_init),
        grid=(b, nh, s // tq),
        in_specs=[
            pl.BlockSpec(memory_space=pltpu.SMEM),
            pl.BlockSpec((None, None, 2, tq, dh2), lambda i, h, j: (i, h, 0, j, 0)),
            pl.BlockSpec((None, None, 2, s, dh2), lambda i, h, j: (i, h, 0, 0, 0)),
            pl.BlockSpec((None, None, s, 2 * dv), lambda i, h, j: (i, h, 0, 0)),
            pl.BlockSpec((None, None, nt, 1, dh2), lambda i, h, j: (i, h, 0, 0, 0)),
            pl.BlockSpec((None, tq, wdt), lambda i, h, j: (h, 0, 0)),
            pl.BlockSpec((4, dh), lambda i, h, j: (0, 0)),
            pl.BlockSpec((1, dv), lambda i, h, j: (0, 0)),
        ],
        out_specs=pl.BlockSpec((None, tq, dv), lambda i, h, j: (i, j, h)),
        out_shape=jax.ShapeDtypeStruct((b, s, nh * dv), jnp.bfloat16),
        scratch_shapes=[
            pltpu.VMEM((2, tq, 1), jnp.float32),
            pltpu.VMEM((2, tq, 2 * dv), jnp.float32),
        ],
        compiler_params=pltpu.CompilerParams(
            dimension_semantics=("parallel", "parallel", "arbitrary"),
            vmem_limit_bytes=VMEM_LIMIT_BYTES),
        name="diff_attention",
    )(jnp.asarray(1.0 / cvals, jnp.float32), q, k, v, kn, bias, lam_p, subln_g.reshape(1, dv))
    return o


def kernel(x, norm_mix_pre, norm_mix_post, norm_ffn_pre, norm_ffn_post, fourier_w_o, diff_w_qkv,
           diff_lambda_q1, diff_lambda_k1, diff_lambda_q2, diff_lambda_k2, diff_subln_g, diff_w_o,
           ffn_w_gate, ffn_w_up, ffn_w_down):
    depth = norm_mix_pre.shape[0]
    h = x
    for i in range(depth):
        j = i // N_MIXERS
        mixer = None
        if i % N_MIXERS == 0:
            h = _fourier_layer(h, norm_mix_pre[i], norm_mix_post[i], fourier_w_o[j])
        else:
            a = _diff_attention_mixer(h, norm_mix_pre[i], diff_w_qkv[j], diff_lambda_q1[j],
                                      diff_lambda_k1[j], diff_lambda_q2[j], diff_lambda_k2[j],
                                      diff_subln_g[j], i)
            mixer = (a, diff_w_o[j], norm_mix_post[i])
        h = _ffn_layer(h, norm_ffn_pre[i], norm_ffn_post[i], ffn_w_gate[i], ffn_w_up[i], ffn_w_down[i],
                       mixer=mixer)
    return h
```

```python
import functools
import math

import numpy as np
import jax
import jax.numpy as jnp
from jax import lax
from jax.experimental import pallas as pl
from jax.experimental.pallas import tpu as pltpu

N_FOURIER_GROUPS = 8
N_DIFF_HEADS = 8
RMS_EPS = 1e-6
N_MIXERS = 2

LANES = 128
VMEM_LIMIT_BYTES = 56 * 1024 * 1024

FFT_N2 = 128
FFT_S2_PER_STEP = 16
FFT_K1_PER_STEP = 8
TOKEN_TILE = 512
FFN_CHUNK = 256
ATT_TQ = 512
ATT_CK = 2048
POS_SPLIT = 256
LOG2E = 1.4426950408889634
SKIP_EXP2_ZERO = 153.0
SKIP_SLACK = 1.01


def _rms_scale(x):
    return lax.rsqrt(jnp.mean(x * x, axis=-1, keepdims=True) + RMS_EPS)


def _np_bf16(v):
    return np.asarray(v, np.float32).astype(jnp.bfloat16)


def _bf16_split3(v):
    v = np.asarray(v, np.float64)
    hi = _np_bf16(v).astype(np.float64)
    mid = _np_bf16(v - hi).astype(np.float64)
    lo = _np_bf16(v - hi - mid).astype(np.float64)
    return hi, mid, lo


def _fft_stage1_kernel(x_ref, g_ref, f1k_ref, ar_ref, ai_ref, *, n1, d, nb):
    x = x_ref[...].reshape(n1 * nb, d)
    xn = (x * _rms_scale(x) * g_ref[...]).astype(jnp.bfloat16)
    a = jnp.dot(f1k_ref[...], xn, preferred_element_type=jnp.float32)
    ar_ref[...] = a[:n1 * nb].astype(jnp.bfloat16).reshape(n1, nb, d)
    ai_ref[...] = a[n1 * nb:].astype(jnp.bfloat16).reshape(n1, nb, d)


def _fft_stage2_kernel(ar_ref, ai_ref, gm_ref, cs_ref, wo_ref, gpost_ref, xres_ref, out_ref,
                       *, n2, d, cg, kb):
    cs = cs_ref[...]
    zs = []
    for i in range(kb):
        a = jnp.concatenate([ar_ref[i], ai_ref[i]], axis=0)
        y = jnp.dot(gm_ref[i], a, preferred_element_type=jnp.float32)
        yr = y[:n2].astype(jnp.bfloat16)
        yi = y[n2:].astype(jnp.bfloat16)
        cols = []
        for gi in range(d // cg):
            lhs = jnp.concatenate([yr[:, gi * cg:(gi + 1) * cg], yi[:, gi * cg:(gi + 1) * cg]], axis=1)
            cols.append(jnp.dot(lhs, cs, preferred_element_type=jnp.float32))
        zs.append(jnp.concatenate(cols, axis=1).astype(jnp.bfloat16))
    z = jnp.concatenate(zs, axis=0)
    m = jnp.dot(z, wo_ref[...], preferred_element_type=jnp.float32)
    r = m * _rms_scale(m) * gpost_ref[...]
    out_ref[...] = xres_ref[...] + pltpu.einshape("ikd->kid", r.reshape(kb, n2, d))


def _fourier_layer(x, g_pre, g_post, w_o):
    b, s, d = x.shape
    n2 = FFT_N2
    n1 = s // n2
    cg = d // N_FOURIER_GROUPS
    nb = FFT_S2_PER_STEP
    assert n1 * n2 == s and n2 % nb == 0

    s1 = np.arange(n1)
    th1 = 2.0 * np.pi * ((s1[:, None] * s1[None, :]) % n1) / n1
    f1 = np.concatenate([np.cos(th1), -np.sin(th1)], axis=0)
    s2 = np.arange(n2)
    kk = (np.arange(n1)[:, None, None] + n1 * np.arange(n2)[None, :, None])
    th2 = 2.0 * np.pi * ((kk * s2[None, None, :]) % s) / s
    gr, gi = np.cos(th2), -np.sin(th2)
    gm = np.concatenate([np.concatenate([gr, -gi], axis=2),
                         np.concatenate([gi, gr], axis=2)], axis=1)
    c = np.arange(cg)
    thc = 2.0 * np.pi * ((c[:, None] * c[None, :]) % cg) / cg
    norm = 1.0 / math.sqrt(s * cg)
    cs = np.concatenate([np.cos(thc), np.sin(thc)], axis=0) * norm

    f1k = jnp.asarray(_np_bf16(np.kron(f1, np.eye(nb))))
    gm = jnp.asarray(_np_bf16(gm))
    cs = jnp.asarray(_np_bf16(cs))

    x4 = x.reshape(b, n1, n2, d)
    ar, ai = pl.pallas_call(
        functools.partial(_fft_stage1_kernel, n1=n1, d=d, nb=nb),
        grid=(b, n2 // nb),
        in_specs=[
            pl.BlockSpec((None, n1, nb, d), lambda i, j: (i, 0, j, 0)),
            pl.BlockSpec((1, d), lambda i, j: (0, 0)),
            pl.BlockSpec((2 * n1 * nb, n1 * nb), lambda i, j: (0, 0), pipeline_mode=pl.Buffered(1)),
        ],
        out_specs=[
            pl.BlockSpec((None, n1, nb, d), lambda i, j: (i, 0, j, 0)),
            pl.BlockSpec((None, n1, nb, d), lambda i, j: (i, 0, j, 0)),
        ],
        out_shape=[jax.ShapeDtypeStruct((b, n1, n2, d), jnp.bfloat16)] * 2,
        compiler_params=pltpu.CompilerParams(
            dimension_semantics=("parallel", "parallel"), vmem_limit_bytes=VMEM_LIMIT_BYTES),
        name="fft_stage1",
    )(x4, g_pre.reshape(1, d), f1k)

    kb = min(FFT_K1_PER_STEP, n1)
    assert n1 % kb == 0
    xk = x.reshape(b, n2, n1, d)
    out = pl.pallas_call(
        functools.partial(_fft_stage2_kernel, n2=n2, d=d, cg=cg, kb=kb),
        grid=(b, n1 // kb),
        in_specs=[
            pl.BlockSpec((None, kb, n2, d), lambda i, j: (i, j, 0, 0)),
            pl.BlockSpec((None, kb, n2, d), lambda i, j: (i, j, 0, 0)),
            pl.BlockSpec((kb, 2 * n2, 2 * n2), lambda i, j: (j, 0, 0)),
            pl.BlockSpec((2 * cg, cg), lambda i, j: (0, 0)),
            pl.BlockSpec((d, d), lambda i, j: (0, 0)),
            pl.BlockSpec((1, d), lambda i, j: (0, 0)),
            pl.BlockSpec((None, n2, kb, d), lambda i, j: (i, 0, j, 0)),
        ],
        out_specs=pl.BlockSpec((None, n2, kb, d), lambda i, j: (i, 0, j, 0)),
        out_shape=jax.ShapeDtypeStruct((b, n2, n1, d), jnp.float32),
        compiler_params=pltpu.CompilerParams(
            dimension_semantics=("parallel", "parallel"), vmem_limit_bytes=VMEM_LIMIT_BYTES),
        name="fft_stage2",
    )(ar, ai, gm, cs, w_o.astype(jnp.bfloat16), g_post.reshape(1, d), xk)
    return out.reshape(b, s, d)


def _ffn_kernel(*refs, f, chunk, mixer_proj):
    if mixer_proj:
        a_ref, wo_ref, gmix_ref, x_ref, gpre_ref, gpost_ref, wg_ref, wu_ref, wd_ref, out_ref = refs
        m = jnp.dot(a_ref[...], wo_ref[...], preferred_element_type=jnp.float32)
        x = x_ref[...] + m * _rms_scale(m) * gmix_ref[...]
    else:
        x_ref, gpre_ref, gpost_ref, wg_ref, wu_ref, wd_ref, out_ref = refs
        x = x_ref[...]
    xn = (x * _rms_scale(x) * gpre_ref[...]).astype(jnp.bfloat16)
    acc = jnp.zeros(x.shape, jnp.float32)
    for c0 in range(0, f, chunk):
        c1 = min(c0 + chunk, f)
        gt = jnp.dot(xn, wg_ref[:, c0:c1], preferred_element_type=jnp.float32)
        up = jnp.dot(xn, wu_ref[:, c0:c1], preferred_element_type=jnp.float32)
        hh = (gt * jax.nn.sigmoid(gt) * up).astype(jnp.bfloat16)
        acc = acc + jnp.dot(hh, wd_ref[c0:c1, :], preferred_element_type=jnp.float32)
    out_ref[...] = x + acc * _rms_scale(acc) * gpost_ref[...]


def _ffn_layer(h, g_pre, g_post, w_gate, w_up, w_down, mixer=None):
    b, s, d = h.shape
    f = w_gate.shape[1]
    t = b * s
    tm = min(TOKEN_TILE, t)
    assert t % tm == 0
    const = lambda i: (0, 0)
    rows = pl.BlockSpec((tm, d), lambda i: (i, 0))
    vec = pl.BlockSpec((1, d), const)
    in_specs = [rows, vec, vec,
                pl.BlockSpec((d, f), const, pipeline_mode=pl.Buffered(1)),
                pl.BlockSpec((d, f), const, pipeline_mode=pl.Buffered(1)),
                pl.BlockSpec((f, d), const, pipeline_mode=pl.Buffered(1))]
    args = [h.reshape(t, d), g_pre.reshape(1, d), g_post.reshape(1, d),
            w_gate.astype(jnp.bfloat16), w_up.astype(jnp.bfloat16), w_down.astype(jnp.bfloat16)]
    if mixer is not None:
        a, w_o, g_mix = mixer
        in_specs = [rows, pl.BlockSpec((d, d), const, pipeline_mode=pl.Buffered(1)), vec] + in_specs
        args = [a.reshape(t, d), w_o.astype(jnp.bfloat16), g_mix.reshape(1, d)] + args
    out = pl.pallas_call(
        functools.partial(_ffn_kernel, f=f, chunk=FFN_CHUNK, mixer_proj=mixer is not None),
        grid=(t // tm,),
        in_specs=in_specs,
        out_specs=rows,
        out_shape=jax.ShapeDtypeStruct((t, d), jnp.float32),
        compiler_params=pltpu.CompilerParams(
            dimension_semantics=("parallel",), vmem_limit_bytes=VMEM_LIMIT_BYTES),
        name="swiglu_ffn",
    )(*args)
    return out.reshape(b, s, d)


def _qkv_kernel(x_ref, gpre_ref, w_ref, qa_ref, ka_ref, q_ref, k_ref, v_ref, qn_ref, kn_ref,
                *, d, nh, tm, qscale):
    x = x_ref[...]
    xn = (x * _rms_scale(x) * gpre_ref[...]).astype(jnp.bfloat16)
    dh2 = d // nh
    row = pl.program_id(1) * tm + lax.broadcasted_iota(jnp.int32, (tm, 1), 0)
    pos_lo = (row & (POS_SPLIT - 1)).astype(jnp.float32)
    pos_hi = (row & (-POS_SPLIT)).astype(jnp.float32)
    lane = lax.broadcasted_iota(jnp.int32, (1, dh2), 1)
    half = dh2 // 2
    q_all = jnp.dot(xn, w_ref[:, 0:d], preferred_element_type=jnp.float32) * qscale
    k_all = jnp.dot(xn, w_ref[:, d:2 * d], preferred_element_type=jnp.float32)
    v_all = jnp.dot(xn, w_ref[:, 2 * d:3 * d], preferred_element_type=jnp.float32)
    ones_col = jnp.where(lane == 0, 1.0, 0.0).astype(jnp.bfloat16)
    lane_col = lax.broadcasted_iota(jnp.int32, (dh2, 1), 0)
    comp_sum = jnp.where((lane_col >= half).astype(jnp.int32) == lane, 1.0, 0.0).astype(jnp.bfloat16)
    for h in range(nh):
        qh = q_all[:, h * dh2:(h + 1) * dh2]
        kh = k_all[:, h * dh2:(h + 1) * dh2]
        for src, dst in ((qh, qn_ref), (kh, kn_ref)):
            vb = src.astype(jnp.bfloat16).astype(jnp.float32)
            sq = jnp.dot((vb * vb).astype(jnp.bfloat16), comp_sum, preferred_element_type=jnp.float32)
            dst[h] = jnp.max(sq, axis=0, keepdims=True)
        for comp in range(2):
            data = (lane < half) if comp == 0 else (lane >= half)
            qa = qa_ref[h, comp]
            ka = ka_ref[h, comp]
            q_aug = qa[0:1] + qa[1:2] * pos_lo + qa[2:3] * pos_hi
            k_aug = ka[0:1] + ka[1:2] * pos_lo + ka[2:3] * pos_hi
            q_ref[h, comp] = jnp.where(data, qh, q_aug).astype(jnp.bfloat16)
            k_ref[h, comp] = jnp.where(data, kh, k_aug).astype(jnp.bfloat16)
        vh = v_all[:, h * dh2:(h + 1) * dh2].astype(jnp.bfloat16)
        v_ref[h] = jnp.concatenate([vh, jnp.broadcast_to(ones_col, (tm, dh2))], axis=1)


def _aug_tables(nh, dh2, cvals):
    half = dh2 // 2
    qa = np.zeros((nh, 2, 3, dh2), np.float64)
    ka = np.zeros((nh, 2, 3, dh2), np.float64)
    for h in range(nh):
        pieces = _bf16_split3(cvals[h])
        for comp in range(2):
            base = half if comp == 0 else 0
            for p in range(3):
                for part in (1, 2):
                    qa[h, comp, 0, base + 3 * (part - 1) + p] = pieces[p]
                    ka[h, comp, part, base + 3 * (part - 1) + p] = 1.0
                    qa[h, comp, part, base + 6 + 3 * (part - 1) + p] = -1.0
                    ka[h, comp, 0, base + 6 + 3 * (part - 1) + p] = pieces[p]
    return jnp.asarray(qa, jnp.float32), jnp.asarray(ka, jnp.float32)


def _reach_kernel(qn_ref, kn_ref, invc_ref, reach_ref, *, s):
    kmax = jnp.max(kn_ref[...], axis=1, keepdims=True)
    prod = qn_ref[...] * kmax
    u2 = jnp.maximum(prod[..., 0:1], prod[..., 1:2])
    thr = 2.0 * SKIP_SLACK * jnp.sqrt(u2) + SKIP_EXP2_ZERO
    reach = jnp.minimum(thr * invc_ref[..., 0:1], float(s)).astype(jnp.int32) + 1
    reach_ref[...] = jnp.broadcast_to(reach, reach_ref.shape)


def _attn_kernel(reach_ref, q_ref, k_ref, v_ref, bias_ref, lam_ref, g_ref, o_ref, m_scr, acc_scr,
                 *, s, dv, lam_init):
    tq, ck = ATT_TQ, ATT_CK
    nch = s // ck
    hd = pl.program_id(1)
    qi = pl.program_id(2)
    i0 = qi * tq
    lane = lax.broadcasted_iota(jnp.int32, (1, LANES), 1)
    half = LANES // 2

    q_left, q_right, q_diag = [], [], []
    for comp in range(2):
        data = (lane < half) if comp == 0 else (lane >= half)
        qc = q_ref[comp]
        q_left.append(qc)
        q_right.append(jnp.where(data, qc, -qc))
        q_diag.append(jnp.where(data, qc, jnp.zeros_like(qc)))

    def scores(j0, table_off, nk):
        j0 = pl.multiple_of(j0, POS_SPLIT)
        ts = []
        for comp in range(2):
            kc = k_ref[comp, pl.ds(j0, nk), :]
            if table_off is None:
                qv = jnp.where(j0 < i0, q_left[comp], q_right[comp])
            else:
                qv = q_diag[comp]
            t = lax.dot_general(qv, kc, (((1,), (1,)), ((), ())),
                                preferred_element_type=jnp.float32)
            if table_off is not None:
                t = t + bias_ref[:, pl.ds(pl.multiple_of(table_off, POS_SPLIT), nk)]
            ts.append(t)
        return j0, ts, nk

    def accumulate(j0, ts, nk, first):
        vc = v_ref[pl.ds(j0, nk), :]
        for comp in range(2):
            t = ts[comp]
            m_new = jnp.max(t, axis=-1, keepdims=True)
            if not first:
                m_old = m_scr[comp]
                m_new = jnp.maximum(m_old, m_new)
            p = jnp.exp2(t - m_new).astype(jnp.bfloat16)
            pv = jnp.dot(p, vc, preferred_element_type=jnp.float32)
            if first:
                acc_scr[comp] = pv
            else:
                acc_scr[comp] = jnp.exp2(m_old - m_new) * acc_scr[comp] + pv
            m_scr[comp] = m_new

    reach = reach_ref[(pl.program_id(0) * pl.num_programs(1) + hd) * pl.num_programs(2) + qi]
    lo_key = jnp.maximum(i0 - reach, 0)
    hi_key = jnp.minimum(i0 + tq + reach, s)
    start0 = lo_key & (-POS_SPLIT)
    start0 = jnp.where(((i0 - start0) & (ck - 1)) + tq > ck, start0 - POS_SPLIT, start0)
    width_need = (hi_key - start0 + (ck - 1)) // ck

    def sweep(width):
        def run():
            start = jnp.minimum(start0, s - width * ck)
            rel = i0 - start
            own = rel // ck
            pending = scores(start + own * ck, (ck - tq) - (rel & (ck - 1)), ck)
            for n in range(1, width):
                r = own + n
                nxt = scores(start + jnp.where(r >= width, r - width, r) * ck, None, ck)
                accumulate(*pending, first=(n == 1))
                pending = nxt
            accumulate(*pending, first=(width == 1))
        return run

    def dispatch(width):
        if width == nch:
            sweep(nch)()
        else:
            lax.cond(width_need <= width, sweep(width), lambda: dispatch(width + 1))

    dispatch(1)

    lam_p = lam_ref[...]
    lam = (jnp.exp(jnp.sum(lam_p[0:1] * lam_p[1:2], axis=-1, keepdims=True))
           - jnp.exp(jnp.sum(lam_p[2:3] * lam_p[3:4], axis=-1, keepdims=True)) + lam_init)
    a0 = acc_scr[0]
    a1 = acc_scr[1]
    o = a0[:, :dv] / a0[:, dv:dv + 1] - lam * (a1[:, :dv] / a1[:, dv:dv + 1])
    o = o * _rms_scale(o) * g_ref[...] * (1.0 - lam_init)
    o_ref[...] = o.astype(jnp.bfloat16)


def _diff_attention_mixer(hin, g_pre, w_qkv, lq1, lk1, lq2, lk2, subln_g, layer_idx):
    b, s, d = hin.shape
    nh = N_DIFF_HEADS
    dh2 = d // nh
    dh = dh2 // 2
    dv = dh2
    assert dh2 == LANES and s % ATT_CK == 0
    lam_init = 0.8 - 0.6 * math.exp(-0.3 * layer_idx)
    slopes = np.asarray([2.0 ** (-8.0 * (i + 1) / nh) for i in range(nh)], np.float64)
    cvals = slopes * LOG2E
    qa, ka = _aug_tables(nh, dh2, cvals)
    qscale = dh ** -0.5 * LOG2E

    tm = ATT_TQ
    const2 = lambda i, j: (0, 0)
    nt = s // tm
    norm_spec = pl.BlockSpec((None, nh, None, 1, dh2), lambda i, j: (i, 0, j, 0, 0))
    norm_shape = jax.ShapeDtypeStruct((b, nh, nt, 1, dh2), jnp.float32)
    q, k, v, qn, kn = pl.pallas_call(
        functools.partial(_qkv_kernel, d=d, nh=nh, tm=tm, qscale=qscale),
        grid=(b, s // tm),
        in_specs=[
            pl.BlockSpec((None, tm, d), lambda i, j: (i, j, 0)),
            pl.BlockSpec((1, d), const2),
            pl.BlockSpec((d, 3 * d), const2, pipeline_mode=pl.Buffered(1)),
            pl.BlockSpec((nh, 2, 3, dh2), lambda i, j: (0, 0, 0, 0)),
            pl.BlockSpec((nh, 2, 3, dh2), lambda i, j: (0, 0, 0, 0)),
        ],
        out_specs=[
            pl.BlockSpec((None, nh, 2, tm, dh2), lambda i, j: (i, 0, 0, j, 0)),
            pl.BlockSpec((None, nh, 2, tm, dh2), lambda i, j: (i, 0, 0, j, 0)),
            pl.BlockSpec((None, nh, tm, 2 * dv), lambda i, j: (i, 0, j, 0)),
            norm_spec,
            norm_spec,
        ],
        out_shape=[
            jax.ShapeDtypeStruct((b, nh, 2, s, dh2), jnp.bfloat16),
            jax.ShapeDtypeStruct((b, nh, 2, s, dh2), jnp.bfloat16),
            jax.ShapeDtypeStruct((b, nh, s, 2 * dv), jnp.bfloat16),
            norm_shape,
            norm_shape,
        ],
        compiler_params=pltpu.CompilerParams(
            dimension_semantics=("parallel", "parallel"), vmem_limit_bytes=VMEM_LIMIT_BYTES),
        name="qkv_proj",
    )(hin, g_pre.reshape(1, d), w_qkv.astype(jnp.bfloat16), qa, ka)

    tq, ck = ATT_TQ, ATT_CK
    wdt = ck + (ck - tq)
    dist = np.abs(np.arange(tq)[:, None] - np.arange(wdt)[None, :] + (ck - tq))
    bias = jnp.asarray(-cvals[:, None, None] * dist[None], jnp.float32)
    lam_p = jnp.stack([lq1, lk1, lq2, lk2]).astype(jnp.float32)

    full = pl.BlockSpec((None, nh, nt, 1, dh2), lambda i: (i, 0, 0, 0, 0))
    invc = jnp.broadcast_to(jnp.asarray(1.0 / cvals, jnp.float32)[:, None, None, None], (nh, 1, 1, dh2))
    reach = pl.pallas_call(
        functools.partial(_reach_kernel, s=s),
        grid=(b,),
        in_specs=[full, full, pl.BlockSpec((nh, 1, 1, dh2), lambda i: (0, 0, 0, 0))],
        out_specs=full,
        out_shape=jax.ShapeDtypeStruct((b, nh, nt, 1, dh2), jnp.int32),
        name="skip_reach",
    )(qn, kn, invc)
    reach = reach[..., 0, 0].reshape(-1)

    o = pl.pallas_call(
        functools.partial(_attn_kernel, s=s, dv=dv, lam_init=lam_init),
        grid=(b, nh, s // tq),
        in_specs=[
            pl.BlockSpec(memory_space=pltpu.SMEM),
            pl.BlockSpec((None, None, 2, tq, dh2), lambda i, h, j: (i, h, 0, j, 0)),
            pl.BlockSpec((None, None, 2, s, dh2), lambda i, h, j: (i, h, 0, 0, 0)),
            pl.BlockSpec((None, None, s, 2 * dv), lambda i, h, j: (i, h, 0, 0)),
            pl.BlockSpec((None, tq, wdt), lambda i, h, j: (h, 0, 0)),
            pl.BlockSpec((4, dh), lambda i, h, j: (0, 0)),
            pl.BlockSpec((1, dv), lambda i, h, j: (0, 0)),
        ],
        out_specs=pl.BlockSpec((None, tq, dv), lambda i, h, j: (i, j, h)),
        out_shape=jax.ShapeDtypeStruct((b, s, nh * dv), jnp.bfloat16),
        scratch_shapes=[
            pltpu.VMEM((2, tq, 1), jnp.float32),
            pltpu.VMEM((2, tq, 2 * dv), jnp.float32),
        ],
        compiler_params=pltpu.CompilerParams(
            dimension_semantics=("parallel", "parallel", "arbitrary"),
            vmem_limit_bytes=VMEM_LIMIT_BYTES),
        name="diff_attention",
    )(reach, q, k, v, bias, lam_p, subln_g.reshape(1, dv))
    return o


def kernel(x, norm_mix_pre, norm_mix_post, norm_ffn_pre, norm_ffn_post, fourier_w_o, diff_w_qkv,
           diff_lambda_q1, diff_lambda_k1, diff_lambda_q2, diff_lambda_k2, diff_subln_g, diff_w_o,
           ffn_w_gate, ffn_w_up, ffn_w_down):
    depth = norm_mix_pre.shape[0]
    h = x
    for i in range(depth):
        j = i // N_MIXERS
        mixer = None
        if i % N_MIXERS == 0:
            h = _fourier_layer(h, norm_mix_pre[i], norm_mix_post[i], fourier_w_o[j])
        else:
            a = _diff_attention_mixer(h, norm_mix_pre[i], diff_w_qkv[j], diff_lambda_q1[j],
                                      diff_lambda_k1[j], diff_lambda_q2[j], diff_lambda_k2[j],
                                      diff_subln_g[j], i)
            mixer = (a, diff_w_o[j], norm_mix_post[i])
        h = _ffn_layer(h, norm_ffn_pre[i], norm_ffn_post[i], ffn_w_gate[i], ffn_w_up[i], ffn_w_down[i],
                       mixer=mixer)
    return h
```

```python
import functools
import math

import numpy as np
import jax
import jax.numpy as jnp
from jax import lax
from jax.experimental import pallas as pl
from jax.experimental.pallas import tpu as pltpu

N_FOURIER_GROUPS = 8
N_DIFF_HEADS = 8
RMS_EPS = 1e-6
N_MIXERS = 2

LANES = 128
VMEM_LIMIT_BYTES = 56 * 1024 * 1024

FFT_N2 = 128
FFT_S2_PER_STEP = 16
FFT_K1_PER_STEP = 8
TOKEN_TILE = 512
FFN_CHUNK = 256
ATT_TQ = 512
ATT_CK = 2048
POS_SPLIT = 256
LOG2E = 1.4426950408889634
SKIP_EXP2_ZERO = 153.0
SKIP_SLACK = 1.01


def _rms_scale(x):
    return lax.rsqrt(jnp.mean(x * x, axis=-1, keepdims=True) + RMS_EPS)


def _np_bf16(v):
    return np.asarray(v, np.float32).astype(jnp.bfloat16)


def _bf16_split3(v):
    v = np.asarray(v, np.float64)
    hi = _np_bf16(v).astype(np.float64)
    mid = _np_bf16(v - hi).astype(np.float64)
    lo = _np_bf16(v - hi - mid).astype(np.float64)
    return hi, mid, lo


def _fft_stage1_kernel(x_ref, g_ref, f1k_ref, ar_ref, ai_ref, *, n1, d, nb):
    x = x_ref[...].reshape(n1 * nb, d)
    xn = (x * _rms_scale(x) * g_ref[...]).astype(jnp.bfloat16)
    a = jnp.dot(f1k_ref[...], xn, preferred_element_type=jnp.float32)
    ar_ref[...] = a[:n1 * nb].astype(jnp.bfloat16).reshape(n1, nb, d)
    ai_ref[...] = a[n1 * nb:].astype(jnp.bfloat16).reshape(n1, nb, d)


def _fft_stage2_kernel(ar_ref, ai_ref, gm_ref, cs_ref, wo_ref, gpost_ref, xres_ref, out_ref,
                       *, n2, d, cg, kb):
    cs = cs_ref[...]
    zs = []
    for i in range(kb):
        a = jnp.concatenate([ar_ref[i], ai_ref[i]], axis=0)
        y = jnp.dot(gm_ref[i], a, preferred_element_type=jnp.float32)
        yr = y[:n2].astype(jnp.bfloat16)
        yi = y[n2:].astype(jnp.bfloat16)
        cols = []
        for gi in range(d // cg):
            lhs = jnp.concatenate([yr[:, gi * cg:(gi + 1) * cg], yi[:, gi * cg:(gi + 1) * cg]], axis=1)
            cols.append(jnp.dot(lhs, cs, preferred_element_type=jnp.float32))
        zs.append(jnp.concatenate(cols, axis=1).astype(jnp.bfloat16))
    z = jnp.concatenate(zs, axis=0)
    m = jnp.dot(z, wo_ref[...], preferred_element_type=jnp.float32)
    r = m * _rms_scale(m) * gpost_ref[...]
    out_ref[...] = xres_ref[...] + pltpu.einshape("ikd->kid", r.reshape(kb, n2, d))


def _fourier_layer(x, g_pre, g_post, w_o):
    b, s, d = x.shape
    n2 = FFT_N2
    n1 = s // n2
    cg = d // N_FOURIER_GROUPS
    nb = FFT_S2_PER_STEP
    assert n1 * n2 == s and n2 % nb == 0

    s1 = np.arange(n1)
    th1 = 2.0 * np.pi * ((s1[:, None] * s1[None, :]) % n1) / n1
    f1 = np.concatenate([np.cos(th1), -np.sin(th1)], axis=0)
    s2 = np.arange(n2)
    kk = (np.arange(n1)[:, None, None] + n1 * np.arange(n2)[None, :, None])
    th2 = 2.0 * np.pi * ((kk * s2[None, None, :]) % s) / s
    gr, gi = np.cos(th2), -np.sin(th2)
    gm = np.concatenate([np.concatenate([gr, -gi], axis=2),
                         np.concatenate([gi, gr], axis=2)], axis=1)
    c = np.arange(cg)
    thc = 2.0 * np.pi * ((c[:, None] * c[None, :]) % cg) / cg
    norm = 1.0 / math.sqrt(s * cg)
    cs = np.concatenate([np.cos(thc), np.sin(thc)], axis=0) * norm

    f1k = jnp.asarray(_np_bf16(np.kron(f1, np.eye(nb))))
    gm = jnp.asarray(_np_bf16(gm))
    cs = jnp.asarray(_np_bf16(cs))

    x4 = x.reshape(b, n1, n2, d)
    ar, ai = pl.pallas_call(
        functools.partial(_fft_stage1_kernel, n1=n1, d=d, nb=nb),
        grid=(b, n2 // nb),
        in_specs=[
            pl.BlockSpec((None, n1, nb, d), lambda i, j: (i, 0, j, 0)),
            pl.BlockSpec((1, d), lambda i, j: (0, 0)),
            pl.BlockSpec((2 * n1 * nb, n1 * nb), lambda i, j: (0, 0), pipeline_mode=pl.Buffered(1)),
        ],
        out_specs=[
            pl.BlockSpec((None, n1, nb, d), lambda i, j: (i, 0, j, 0)),
            pl.BlockSpec((None, n1, nb, d), lambda i, j: (i, 0, j, 0)),
        ],
        out_shape=[jax.ShapeDtypeStruct((b, n1, n2, d), jnp.bfloat16)] * 2,
        compiler_params=pltpu.CompilerParams(
            dimension_semantics=("parallel", "parallel"), vmem_limit_bytes=VMEM_LIMIT_BYTES),
        name="fft_stage1",
    )(x4, g_pre.reshape(1, d), f1k)

    kb = min(FFT_K1_PER_STEP, n1)
    assert n1 % kb == 0
    xk = x.reshape(b, n2, n1, d)
    out = pl.pallas_call(
        functools.partial(_fft_stage2_kernel, n2=n2, d=d, cg=cg, kb=kb),
        grid=(b, n1 // kb),
        in_specs=[
            pl.BlockSpec((None, kb, n2, d), lambda i, j: (i, j, 0, 0)),
            pl.BlockSpec((None, kb, n2, d), lambda i, j: (i, j, 0, 0)),
            pl.BlockSpec((kb, 2 * n2, 2 * n2), lambda i, j: (j, 0, 0)),
            pl.BlockSpec((2 * cg, cg), lambda i, j: (0, 0)),
            pl.BlockSpec((d, d), lambda i, j: (0, 0)),
            pl.BlockSpec((1, d), lambda i, j: (0, 0)),
            pl.BlockSpec((None, n2, kb, d), lambda i, j: (i, 0, j, 0)),
        ],
        out_specs=pl.BlockSpec((None, n2, kb, d), lambda i, j: (i, 0, j, 0)),
        out_shape=jax.ShapeDtypeStruct((b, n2, n1, d), jnp.float32),
        compiler_params=pltpu.CompilerParams(
            dimension_semantics=("parallel", "parallel"), vmem_limit_bytes=VMEM_LIMIT_BYTES),
        name="fft_stage2",
    )(ar, ai, gm, cs, w_o.astype(jnp.bfloat16), g_post.reshape(1, d), xk)
    return out.reshape(b, s, d)


def _ffn_kernel(*refs, f, chunk, mixer_proj):
    if mixer_proj:
        a_ref, wo_ref, gmix_ref, x_ref, gpre_ref, gpost_ref, wg_ref, wu_ref, wd_ref, out_ref = refs
        m = jnp.dot(a_ref[...], wo_ref[...], preferred_element_type=jnp.float32)
        x = x_ref[...] + m * _rms_scale(m) * gmix_ref[...]
    else:
        x_ref, gpre_ref, gpost_ref, wg_ref, wu_ref, wd_ref, out_ref = refs
        x = x_ref[...]
    xn = (x * _rms_scale(x) * gpre_ref[...]).astype(jnp.bfloat16)
    acc = jnp.zeros(x.shape, jnp.float32)
    for c0 in range(0, f, chunk):
        c1 = min(c0 + chunk, f)
        gt = jnp.dot(xn, wg_ref[:, c0:c1], preferred_element_type=jnp.float32)
        up = jnp.dot(xn, wu_ref[:, c0:c1], preferred_element_type=jnp.float32)
        hh = (gt * jax.nn.sigmoid(gt) * up).astype(jnp.bfloat16)
        acc = acc + jnp.dot(hh, wd_ref[c0:c1, :], preferred_element_type=jnp.float32)
    out_ref[...] = x + acc * _rms_scale(acc) * gpost_ref[...]


def _ffn_layer(h, g_pre, g_post, w_gate, w_up, w_down, mixer=None):
    b, s, d = h.shape
    f = w_gate.shape[1]
    t = b * s
    tm = min(TOKEN_TILE, t)
    assert t % tm == 0
    const = lambda i: (0, 0)
    rows = pl.BlockSpec((tm, d), lambda i: (i, 0))
    vec = pl.BlockSpec((1, d), const)
    in_specs = [rows, vec, vec,
                pl.BlockSpec((d, f), const, pipeline_mode=pl.Buffered(1)),
                pl.BlockSpec((d, f), const, pipeline_mode=pl.Buffered(1)),
                pl.BlockSpec((f, d), const, pipeline_mode=pl.Buffered(1))]
    args = [h.reshape(t, d), g_pre.reshape(1, d), g_post.reshape(1, d),
            w_gate.astype(jnp.bfloat16), w_up.astype(jnp.bfloat16), w_down.astype(jnp.bfloat16)]
    if mixer is not None:
        a, w_o, g_mix = mixer
        in_specs = [rows, pl.BlockSpec((d, d), const, pipeline_mode=pl.Buffered(1)), vec] + in_specs
        args = [a.reshape(t, d), w_o.astype(jnp.bfloat16), g_mix.reshape(1, d)] + args
    out = pl.pallas_call(
        functools.partial(_ffn_kernel, f=f, chunk=FFN_CHUNK, mixer_proj=mixer is not None),
        grid=(t // tm,),
        in_specs=in_specs,
        out_specs=rows,
        out_shape=jax.ShapeDtypeStruct((t, d), jnp.float32),
        compiler_params=pltpu.CompilerParams(
            dimension_semantics=("parallel",), vmem_limit_bytes=VMEM_LIMIT_BYTES),
        name="swiglu_ffn",
    )(*args)
    return out.reshape(b, s, d)


def _qkv_kernel(x_ref, gpre_ref, w_ref, qa_ref, ka_ref, q_ref, k_ref, v_ref, qn_ref, kn_ref,
                *, d, nh, tm, qscale):
    x = x_ref[...]
    xn = (x * _rms_scale(x) * gpre_ref[...]).astype(jnp.bfloat16)
    dh2 = d // nh
    row = pl.program_id(1) * tm + lax.broadcasted_iota(jnp.int32, (tm, 1), 0)
    pos_lo = (row & (POS_SPLIT - 1)).astype(jnp.float32)
    pos_hi = (row & (-POS_SPLIT)).astype(jnp.float32)
    lane = lax.broadcasted_iota(jnp.int32, (1, dh2), 1)
    half = dh2 // 2
    q_all = jnp.dot(xn, w_ref[:, 0:d], preferred_element_type=jnp.float32) * qscale
    k_all = jnp.dot(xn, w_ref[:, d:2 * d], preferred_element_type=jnp.float32)
    v_all = jnp.dot(xn, w_ref[:, 2 * d:3 * d], preferred_element_type=jnp.float32)
    ones_col = jnp.where(lane == 0, 1.0, 0.0).astype(jnp.bfloat16)
    lane_col = lax.broadcasted_iota(jnp.int32, (dh2, 1), 0)
    comp_sum = jnp.where((lane_col >= half).astype(jnp.int32) == lane, 1.0, 0.0).astype(jnp.bfloat16)
    for h in range(nh):
        qh = q_all[:, h * dh2:(h + 1) * dh2]
        kh = k_all[:, h * dh2:(h + 1) * dh2]
        for src, dst in ((qh, qn_ref), (kh, kn_ref)):
            vb = src.astype(jnp.bfloat16).astype(jnp.float32)
            sq = jnp.dot((vb * vb).astype(jnp.bfloat16), comp_sum, preferred_element_type=jnp.float32)
            dst[h] = jnp.max(sq, axis=0, keepdims=True)
        for comp in range(2):
            data = (lane < half) if comp == 0 else (lane >= half)
            qa = qa_ref[h, comp]
            ka = ka_ref[h, comp]
            q_aug = qa[0:1] + qa[1:2] * pos_lo + qa[2:3] * pos_hi
            k_aug = ka[0:1] + ka[1:2] * pos_lo + ka[2:3] * pos_hi
            q_ref[h, comp] = jnp.where(data, qh, q_aug).astype(jnp.bfloat16)
            k_ref[h, comp] = jnp.where(data, kh, k_aug).astype(jnp.bfloat16)
        vh = v_all[:, h * dh2:(h + 1) * dh2].astype(jnp.bfloat16)
        v_ref[h] = jnp.concatenate([vh, jnp.broadcast_to(ones_col, (tm, dh2))], axis=1)


def _aug_tables(nh, dh2, cvals):
    half = dh2 // 2
    qa = np.zeros((nh, 2, 3, dh2), np.float64)
    ka = np.zeros((nh, 2, 3, dh2), np.float64)
    for h in range(nh):
        pieces = _bf16_split3(cvals[h])
        for comp in range(2):
            base = half if comp == 0 else 0
            for p in range(3):
                for part in (1, 2):
                    qa[h, comp, 0, base + 3 * (part - 1) + p] = pieces[p]
                    ka[h, comp, part, base + 3 * (part - 1) + p] = 1.0
                    qa[h, comp, part, base + 6 + 3 * (part - 1) + p] = -1.0
                    ka[h, comp, 0, base + 6 + 3 * (part - 1) + p] = pieces[p]
    return jnp.asarray(qa, jnp.float32), jnp.asarray(ka, jnp.float32)


def _reach_kernel(qn_ref, kn_ref, invc_ref, reach_ref, *, s):
    kmax = jnp.max(kn_ref[...], axis=1, keepdims=True)
    prod = qn_ref[...] * kmax
    u2 = jnp.maximum(prod[..., 0:1], prod[..., 1:2])
    thr = 2.0 * SKIP_SLACK * jnp.sqrt(u2) + SKIP_EXP2_ZERO
    reach = jnp.minimum(thr * invc_ref[..., 0:1], float(s)).astype(jnp.int32) + 1
    reach_ref[...] = jnp.broadcast_to(reach, reach_ref.shape)


def _attn_kernel(reach_ref, q_ref, k_ref, v_ref, bias_ref, lam_ref, g_ref, o_ref, m_scr, acc_scr,
                 *, s, dv, lam_init):
    tq, ck = ATT_TQ, ATT_CK
    nch = s // ck
    hd = pl.program_id(1)
    qi = pl.program_id(2)
    i0 = qi * tq
    lane = lax.broadcasted_iota(jnp.int32, (1, LANES), 1)
    half = LANES // 2

    q_left, q_right, q_diag = [], [], []
    for comp in range(2):
        data = (lane < half) if comp == 0 else (lane >= half)
        qc = q_ref[comp]
        q_left.append(qc)
        q_right.append(jnp.where(data, qc, -qc))
        q_diag.append(jnp.where(data, qc, jnp.zeros_like(qc)))

    def scores(j0, table_off, nk):
        j0 = pl.multiple_of(j0, POS_SPLIT)
        ts = []
        for comp in range(2):
            kc = k_ref[comp, pl.ds(j0, nk), :]
            if table_off is None:
                qv = jnp.where(j0 < i0, q_left[comp], q_right[comp])
            else:
                qv = q_diag[comp]
            t = lax.dot_general(qv, kc, (((1,), (1,)), ((), ())),
                                preferred_element_type=jnp.float32)
            if table_off is not None:
                t = t + bias_ref[:, pl.ds(pl.multiple_of(table_off, POS_SPLIT), nk)]
            ts.append(t)
        return j0, ts, nk

    def accumulate(j0, ts, nk, first):
        vc = v_ref[pl.ds(j0, nk), :]
        for comp in range(2):
            t = ts[comp]
            m_new = jnp.max(t, axis=-1, keepdims=True)
            if not first:
                m_old = m_scr[comp]
                m_new = jnp.maximum(m_old, m_new)
            p = jnp.exp2(t - m_new).astype(jnp.bfloat16)
            pv = jnp.dot(p, vc, preferred_element_type=jnp.float32)
            if first:
                acc_scr[comp] = pv
            else:
                acc_scr[comp] = jnp.exp2(m_old - m_new) * acc_scr[comp] + pv
            m_scr[comp] = m_new

    hk = ck // 2
    reach = reach_ref[(pl.program_id(0) * pl.num_programs(1) + hd) * pl.num_programs(2) + qi]
    lo_key = jnp.maximum(i0 - reach, 0)
    hi_key = jnp.minimum(i0 + tq + reach, s)
    start0 = lo_key & (-POS_SPLIT)
    start0 = jnp.where(((i0 - start0) & (ck - 1)) + tq > ck, start0 - POS_SPLIT, start0)
    halves_need = (hi_key - start0 + (hk - 1)) // hk

    def window_start(halves):
        return jnp.minimum(start0, s - halves * hk)

    def sweep(halves):
        whole, tail = halves // 2, halves % 2

        def run():
            start = window_start(halves)
            rel = i0 - start
            own = rel // ck
            pending = scores(start + own * ck, (ck - tq) - (rel & (ck - 1)), ck)
            for n in range(1, whole):
                r = own + n
                nxt = scores(start + jnp.where(r >= whole, r - whole, r) * ck, None, ck)
                accumulate(*pending, first=(n == 1))
                pending = nxt
            if tail:
                nxt = scores(start + whole * ck, None, hk)
                accumulate(*pending, first=(whole == 1))
                pending = nxt
            accumulate(*pending, first=(whole == 1 and not tail))
        return run

    def dispatch(halves):
        if halves == 2 * nch:
            sweep(halves)()
            return
        ok = halves_need <= halves
        if halves % 2:
            ok = ok & (i0 - window_start(halves) + tq <= (halves // 2) * ck)
        lax.cond(ok, sweep(halves), lambda: dispatch(halves + 1))

    dispatch(2)

    lam_p = lam_ref[...]
    lam = (jnp.exp(jnp.sum(lam_p[0:1] * lam_p[1:2], axis=-1, keepdims=True))
           - jnp.exp(jnp.sum(lam_p[2:3] * lam_p[3:4], axis=-1, keepdims=True)) + lam_init)
    a0 = acc_scr[0]
    a1 = acc_scr[1]
    o = a0[:, :dv] / a0[:, dv:dv + 1] - lam * (a1[:, :dv] / a1[:, dv:dv + 1])
    o = o * _rms_scale(o) * g_ref[...] * (1.0 - lam_init)
    o_ref[...] = o.astype(jnp.bfloat16)


def _diff_attention_mixer(hin, g_pre, w_qkv, lq1, lk1, lq2, lk2, subln_g, layer_idx):
    b, s, d = hin.shape
    nh = N_DIFF_HEADS
    dh2 = d // nh
    dh = dh2 // 2
    dv = dh2
    assert dh2 == LANES and s % ATT_CK == 0
    assert s <= POS_SPLIT * 256
    lam_init = 0.8 - 0.6 * math.exp(-0.3 * layer_idx)
    slopes = np.asarray([2.0 ** (-8.0 * (i + 1) / nh) for i in range(nh)], np.float64)
    cvals = slopes * LOG2E
    qa, ka = _aug_tables(nh, dh2, cvals)
    qscale = dh ** -0.5 * LOG2E

    tm = ATT_TQ
    const2 = lambda i, j: (0, 0)
    nt = s // tm
    norm_spec = pl.BlockSpec((None, nh, None, 1, dh2), lambda i, j: (i, 0, j, 0, 0))
    norm_shape = jax.ShapeDtypeStruct((b, nh, nt, 1, dh2), jnp.float32)
    q, k, v, qn, kn = pl.pallas_call(
        functools.partial(_qkv_kernel, d=d, nh=nh, tm=tm, qscale=qscale),
        grid=(b, s // tm),
        in_specs=[
            pl.BlockSpec((None, tm, d), lambda i, j: (i, j, 0)),
            pl.BlockSpec((1, d), const2),
            pl.BlockSpec((d, 3 * d), const2, pipeline_mode=pl.Buffered(1)),
            pl.BlockSpec((nh, 2, 3, dh2), lambda i, j: (0, 0, 0, 0)),
            pl.BlockSpec((nh, 2, 3, dh2), lambda i, j: (0, 0, 0, 0)),
        ],
        out_specs=[
            pl.BlockSpec((None, nh, 2, tm, dh2), lambda i, j: (i, 0, 0, j, 0)),
            pl.BlockSpec((None, nh, 2, tm, dh2), lambda i, j: (i, 0, 0, j, 0)),
            pl.BlockSpec((None, nh, tm, 2 * dv), lambda i, j: (i, 0, j, 0)),
            norm_spec,
            norm_spec,
        ],
        out_shape=[
            jax.ShapeDtypeStruct((b, nh, 2, s, dh2), jnp.bfloat16),
            jax.ShapeDtypeStruct((b, nh, 2, s, dh2), jnp.bfloat16),
            jax.ShapeDtypeStruct((b, nh, s, 2 * dv), jnp.bfloat16),
            norm_shape,
            norm_shape,
        ],
        compiler_params=pltpu.CompilerParams(
            dimension_semantics=("parallel", "parallel"), vmem_limit_bytes=VMEM_LIMIT_BYTES),
        name="qkv_proj",
    )(hin, g_pre.reshape(1, d), w_qkv.astype(jnp.bfloat16), qa, ka)

    tq, ck = ATT_TQ, ATT_CK
    wdt = ck + (ck - tq)
    dist = np.abs(np.arange(tq)[:, None] - np.arange(wdt)[None, :] + (ck - tq))
    bias = jnp.asarray(-cvals[:, None, None] * dist[None], jnp.float32)
    lam_p = jnp.stack([lq1, lk1, lq2, lk2]).astype(jnp.float32)

    full = pl.BlockSpec((None, nh, nt, 1, dh2), lambda i: (i, 0, 0, 0, 0))
    invc = jnp.broadcast_to(jnp.asarray(1.0 / cvals, jnp.float32)[:, None, None, None], (nh, 1, 1, dh2))
    reach = pl.pallas_call(
        functools.partial(_reach_kernel, s=s),
        grid=(b,),
        in_specs=[full, full, pl.BlockSpec((nh, 1, 1, dh2), lambda i: (0, 0, 0, 0))],
        out_specs=full,
        out_shape=jax.ShapeDtypeStruct((b, nh, nt, 1, dh2), jnp.int32),
        name="skip_reach",
    )(qn, kn, invc)
    reach = reach[..., 0, 0].reshape(-1)

    o = pl.pallas_call(
        functools.partial(_attn_kernel, s=s, dv=dv, lam_init=lam_init),
        grid=(b, nh, s // tq),
        in_specs=[
            pl.BlockSpec(memory_space=pltpu.SMEM),
            pl.BlockSpec((None, None, 2, tq, dh2), lambda i, h, j: (i, h, 0, j, 0)),
            pl.BlockSpec((None, None, 2, s, dh2), lambda i, h, j: (i, h, 0, 0, 0)),
            pl.BlockSpec((None, None, s, 2 * dv), lambda i, h, j: (i, h, 0, 0)),
            pl.BlockSpec((None, tq, wdt), lambda i, h, j: (h, 0, 0)),
            pl.BlockSpec((4, dh), lambda i, h, j: (0, 0)),
            pl.BlockSpec((1, dv), lambda i, h, j: (0, 0)),
        ],
        out_specs=pl.BlockSpec((None, tq, dv), lambda i, h, j: (i, j, h)),
        out_shape=jax.ShapeDtypeStruct((b, s, nh * dv), jnp.bfloat16),
        scratch_shapes=[
            pltpu.VMEM((2, tq, 1), jnp.float32),
            pltpu.VMEM((2, tq, 2 * dv), jnp.float32),
        ],
        compiler_params=pltpu.CompilerParams(
            dimension_semantics=("parallel", "parallel", "arbitrary"),
            vmem_limit_bytes=VMEM_LIMIT_BYTES),
        name="diff_attention",
    )(reach, q, k, v, bias, lam_p, subln_g.reshape(1, dv))
    return o


def kernel(x, norm_mix_pre, norm_mix_post, norm_ffn_pre, norm_ffn_post, fourier_w_o, diff_w_qkv,
           diff_lambda_q1, diff_lambda_k1, diff_lambda_q2, diff_lambda_k2, diff_subln_g, diff_w_o,
           ffn_w_gate, ffn_w_up, ffn_w_down):
    depth = norm_mix_pre.shape[0]
    h = x
    for i in range(depth):
        j = i // N_MIXERS
        mixer = None
        if i % N_MIXERS == 0:
            h = _fourier_layer(h, norm_mix_pre[i], norm_mix_post[i], fourier_w_o[j])
        else:
            a = _diff_attention_mixer(h, norm_mix_pre[i], diff_w_qkv[j], diff_lambda_q1[j],
                                      diff_lambda_k1[j], diff_lambda_q2[j], diff_lambda_k2[j],
                                      diff_subln_g[j], i)
            mixer = (a, diff_w_o[j], norm_mix_post[i])
        h = _ffn_layer(h, norm_ffn_pre[i], norm_ffn_post[i], ffn_w_gate[i], ffn_w_up[i], ffn_w_down[i],
                       mixer=mixer)
    return h
```

```python
import functools
import math

import numpy as np
import jax
import jax.numpy as jnp
from jax import lax
from jax.experimental import pallas as pl
from jax.experimental.pallas import tpu as pltpu

N_FOURIER_GROUPS = 8
N_DIFF_HEADS = 8
RMS_EPS = 1e-6
N_MIXERS = 2

LANES = 128
VMEM_LIMIT_BYTES = 56 * 1024 * 1024

FFT_N2 = 128
FFT_S2_PER_STEP = 16
FFT_K1_PER_STEP = 8
TOKEN_TILE = 512
FFN_CHUNK = 256
ATT_TQ = 512
ATT_CK = 2048
ATT_WINDOW_HALVES = (2, 3, 5)
POS_SPLIT = 256
LOG2E = 1.4426950408889634
SKIP_EXP2_ZERO = 153.0
SKIP_SLACK = 1.01


def _rms_scale(x):
    return lax.rsqrt(jnp.mean(x * x, axis=-1, keepdims=True) + RMS_EPS)


def _np_bf16(v):
    return np.asarray(v, np.float32).astype(jnp.bfloat16)


def _bf16_split3(v):
    v = np.asarray(v, np.float64)
    hi = _np_bf16(v).astype(np.float64)
    mid = _np_bf16(v - hi).astype(np.float64)
    lo = _np_bf16(v - hi - mid).astype(np.float64)
    return hi, mid, lo


def _fft_stage1_kernel(x_ref, g_ref, f1k_ref, ar_ref, ai_ref, *, n1, d, nb):
    x = x_ref[...].reshape(n1 * nb, d)
    xn = (x * _rms_scale(x) * g_ref[...]).astype(jnp.bfloat16)
    a = jnp.dot(f1k_ref[...], xn, preferred_element_type=jnp.float32)
    ar_ref[...] = a[:n1 * nb].astype(jnp.bfloat16).reshape(n1, nb, d)
    ai_ref[...] = a[n1 * nb:].astype(jnp.bfloat16).reshape(n1, nb, d)


def _fft_stage2_kernel(ar_ref, ai_ref, gm_ref, cs_ref, wo_ref, gpost_ref, xres_ref, out_ref,
                       *, n2, d, cg, kb):
    cs = cs_ref[...]
    zs = []
    for i in range(kb):
        a = jnp.concatenate([ar_ref[i], ai_ref[i]], axis=0)
        y = jnp.dot(gm_ref[i], a, preferred_element_type=jnp.float32)
        yr = y[:n2].astype(jnp.bfloat16)
        yi = y[n2:].astype(jnp.bfloat16)
        cols = []
        for gi in range(d // cg):
            lhs = jnp.concatenate([yr[:, gi * cg:(gi + 1) * cg], yi[:, gi * cg:(gi + 1) * cg]], axis=1)
            cols.append(jnp.dot(lhs, cs, preferred_element_type=jnp.float32))
        zs.append(jnp.concatenate(cols, axis=1).astype(jnp.bfloat16))
    z = jnp.concatenate(zs, axis=0)
    m = jnp.dot(z, wo_ref[...], preferred_element_type=jnp.float32)
    r = m * _rms_scale(m) * gpost_ref[...]
    out_ref[...] = xres_ref[...] + pltpu.einshape("ikd->kid", r.reshape(kb, n2, d))


def _fourier_layer(x, g_pre, g_post, w_o):
    b, s, d = x.shape
    n2 = FFT_N2
    n1 = s // n2
    cg = d // N_FOURIER_GROUPS
    nb = FFT_S2_PER_STEP
    assert n1 * n2 == s and n2 % nb == 0

    s1 = np.arange(n1)
    th1 = 2.0 * np.pi * ((s1[:, None] * s1[None, :]) % n1) / n1
    f1 = np.concatenate([np.cos(th1), -np.sin(th1)], axis=0)
    s2 = np.arange(n2)
    kk = (np.arange(n1)[:, None, None] + n1 * np.arange(n2)[None, :, None])
    th2 = 2.0 * np.pi * ((kk * s2[None, None, :]) % s) / s
    gr, gi = np.cos(th2), -np.sin(th2)
    gm = np.concatenate([np.concatenate([gr, -gi], axis=2),
                         np.concatenate([gi, gr], axis=2)], axis=1)
    c = np.arange(cg)
    thc = 2.0 * np.pi * ((c[:, None] * c[None, :]) % cg) / cg
    norm = 1.0 / math.sqrt(s * cg)
    cs = np.concatenate([np.cos(thc), np.sin(thc)], axis=0) * norm

    f1k = jnp.asarray(_np_bf16(np.kron(f1, np.eye(nb))))
    gm = jnp.asarray(_np_bf16(gm))
    cs = jnp.asarray(_np_bf16(cs))

    x4 = x.reshape(b, n1, n2, d)
    ar, ai = pl.pallas_call(
        functools.partial(_fft_stage1_kernel, n1=n1, d=d, nb=nb),
        grid=(b, n2 // nb),
        in_specs=[
            pl.BlockSpec((None, n1, nb, d), lambda i, j: (i, 0, j, 0)),
            pl.BlockSpec((1, d), lambda i, j: (0, 0)),
            pl.BlockSpec((2 * n1 * nb, n1 * nb), lambda i, j: (0, 0), pipeline_mode=pl.Buffered(1)),
        ],
        out_specs=[
            pl.BlockSpec((None, n1, nb, d), lambda i, j: (i, 0, j, 0)),
            pl.BlockSpec((None, n1, nb, d), lambda i, j: (i, 0, j, 0)),
        ],
        out_shape=[jax.ShapeDtypeStruct((b, n1, n2, d), jnp.bfloat16)] * 2,
        compiler_params=pltpu.CompilerParams(
            dimension_semantics=("parallel", "parallel"), vmem_limit_bytes=VMEM_LIMIT_BYTES),
        name="fft_stage1",
    )(x4, g_pre.reshape(1, d), f1k)

    kb = min(FFT_K1_PER_STEP, n1)
    assert n1 % kb == 0
    xk = x.reshape(b, n2, n1, d)
    out = pl.pallas_call(
        functools.partial(_fft_stage2_kernel, n2=n2, d=d, cg=cg, kb=kb),
        grid=(b, n1 // kb),
        in_specs=[
            pl.BlockSpec((None, kb, n2, d), lambda i, j: (i, j, 0, 0)),
            pl.BlockSpec((None, kb, n2, d), lambda i, j: (i, j, 0, 0)),
            pl.BlockSpec((kb, 2 * n2, 2 * n2), lambda i, j: (j, 0, 0)),
            pl.BlockSpec((2 * cg, cg), lambda i, j: (0, 0)),
            pl.BlockSpec((d, d), lambda i, j: (0, 0)),
            pl.BlockSpec((1, d), lambda i, j: (0, 0)),
            pl.BlockSpec((None, n2, kb, d), lambda i, j: (i, 0, j, 0)),
        ],
        out_specs=pl.BlockSpec((None, n2, kb, d), lambda i, j: (i, 0, j, 0)),
        out_shape=jax.ShapeDtypeStruct((b, n2, n1, d), jnp.float32),
        compiler_params=pltpu.CompilerParams(
            dimension_semantics=("parallel", "parallel"), vmem_limit_bytes=VMEM_LIMIT_BYTES),
        name="fft_stage2",
    )(ar, ai, gm, cs, w_o.astype(jnp.bfloat16), g_post.reshape(1, d), xk)
    return out.reshape(b, s, d)


def _ffn_kernel(*refs, f, chunk, mixer_proj):
    if mixer_proj:
        a_ref, wo_ref, gmix_ref, x_ref, gpre_ref, gpost_ref, wg_ref, wu_ref, wd_ref, out_ref = refs
        m = jnp.dot(a_ref[...], wo_ref[...], preferred_element_type=jnp.float32)
        x = x_ref[...] + m * _rms_scale(m) * gmix_ref[...]
    else:
        x_ref, gpre_ref, gpost_ref, wg_ref, wu_ref, wd_ref, out_ref = refs
        x = x_ref[...]
    xn = (x * _rms_scale(x) * gpre_ref[...]).astype(jnp.bfloat16)
    acc = jnp.zeros(x.shape, jnp.float32)
    for c0 in range(0, f, chunk):
        c1 = min(c0 + chunk, f)
        gt = jnp.dot(xn, wg_ref[:, c0:c1], preferred_element_type=jnp.float32)
        up = jnp.dot(xn, wu_ref[:, c0:c1], preferred_element_type=jnp.float32)
        hh = (gt * jax.nn.sigmoid(gt) * up).astype(jnp.bfloat16)
        acc = acc + jnp.dot(hh, wd_ref[c0:c1, :], preferred_element_type=jnp.float32)
    out_ref[...] = x + acc * _rms_scale(acc) * gpost_ref[...]


def _ffn_layer(h, g_pre, g_post, w_gate, w_up, w_down, mixer=None):
    b, s, d = h.shape
    f = w_gate.shape[1]
    t = b * s
    tm = min(TOKEN_TILE, t)
    assert t % tm == 0
    const = lambda i: (0, 0)
    rows = pl.BlockSpec((tm, d), lambda i: (i, 0))
    vec = pl.BlockSpec((1, d), const)
    in_specs = [rows, vec, vec,
                pl.BlockSpec((d, f), const, pipeline_mode=pl.Buffered(1)),
                pl.BlockSpec((d, f), const, pipeline_mode=pl.Buffered(1)),
                pl.BlockSpec((f, d), const, pipeline_mode=pl.Buffered(1))]
    args = [h.reshape(t, d), g_pre.reshape(1, d), g_post.reshape(1, d),
            w_gate.astype(jnp.bfloat16), w_up.astype(jnp.bfloat16), w_down.astype(jnp.bfloat16)]
    if mixer is not None:
        a, w_o, g_mix = mixer
        in_specs = [rows, pl.BlockSpec((d, d), const, pipeline_mode=pl.Buffered(1)), vec] + in_specs
        args = [a.reshape(t, d), w_o.astype(jnp.bfloat16), g_mix.reshape(1, d)] + args
    out = pl.pallas_call(
        functools.partial(_ffn_kernel, f=f, chunk=FFN_CHUNK, mixer_proj=mixer is not None),
        grid=(t // tm,),
        in_specs=in_specs,
        out_specs=rows,
        out_shape=jax.ShapeDtypeStruct((t, d), jnp.float32),
        compiler_params=pltpu.CompilerParams(
            dimension_semantics=("parallel",), vmem_limit_bytes=VMEM_LIMIT_BYTES),
        name="swiglu_ffn",
    )(*args)
    return out.reshape(b, s, d)


def _qkv_kernel(x_ref, gpre_ref, w_ref, qa_ref, ka_ref, q_ref, k_ref, v_ref, qn_ref, kn_ref,
                *, d, nh, tm, qscale):
    x = x_ref[...]
    xn = (x * _rms_scale(x) * gpre_ref[...]).astype(jnp.bfloat16)
    dh2 = d // nh
    row = pl.program_id(1) * tm + lax.broadcasted_iota(jnp.int32, (tm, 1), 0)
    pos_lo = (row & (POS_SPLIT - 1)).astype(jnp.float32)
    pos_hi = (row & (-POS_SPLIT)).astype(jnp.float32)
    lane = lax.broadcasted_iota(jnp.int32, (1, dh2), 1)
    half = dh2 // 2
    q_all = jnp.dot(xn, w_ref[:, 0:d], preferred_element_type=jnp.float32) * qscale
    k_all = jnp.dot(xn, w_ref[:, d:2 * d], preferred_element_type=jnp.float32)
    v_all = jnp.dot(xn, w_ref[:, 2 * d:3 * d], preferred_element_type=jnp.float32)
    ones_col = jnp.where(lane == 0, 1.0, 0.0).astype(jnp.bfloat16)
    lane_col = lax.broadcasted_iota(jnp.int32, (dh2, 1), 0)
    comp_sum = jnp.where((lane_col >= half).astype(jnp.int32) == lane, 1.0, 0.0).astype(jnp.bfloat16)
    for h in range(nh):
        qh = q_all[:, h * dh2:(h + 1) * dh2]
        kh = k_all[:, h * dh2:(h + 1) * dh2]
        for src, dst in ((qh, qn_ref), (kh, kn_ref)):
            vb = src.astype(jnp.bfloat16).astype(jnp.float32)
            sq = jnp.dot((vb * vb).astype(jnp.bfloat16), comp_sum, preferred_element_type=jnp.float32)
            dst[h] = jnp.max(sq, axis=0, keepdims=True)
        for comp in range(2):
            data = (lane < half) if comp == 0 else (lane >= half)
            qa = qa_ref[h, comp]
            ka = ka_ref[h, comp]
            q_aug = qa[0:1] + qa[1:2] * pos_lo + qa[2:3] * pos_hi
            k_aug = ka[0:1] + ka[1:2] * pos_lo + ka[2:3] * pos_hi
            q_ref[h, comp] = jnp.where(data, qh, q_aug).astype(jnp.bfloat16)
            k_ref[h, comp] = jnp.where(data, kh, k_aug).astype(jnp.bfloat16)
        vh = v_all[:, h * dh2:(h + 1) * dh2].astype(jnp.bfloat16)
        v_ref[h] = jnp.concatenate([vh, jnp.broadcast_to(ones_col, (tm, dh2))], axis=1)


def _aug_tables(nh, dh2, cvals):
    half = dh2 // 2
    qa = np.zeros((nh, 2, 3, dh2), np.float64)
    ka = np.zeros((nh, 2, 3, dh2), np.float64)
    for h in range(nh):
        pieces = _bf16_split3(cvals[h])
        for comp in range(2):
            base = half if comp == 0 else 0
            for p in range(3):
                for part in (1, 2):
                    qa[h, comp, 0, base + 3 * (part - 1) + p] = pieces[p]
                    ka[h, comp, part, base + 3 * (part - 1) + p] = 1.0
                    qa[h, comp, part, base + 6 + 3 * (part - 1) + p] = -1.0
                    ka[h, comp, 0, base + 6 + 3 * (part - 1) + p] = pieces[p]
    return jnp.asarray(qa, jnp.float32), jnp.asarray(ka, jnp.float32)


def _reach_kernel(qn_ref, kn_ref, invc_ref, reach_ref, *, s):
    kmax = jnp.max(kn_ref[...], axis=1, keepdims=True)
    prod = qn_ref[...] * kmax
    u2 = jnp.maximum(prod[..., 0:1], prod[..., 1:2])
    thr = 2.0 * SKIP_SLACK * jnp.sqrt(u2) + SKIP_EXP2_ZERO
    reach = jnp.minimum(thr * invc_ref[..., 0:1], float(s)).astype(jnp.int32) + 1
    reach_ref[...] = jnp.broadcast_to(reach, reach_ref.shape)


def _attn_kernel(reach_ref, q_ref, k_ref, v_ref, bias_ref, lam_ref, g_ref, o_ref, m_scr, acc_scr,
                 *, s, dv, lam_init):
    tq, ck = ATT_TQ, ATT_CK
    nch = s // ck
    hd = pl.program_id(1)
    qi = pl.program_id(2)
    i0 = qi * tq
    lane = lax.broadcasted_iota(jnp.int32, (1, LANES), 1)
    half = LANES // 2

    q_left, q_right, q_diag = [], [], []
    for comp in range(2):
        data = (lane < half) if comp == 0 else (lane >= half)
        qc = q_ref[comp]
        q_left.append(qc)
        q_right.append(jnp.where(data, qc, -qc))
        q_diag.append(jnp.where(data, qc, jnp.zeros_like(qc)))

    def scores(j0, table_off, nk):
        j0 = pl.multiple_of(j0, POS_SPLIT)
        ts = []
        for comp in range(2):
            kc = k_ref[comp, pl.ds(j0, nk), :]
            if table_off is None:
                qv = jnp.where(j0 < i0, q_left[comp], q_right[comp])
            else:
                qv = q_diag[comp]
            t = lax.dot_general(qv, kc, (((1,), (1,)), ((), ())),
                                preferred_element_type=jnp.float32)
            if table_off is not None:
                t = t + bias_ref[:, pl.ds(pl.multiple_of(table_off, POS_SPLIT), nk)]
            ts.append(t)
        return j0, ts, nk

    def accumulate(j0, ts, nk, first):
        vc = v_ref[pl.ds(j0, nk), :]
        for comp in range(2):
            t = ts[comp]
            m_new = jnp.max(t, axis=-1, keepdims=True)
            if not first:
                m_old = m_scr[comp]
                m_new = jnp.maximum(m_old, m_new)
            p = jnp.exp2(t - m_new).astype(jnp.bfloat16)
            pv = jnp.dot(p, vc, preferred_element_type=jnp.float32)
            if first:
                acc_scr[comp] = pv
            else:
                acc_scr[comp] = jnp.exp2(m_old - m_new) * acc_scr[comp] + pv
            m_scr[comp] = m_new

    hk = ck // 2
    reach = reach_ref[(pl.program_id(0) * pl.num_programs(1) + hd) * pl.num_programs(2) + qi]
    lo_key = jnp.maximum(i0 - reach, 0)
    hi_key = jnp.minimum(i0 + tq + reach, s)
    start0 = lo_key & (-POS_SPLIT)
    start0 = jnp.where(((i0 - start0) & (ck - 1)) + tq > ck, start0 - POS_SPLIT, start0)
    halves_need = (hi_key - start0 + (hk - 1)) // hk

    def window_start(halves):
        return jnp.minimum(start0, s - halves * hk)

    def sweep(halves):
        whole, tail = halves // 2, halves % 2

        def run():
            start = window_start(halves)
            rel = i0 - start
            own = rel // ck
            pending = scores(start + own * ck, (ck - tq) - (rel & (ck - 1)), ck)
            for n in range(1, whole):
                r = own + n
                nxt = scores(start + jnp.where(r >= whole, r - whole, r) * ck, None, ck)
                accumulate(*pending, first=(n == 1))
                pending = nxt
            if tail:
                nxt = scores(start + whole * ck, None, hk)
                accumulate(*pending, first=(whole == 1))
                pending = nxt
            accumulate(*pending, first=(whole == 1 and not tail))
        return run

    sizes = [h for h in ATT_WINDOW_HALVES if h < 2 * nch] + [2 * nch]

    def dispatch(idx):
        halves = sizes[idx]
        if idx == len(sizes) - 1:
            sweep(halves)()
            return
        ok = halves_need <= halves
        if halves % 2:
            ok = ok & (i0 - window_start(halves) + tq <= (halves // 2) * ck)
        lax.cond(ok, sweep(halves), lambda: dispatch(idx + 1))

    dispatch(0)

    lam_p = lam_ref[...]
    lam = (jnp.exp(jnp.sum(lam_p[0:1] * lam_p[1:2], axis=-1, keepdims=True))
           - jnp.exp(jnp.sum(lam_p[2:3] * lam_p[3:4], axis=-1, keepdims=True)) + lam_init)
    a0 = acc_scr[0]
    a1 = acc_scr[1]
    o = a0[:, :dv] / a0[:, dv:dv + 1] - lam * (a1[:, :dv] / a1[:, dv:dv + 1])
    o = o * _rms_scale(o) * g_ref[...] * (1.0 - lam_init)
    o_ref[...] = o.astype(jnp.bfloat16)


def _diff_attention_mixer(hin, g_pre, w_qkv, lq1, lk1, lq2, lk2, subln_g, layer_idx):
    b, s, d = hin.shape
    nh = N_DIFF_HEADS
    dh2 = d // nh
    dh = dh2 // 2
    dv = dh2
    assert dh2 == LANES and s % ATT_CK == 0
    assert s <= POS_SPLIT * 256
    lam_init = 0.8 - 0.6 * math.exp(-0.3 * layer_idx)
    slopes = np.asarray([2.0 ** (-8.0 * (i + 1) / nh) for i in range(nh)], np.float64)
    cvals = slopes * LOG2E
    qa, ka = _aug_tables(nh, dh2, cvals)
    qscale = dh ** -0.5 * LOG2E

    tm = ATT_TQ
    const2 = lambda i, j: (0, 0)
    nt = s // tm
    norm_spec = pl.BlockSpec((None, nh, None, 1, dh2), lambda i, j: (i, 0, j, 0, 0))
    norm_shape = jax.ShapeDtypeStruct((b, nh, nt, 1, dh2), jnp.float32)
    q, k, v, qn, kn = pl.pallas_call(
        functools.partial(_qkv_kernel, d=d, nh=nh, tm=tm, qscale=qscale),
        grid=(b, s // tm),
        in_specs=[
            pl.BlockSpec((None, tm, d), lambda i, j: (i, j, 0)),
            pl.BlockSpec((1, d), const2),
            pl.BlockSpec((d, 3 * d), const2, pipeline_mode=pl.Buffered(1)),
            pl.BlockSpec((nh, 2, 3, dh2), lambda i, j: (0, 0, 0, 0)),
            pl.BlockSpec((nh, 2, 3, dh2), lambda i, j: (0, 0, 0, 0)),
        ],
        out_specs=[
            pl.BlockSpec((None, nh, 2, tm, dh2), lambda i, j: (i, 0, 0, j, 0)),
            pl.BlockSpec((None, nh, 2, tm, dh2), lambda i, j: (i, 0, 0, j, 0)),
            pl.BlockSpec((None, nh, tm, 2 * dv), lambda i, j: (i, 0, j, 0)),
            norm_spec,
            norm_spec,
        ],
        out_shape=[
            jax.ShapeDtypeStruct((b, nh, 2, s, dh2), jnp.bfloat16),
            jax.ShapeDtypeStruct((b, nh, 2, s, dh2), jnp.bfloat16),
            jax.ShapeDtypeStruct((b, nh, s, 2 * dv), jnp.bfloat16),
            norm_shape,
            norm_shape,
        ],
        compiler_params=pltpu.CompilerParams(
            dimension_semantics=("parallel", "parallel"), vmem_limit_bytes=VMEM_LIMIT_BYTES),
        name="qkv_proj",
    )(hin, g_pre.reshape(1, d), w_qkv.astype(jnp.bfloat16), qa, ka)

    tq, ck = ATT_TQ, ATT_CK
    wdt = ck + (ck - tq)
    dist = np.abs(np.arange(tq)[:, None] - np.arange(wdt)[None, :] + (ck - tq))
    bias = jnp.asarray(-cvals[:, None, None] * dist[None], jnp.float32)
    lam_p = jnp.stack([lq1, lk1, lq2, lk2]).astype(jnp.float32)

    full = pl.BlockSpec((None, nh, nt, 1, dh2), lambda i: (i, 0, 0, 0, 0))
    invc = jnp.broadcast_to(jnp.asarray(1.0 / cvals, jnp.float32)[:, None, None, None], (nh, 1, 1, dh2))
    reach = pl.pallas_call(
        functools.partial(_reach_kernel, s=s),
        grid=(b,),
        in_specs=[full, full, pl.BlockSpec((nh, 1, 1, dh2), lambda i: (0, 0, 0, 0))],
        out_specs=full,
        out_shape=jax.ShapeDtypeStruct((b, nh, nt, 1, dh2), jnp.int32),
        name="skip_reach",
    )(qn, kn, invc)
    reach = reach[..., 0, 0].reshape(-1)

    o = pl.pallas_call(
        functools.partial(_attn_kernel, s=s, dv=dv, lam_init=lam_init),
        grid=(b, nh, s // tq),
        in_specs=[
            pl.BlockSpec(memory_space=pltpu.SMEM),
            pl.BlockSpec((None, None, 2, tq, dh2), lambda i, h, j: (i, h, 0, j, 0)),
            pl.BlockSpec((None, None, 2, s, dh2), lambda i, h, j: (i, h, 0, 0, 0)),
            pl.BlockSpec((None, None, s, 2 * dv), lambda i, h, j: (i, h, 0, 0)),
            pl.BlockSpec((None, tq, wdt), lambda i, h, j: (h, 0, 0)),
            pl.BlockSpec((4, dh), lambda i, h, j: (0, 0)),
            pl.BlockSpec((1, dv), lambda i, h, j: (0, 0)),
        ],
        out_specs=pl.BlockSpec((None, tq, dv), lambda i, h, j: (i, j, h)),
        out_shape=jax.ShapeDtypeStruct((b, s, nh * dv), jnp.bfloat16),
        scratch_shapes=[
            pltpu.VMEM((2, tq, 1), jnp.float32),
            pltpu.VMEM((2, tq, 2 * dv), jnp.float32),
        ],
        compiler_params=pltpu.CompilerParams(
            dimension_semantics=("parallel", "parallel", "arbitrary"),
            vmem_limit_bytes=VMEM_LIMIT_BYTES),
        name="diff_attention",
    )(reach, q, k, v, bias, lam_p, subln_g.reshape(1, dv))
    return o


def kernel(x, norm_mix_pre, norm_mix_post, norm_ffn_pre, norm_ffn_post, fourier_w_o, diff_w_qkv,
           diff_lambda_q1, diff_lambda_k1, diff_lambda_q2, diff_lambda_k2, diff_subln_g, diff_w_o,
           ffn_w_gate, ffn_w_up, ffn_w_down):
    depth = norm_mix_pre.shape[0]
    h = x
    for i in range(depth):
        j = i // N_MIXERS
        mixer = None
        if i % N_MIXERS == 0:
            h = _fourier_layer(h, norm_mix_pre[i], norm_mix_post[i], fourier_w_o[j])
        else:
            a = _diff_attention_mixer(h, norm_mix_pre[i], diff_w_qkv[j], diff_lambda_q1[j],
                                      diff_lambda_k1[j], diff_lambda_q2[j], diff_lambda_k2[j],
                                      diff_subln_g[j], i)
            mixer = (a, diff_w_o[j], norm_mix_post[i])
        h = _ffn_layer(h, norm_ffn_pre[i], norm_ffn_post[i], ffn_w_gate[i], ffn_w_up[i], ffn_w_down[i],
                       mixer=mixer)
    return h
```

```python
import functools
import math

import numpy as np
import jax
import jax.numpy as jnp
from jax import lax
from jax.experimental import pallas as pl
from jax.experimental.pallas import tpu as pltpu

N_FOURIER_GROUPS = 8
N_DIFF_HEADS = 8
RMS_EPS = 1e-6
N_MIXERS = 2

LANES = 128
VMEM_LIMIT_BYTES = 56 * 1024 * 1024

FFT_N2 = 128
FFT_S2_PER_STEP = 16
FFT_K1_PER_STEP = 8
FFN_TOKEN_TILE = 1024
FFN_CHUNK = 256
ATT_TQ = 512
ATT_CK = 2048
POS_SPLIT = 256
LOG2E = 1.4426950408889634
SKIP_EXP2_ZERO = 153.0
SKIP_SLACK = 1.01


def _rms_scale(x):
    return lax.rsqrt(jnp.mean(x * x, axis=-1, keepdims=True) + RMS_EPS)


def _np_bf16(v):
    return np.asarray(v, np.float32).astype(jnp.bfloat16)


def _bf16_split3(v):
    v = np.asarray(v, np.float64)
    hi = _np_bf16(v).astype(np.float64)
    mid = _np_bf16(v - hi).astype(np.float64)
    lo = _np_bf16(v - hi - mid).astype(np.float64)
    return hi, mid, lo


def _fft_stage1_kernel(x_ref, g_ref, f1_ref, ar_ref, ai_ref, *, n1, d, nb):
    x = pltpu.einshape("sjd->jsd", x_ref[...]).reshape(nb * n1, d)
    xn = (x * _rms_scale(x) * g_ref[...]).astype(jnp.bfloat16)
    f1 = f1_ref[...]
    a = jnp.stack([jnp.dot(f1, xn[j * n1:(j + 1) * n1], preferred_element_type=jnp.float32)
                   for j in range(nb)], axis=0)
    a = pltpu.einshape("jkd->kjd", a)
    ar_ref[...] = a[:n1].astype(jnp.bfloat16)
    ai_ref[...] = a[n1:].astype(jnp.bfloat16)


def _fft_stage2_kernel(ar_ref, ai_ref, gm_ref, cs_ref, wo_ref, gpost_ref, xres_ref, out_ref,
                       *, n2, d, cg, kb):
    cs = cs_ref[...]
    zs = []
    for i in range(kb):
        a = jnp.concatenate([ar_ref[i], ai_ref[i]], axis=0)
        y = jnp.dot(gm_ref[i], a, preferred_element_type=jnp.float32)
        yr = y[:n2].astype(jnp.bfloat16)
        yi = y[n2:].astype(jnp.bfloat16)
        cols = []
        for gi in range(d // cg):
            lhs = jnp.concatenate([yr[:, gi * cg:(gi + 1) * cg], yi[:, gi * cg:(gi + 1) * cg]], axis=1)
            cols.append(jnp.dot(lhs, cs, preferred_element_type=jnp.float32))
        zs.append(jnp.concatenate(cols, axis=1).astype(jnp.bfloat16))
    z = jnp.concatenate(zs, axis=0)
    m = jnp.dot(z, wo_ref[...], preferred_element_type=jnp.float32)
    r = m * _rms_scale(m) * gpost_ref[...]
    out_ref[...] = xres_ref[...] + pltpu.einshape("ikd->kid", r.reshape(kb, n2, d))


def _fourier_layer(x, g_pre, g_post, w_o):
    b, s, d = x.shape
    n2 = FFT_N2
    n1 = s // n2
    cg = d // N_FOURIER_GROUPS
    nb = FFT_S2_PER_STEP
    assert n1 * n2 == s and n2 % nb == 0

    s1 = np.arange(n1)
    th1 = 2.0 * np.pi * ((s1[:, None] * s1[None, :]) % n1) / n1
    f1 = np.concatenate([np.cos(th1), -np.sin(th1)], axis=0)
    s2 = np.arange(n2)
    kk = (np.arange(n1)[:, None, None] + n1 * np.arange(n2)[None, :, None])
    th2 = 2.0 * np.pi * ((kk * s2[None, None, :]) % s) / s
    gr, gi = np.cos(th2), -np.sin(th2)
    gm = np.concatenate([np.concatenate([gr, -gi], axis=2),
                         np.concatenate([gi, gr], axis=2)], axis=1)
    c = np.arange(cg)
    thc = 2.0 * np.pi * ((c[:, None] * c[None, :]) % cg) / cg
    norm = 1.0 / math.sqrt(s * cg)
    cs = np.concatenate([np.cos(thc), np.sin(thc)], axis=0) * norm

    f1 = jnp.asarray(_np_bf16(f1))
    gm = jnp.asarray(_np_bf16(gm))
    cs = jnp.asarray(_np_bf16(cs))

    x4 = x.reshape(b, n1, n2, d)
    ar, ai = pl.pallas_call(
        functools.partial(_fft_stage1_kernel, n1=n1, d=d, nb=nb),
        grid=(b, n2 // nb),
        in_specs=[
            pl.BlockSpec((None, n1, nb, d), lambda i, j: (i, 0, j, 0)),
            pl.BlockSpec((1, d), lambda i, j: (0, 0)),
            pl.BlockSpec((2 * n1, n1), lambda i, j: (0, 0)),
        ],
        out_specs=[
            pl.BlockSpec((None, n1, nb, d), lambda i, j: (i, 0, j, 0)),
            pl.BlockSpec((None, n1, nb, d), lambda i, j: (i, 0, j, 0)),
        ],
        out_shape=[jax.ShapeDtypeStruct((b, n1, n2, d), jnp.bfloat16)] * 2,
        compiler_params=pltpu.CompilerParams(
            dimension_semantics=("parallel", "parallel"), vmem_limit_bytes=VMEM_LIMIT_BYTES),
        name="fft_stage1",
    )(x4, g_pre.reshape(1, d), f1)

    kb = min(FFT_K1_PER_STEP, n1)
    assert n1 % kb == 0
    xk = x.reshape(b, n2, n1, d)
    out = pl.pallas_call(
        functools.partial(_fft_stage2_kernel, n2=n2, d=d, cg=cg, kb=kb),
        grid=(b, n1 // kb),
        in_specs=[
            pl.BlockSpec((None, kb, n2, d), lambda i, j: (i, j, 0, 0)),
            pl.BlockSpec((None, kb, n2, d), lambda i, j: (i, j, 0, 0)),
            pl.BlockSpec((kb, 2 * n2, 2 * n2), lambda i, j: (j, 0, 0)),
            pl.BlockSpec((2 * cg, cg), lambda i, j: (0, 0)),
            pl.BlockSpec((d, d), lambda i, j: (0, 0)),
            pl.BlockSpec((1, d), lambda i, j: (0, 0)),
            pl.BlockSpec((None, n2, kb, d), lambda i, j: (i, 0, j, 0)),
        ],
        out_specs=pl.BlockSpec((None, n2, kb, d), lambda i, j: (i, 0, j, 0)),
        out_shape=jax.ShapeDtypeStruct((b, n2, n1, d), jnp.float32),
        compiler_params=pltpu.CompilerParams(
            dimension_semantics=("parallel", "parallel"), vmem_limit_bytes=VMEM_LIMIT_BYTES),
        name="fft_stage2",
    )(ar, ai, gm, cs, w_o.astype(jnp.bfloat16), g_post.reshape(1, d), xk)
    return out.reshape(b, s, d)


def _ffn_kernel(*refs, f, chunk, mixer_proj):
    if mixer_proj:
        a_ref, wo_ref, gmix_ref, x_ref, gpre_ref, gpost_ref, wg_ref, wu_ref, wd_ref, out_ref = refs
        m = jnp.dot(a_ref[...], wo_ref[...], preferred_element_type=jnp.float32)
        x = x_ref[...] + m * _rms_scale(m) * gmix_ref[...]
    else:
        x_ref, gpre_ref, gpost_ref, wg_ref, wu_ref, wd_ref, out_ref = refs
        x = x_ref[...]
    xn = (x * _rms_scale(x) * gpre_ref[...]).astype(jnp.bfloat16)
    acc = jnp.zeros(x.shape, jnp.float32)
    for c0 in range(0, f, chunk):
        c1 = min(c0 + chunk, f)
        gt = jnp.dot(xn, wg_ref[:, c0:c1], preferred_element_type=jnp.float32)
        up = jnp.dot(xn, wu_ref[:, c0:c1], preferred_element_type=jnp.float32)
        hh = (gt * jax.nn.sigmoid(gt) * up).astype(jnp.bfloat16)
        acc = acc + jnp.dot(hh, wd_ref[c0:c1, :], preferred_element_type=jnp.float32)
    out_ref[...] = x + acc * _rms_scale(acc) * gpost_ref[...]


def _ffn_layer(h, g_pre, g_post, w_gate, w_up, w_down, mixer=None):
    b, s, d = h.shape
    f = w_gate.shape[1]
    t = b * s
    tm = min(FFN_TOKEN_TILE, t)
    assert t % tm == 0
    const = lambda i: (0, 0)
    rows = pl.BlockSpec((tm, d), lambda i: (i, 0))
    vec = pl.BlockSpec((1, d), const)
    in_specs = [rows, vec, vec,
                pl.BlockSpec((d, f), const, pipeline_mode=pl.Buffered(1)),
                pl.BlockSpec((d, f), const, pipeline_mode=pl.Buffered(1)),
                pl.BlockSpec((f, d), const, pipeline_mode=pl.Buffered(1))]
    args = [h.reshape(t, d), g_pre.reshape(1, d), g_post.reshape(1, d),
            w_gate.astype(jnp.bfloat16), w_up.astype(jnp.bfloat16), w_down.astype(jnp.bfloat16)]
    if mixer is not None:
        a, w_o, g_mix = mixer
        in_specs = [rows, pl.BlockSpec((d, d), const, pipeline_mode=pl.Buffered(1)), vec] + in_specs
        args = [a.reshape(t, d), w_o.astype(jnp.bfloat16), g_mix.reshape(1, d)] + args
    out = pl.pallas_call(
        functools.partial(_ffn_kernel, f=f, chunk=FFN_CHUNK, mixer_proj=mixer is not None),
        grid=(t // tm,),
        in_specs=in_specs,
        out_specs=rows,
        out_shape=jax.ShapeDtypeStruct((t, d), jnp.float32),
        compiler_params=pltpu.CompilerParams(
            dimension_semantics=("parallel",), vmem_limit_bytes=VMEM_LIMIT_BYTES),
        name="swiglu_ffn",
    )(*args)
    return out.reshape(b, s, d)


def _qkv_kernel(x_ref, gpre_ref, w_ref, qa_ref, ka_ref, q_ref, k_ref, v_ref, qn_ref, kn_ref,
                *, d, nh, tm, qscale):
    x = x_ref[...]
    xn = (x * _rms_scale(x) * gpre_ref[...]).astype(jnp.bfloat16)
    dh2 = d // nh
    row = pl.program_id(1) * tm + lax.broadcasted_iota(jnp.int32, (tm, 1), 0)
    pos_lo = (row & (POS_SPLIT - 1)).astype(jnp.float32)
    pos_hi = (row & (-POS_SPLIT)).astype(jnp.float32)
    lane = lax.broadcasted_iota(jnp.int32, (1, dh2), 1)
    half = dh2 // 2
    q_all = jnp.dot(xn, w_ref[:, 0:d], preferred_element_type=jnp.float32) * qscale
    k_all = jnp.dot(xn, w_ref[:, d:2 * d], preferred_element_type=jnp.float32)
    v_all = jnp.dot(xn, w_ref[:, 2 * d:3 * d], preferred_element_type=jnp.float32)
    ones_col = jnp.where(lane == 0, 1.0, 0.0).astype(jnp.bfloat16)
    lane_col = lax.broadcasted_iota(jnp.int32, (dh2, 1), 0)
    comp_sum = jnp.where((lane_col >= half).astype(jnp.int32) == lane, 1.0, 0.0).astype(jnp.bfloat16)
    for h in range(nh):
        qh = q_all[:, h * dh2:(h + 1) * dh2]
        kh = k_all[:, h * dh2:(h + 1) * dh2]
        for src, dst in ((qh, qn_ref), (kh, kn_ref)):
            vb = src.astype(jnp.bfloat16).astype(jnp.float32)
            sq = jnp.dot((vb * vb).astype(jnp.bfloat16), comp_sum, preferred_element_type=jnp.float32)
            dst[h] = jnp.max(sq, axis=0, keepdims=True)
        for comp in range(2):
            data = (lane < half) if comp == 0 else (lane >= half)
            qa = qa_ref[h, comp]
            ka = ka_ref[h, comp]
            q_aug = qa[0:1] + qa[1:2] * pos_lo + qa[2:3] * pos_hi
            k_aug = ka[0:1] + ka[1:2] * pos_lo + ka[2:3] * pos_hi
            q_ref[h, comp] = jnp.where(data, qh, q_aug).astype(jnp.bfloat16)
            k_ref[h, comp] = jnp.where(data, kh, k_aug).astype(jnp.bfloat16)
        vh = v_all[:, h * dh2:(h + 1) * dh2].astype(jnp.bfloat16)
        v_ref[h] = jnp.concatenate([vh, jnp.broadcast_to(ones_col, (tm, dh2))], axis=1)


def _aug_tables(nh, dh2, cvals):
    half = dh2 // 2
    qa = np.zeros((nh, 2, 3, dh2), np.float64)
    ka = np.zeros((nh, 2, 3, dh2), np.float64)
    for h in range(nh):
        pieces = _bf16_split3(cvals[h])
        for comp in range(2):
            base = half if comp == 0 else 0
            for p in range(3):
                for part in (1, 2):
                    qa[h, comp, 0, base + 3 * (part - 1) + p] = pieces[p]
                    ka[h, comp, part, base + 3 * (part - 1) + p] = 1.0
                    qa[h, comp, part, base + 6 + 3 * (part - 1) + p] = -1.0
                    ka[h, comp, 0, base + 6 + 3 * (part - 1) + p] = pieces[p]
    return jnp.asarray(qa, jnp.float32), jnp.asarray(ka, jnp.float32)


def _reach_kernel(qn_ref, kn_ref, invc_ref, reach_ref, *, s):
    kmax = jnp.max(kn_ref[...], axis=1, keepdims=True)
    prod = qn_ref[...] * kmax
    u2 = jnp.maximum(prod[..., 0:1], prod[..., 1:2])
    thr = 2.0 * SKIP_SLACK * jnp.sqrt(u2) + SKIP_EXP2_ZERO
    reach = jnp.minimum(thr * invc_ref[..., 0:1], float(s)).astype(jnp.int32) + 1
    reach_ref[...] = jnp.broadcast_to(reach, reach_ref.shape)


def _attn_kernel(reach_ref, q_ref, k_ref, v_ref, bias_ref, lam_ref, g_ref, o_ref, m_scr, acc_scr,
                 *, s, dv, lam_init):
    tq, ck = ATT_TQ, ATT_CK
    nch = s // ck
    hd = pl.program_id(1)
    qi = pl.program_id(2)
    i0 = qi * tq
    lane = lax.broadcasted_iota(jnp.int32, (1, LANES), 1)
    half = LANES // 2

    q_left, q_right, q_diag = [], [], []
    for comp in range(2):
        data = (lane < half) if comp == 0 else (lane >= half)
        qc = q_ref[comp]
        q_left.append(qc)
        q_right.append(jnp.where(data, qc, -qc))
        q_diag.append(jnp.where(data, qc, jnp.zeros_like(qc)))

    def scores(j0, table_off, nk):
        j0 = pl.multiple_of(j0, POS_SPLIT)
        ts = []
        for comp in range(2):
            kc = k_ref[comp, pl.ds(j0, nk), :]
            if table_off is None:
                qv = jnp.where(j0 < i0, q_left[comp], q_right[comp])
            else:
                qv = q_diag[comp]
            t = lax.dot_general(qv, kc, (((1,), (1,)), ((), ())),
                                preferred_element_type=jnp.float32)
            if table_off is not None:
                t = t + bias_ref[:, pl.ds(pl.multiple_of(table_off, POS_SPLIT), nk)]
            ts.append(t)
        return j0, ts, nk

    def accumulate(j0, ts, nk, first):
        vc = v_ref[pl.ds(j0, nk), :]
        for comp in range(2):
            t = ts[comp]
            m_new = jnp.max(t, axis=-1, keepdims=True)
            if not first:
                m_old = m_scr[comp]
                m_new = jnp.maximum(m_old, m_new)
            p = jnp.exp2(t - m_new).astype(jnp.bfloat16)
            pv = jnp.dot(p, vc, preferred_element_type=jnp.float32)
            if first:
                acc_scr[comp] = pv
            else:
                acc_scr[comp] = jnp.exp2(m_old - m_new) * acc_scr[comp] + pv
            m_scr[comp] = m_new

    reach = reach_ref[(pl.program_id(0) * pl.num_programs(1) + hd) * pl.num_programs(2) + qi]
    lo_key = jnp.maximum(i0 - reach, 0)
    hi_key = jnp.minimum(i0 + tq + reach, s)
    start0 = lo_key & (-POS_SPLIT)
    start0 = jnp.where(((i0 - start0) & (ck - 1)) + tq > ck, start0 - POS_SPLIT, start0)
    width_need = (hi_key - start0 + (ck - 1)) // ck

    def sweep(width):
        def run():
            start = jnp.minimum(start0, s - width * ck)
            rel = i0 - start
            own = rel // ck
            pending = scores(start + own * ck, (ck - tq) - (rel & (ck - 1)), ck)
            for n in range(1, width):
                r = own + n
                nxt = scores(start + jnp.where(r >= width, r - width, r) * ck, None, ck)
                accumulate(*pending, first=(n == 1))
                pending = nxt
            accumulate(*pending, first=(width == 1))
        return run

    def dispatch(width):
        if width == nch:
            sweep(nch)()
        else:
            lax.cond(width_need <= width, sweep(width), lambda: dispatch(width + 1))

    dispatch(1)

    lam_p = lam_ref[...]
    lam = (jnp.exp(jnp.sum(lam_p[0:1] * lam_p[1:2], axis=-1, keepdims=True))
           - jnp.exp(jnp.sum(lam_p[2:3] * lam_p[3:4], axis=-1, keepdims=True)) + lam_init)
    a0 = acc_scr[0]
    a1 = acc_scr[1]
    o = a0[:, :dv] / a0[:, dv:dv + 1] - lam * (a1[:, :dv] / a1[:, dv:dv + 1])
    o = o * _rms_scale(o) * g_ref[...] * (1.0 - lam_init)
    o_ref[...] = o.astype(jnp.bfloat16)


def _diff_attention_mixer(hin, g_pre, w_qkv, lq1, lk1, lq2, lk2, subln_g, layer_idx):
    b, s, d = hin.shape
    nh = N_DIFF_HEADS
    dh2 = d // nh
    dh = dh2 // 2
    dv = dh2
    assert dh2 == LANES and s % ATT_CK == 0
    assert s <= POS_SPLIT * 256
    lam_init = 0.8 - 0.6 * math.exp(-0.3 * layer_idx)
    slopes = np.asarray([2.0 ** (-8.0 * (i + 1) / nh) for i in range(nh)], np.float64)
    cvals = slopes * LOG2E
    qa, ka = _aug_tables(nh, dh2, cvals)
    qscale = dh ** -0.5 * LOG2E

    tm = ATT_TQ
    const2 = lambda i, j: (0, 0)
    nt = s // tm
    norm_spec = pl.BlockSpec((None, nh, None, 1, dh2), lambda i, j: (i, 0, j, 0, 0))
    norm_shape = jax.ShapeDtypeStruct((b, nh, nt, 1, dh2), jnp.float32)
    q, k, v, qn, kn = pl.pallas_call(
        functools.partial(_qkv_kernel, d=d, nh=nh, tm=tm, qscale=qscale),
        grid=(b, s // tm),
        in_specs=[
            pl.BlockSpec((None, tm, d), lambda i, j: (i, j, 0)),
            pl.BlockSpec((1, d), const2),
            pl.BlockSpec((d, 3 * d), const2, pipeline_mode=pl.Buffered(1)),
            pl.BlockSpec((nh, 2, 3, dh2), lambda i, j: (0, 0, 0, 0)),
            pl.BlockSpec((nh, 2, 3, dh2), lambda i, j: (0, 0, 0, 0)),
        ],
        out_specs=[
            pl.BlockSpec((None, nh, 2, tm, dh2), lambda i, j: (i, 0, 0, j, 0)),
            pl.BlockSpec((None, nh, 2, tm, dh2), lambda i, j: (i, 0, 0, j, 0)),
            pl.BlockSpec((None, nh, tm, 2 * dv), lambda i, j: (i, 0, j, 0)),
            norm_spec,
            norm_spec,
        ],
        out_shape=[
            jax.ShapeDtypeStruct((b, nh, 2, s, dh2), jnp.bfloat16),
            jax.ShapeDtypeStruct((b, nh, 2, s, dh2), jnp.bfloat16),
            jax.ShapeDtypeStruct((b, nh, s, 2 * dv), jnp.bfloat16),
            norm_shape,
            norm_shape,
        ],
        compiler_params=pltpu.CompilerParams(
            dimension_semantics=("parallel", "parallel"), vmem_limit_bytes=VMEM_LIMIT_BYTES),
        name="qkv_proj",
    )(hin, g_pre.reshape(1, d), w_qkv.astype(jnp.bfloat16), qa, ka)

    tq, ck = ATT_TQ, ATT_CK
    wdt = ck + (ck - tq)
    dist = np.abs(np.arange(tq)[:, None] - np.arange(wdt)[None, :] + (ck - tq))
    bias = jnp.asarray(-cvals[:, None, None] * dist[None], jnp.float32)
    lam_p = jnp.stack([lq1, lk1, lq2, lk2]).astype(jnp.float32)

    full = pl.BlockSpec((None, nh, nt, 1, dh2), lambda i: (i, 0, 0, 0, 0))
    invc = jnp.broadcast_to(jnp.asarray(1.0 / cvals, jnp.float32)[:, None, None, None], (nh, 1, 1, dh2))
    reach = pl.pallas_call(
        functools.partial(_reach_kernel, s=s),
        grid=(b,),
        in_specs=[full, full, pl.BlockSpec((nh, 1, 1, dh2), lambda i: (0, 0, 0, 0))],
        out_specs=full,
        out_shape=jax.ShapeDtypeStruct((b, nh, nt, 1, dh2), jnp.int32),
        name="skip_reach",
    )(qn, kn, invc)
    reach = reach[..., 0, 0].reshape(-1)

    o = pl.pallas_call(
        functools.partial(_attn_kernel, s=s, dv=dv, lam_init=lam_init),
        grid=(b, nh, s // tq),
        in_specs=[
            pl.BlockSpec(memory_space=pltpu.SMEM),
            pl.BlockSpec((None, None, 2, tq, dh2), lambda i, h, j: (i, h, 0, j, 0)),
            pl.BlockSpec((None, None, 2, s, dh2), lambda i, h, j: (i, h, 0, 0, 0)),
            pl.BlockSpec((None, None, s, 2 * dv), lambda i, h, j: (i, h, 0, 0)),
            pl.BlockSpec((None, tq, wdt), lambda i, h, j: (h, 0, 0)),
            pl.BlockSpec((4, dh), lambda i, h, j: (0, 0)),
            pl.BlockSpec((1, dv), lambda i, h, j: (0, 0)),
        ],
        out_specs=pl.BlockSpec((None, tq, dv), lambda i, h, j: (i, j, h)),
        out_shape=jax.ShapeDtypeStruct((b, s, nh * dv), jnp.bfloat16),
        scratch_shapes=[
            pltpu.VMEM((2, tq, 1), jnp.float32),
            pltpu.VMEM((2, tq, 2 * dv), jnp.float32),
        ],
        compiler_params=pltpu.CompilerParams(
            dimension_semantics=("parallel", "parallel", "arbitrary"),
            vmem_limit_bytes=VMEM_LIMIT_BYTES),
        name="diff_attention",
    )(reach, q, k, v, bias, lam_p, subln_g.reshape(1, dv))
    return o


def kernel(x, norm_mix_pre, norm_mix_post, norm_ffn_pre, norm_ffn_post, fourier_w_o, diff_w_qkv,
           diff_lambda_q1, diff_lambda_k1, diff_lambda_q2, diff_lambda_k2, diff_subln_g, diff_w_o,
           ffn_w_gate, ffn_w_up, ffn_w_down):
    depth = norm_mix_pre.shape[0]
    h = x
    for i in range(depth):
        j = i // N_MIXERS
        mixer = None
        if i % N_MIXERS == 0:
            h = _fourier_layer(h, norm_mix_pre[i], norm_mix_post[i], fourier_w_o[j])
        else:
            a = _diff_attention_mixer(h, norm_mix_pre[i], diff_w_qkv[j], diff_lambda_q1[j],
                                      diff_lambda_k1[j], diff_lambda_q2[j], diff_lambda_k2[j],
                                      diff_subln_g[j], i)
            mixer = (a, diff_w_o[j], norm_mix_post[i])
        h = _ffn_layer(h, norm_ffn_pre[i], norm_ffn_post[i], ffn_w_gate[i], ffn_w_up[i], ffn_w_down[i],
                       mixer=mixer)
    return h
```

```python
import functools
import math

import numpy as np
import jax
import jax.numpy as jnp
from jax import lax
from jax.experimental import pallas as pl
from jax.experimental.pallas import tpu as pltpu

N_FOURIER_GROUPS = 8
N_DIFF_HEADS = 8
RMS_EPS = 1e-6
N_MIXERS = 2

LANES = 128
VMEM_LIMIT_BYTES = 56 * 1024 * 1024

FFT_N2 = 128
FFT_S2_PER_STEP = 16
FFT_K1_PER_STEP = 8
FFN_TOKEN_TILE = 1024
FFN_CHUNK = 256
ATT_TQ = 512
ATT_CK = 2048
POS_SPLIT = 256
LOG2E = 1.4426950408889634
SKIP_EXP2_ZERO = 153.0
SKIP_SLACK = 1.01
SHIFT_MAX_LOG2_RANGE = 100.0


def _rms_scale(x):
    return lax.rsqrt(jnp.mean(x * x, axis=-1, keepdims=True) + RMS_EPS)


def _np_bf16(v):
    return np.asarray(v, np.float32).astype(jnp.bfloat16)


def _bf16_split3(v):
    v = np.asarray(v, np.float64)
    hi = _np_bf16(v).astype(np.float64)
    mid = _np_bf16(v - hi).astype(np.float64)
    lo = _np_bf16(v - hi - mid).astype(np.float64)
    return hi, mid, lo


def _fft_stage1_kernel(x_ref, g_ref, f1_ref, ar_ref, ai_ref, *, n1, d, nb):
    x = pltpu.einshape("sjd->jsd", x_ref[...]).reshape(nb * n1, d)
    xn = (x * _rms_scale(x) * g_ref[...]).astype(jnp.bfloat16)
    f1 = f1_ref[...]
    a = jnp.stack([jnp.dot(f1, xn[j * n1:(j + 1) * n1], preferred_element_type=jnp.float32)
                   for j in range(nb)], axis=0)
    a = pltpu.einshape("jkd->kjd", a)
    ar_ref[...] = a[:n1].astype(jnp.bfloat16)
    ai_ref[...] = a[n1:].astype(jnp.bfloat16)


def _fft_stage2_kernel(ar_ref, ai_ref, gm_ref, cs_ref, wo_ref, gpost_ref, xres_ref, out_ref,
                       *, n2, d, cg, kb):
    cs = cs_ref[...]
    zs = []
    for i in range(kb):
        a = jnp.concatenate([ar_ref[i], ai_ref[i]], axis=0)
        y = jnp.dot(gm_ref[i], a, preferred_element_type=jnp.float32)
        yr = y[:n2].astype(jnp.bfloat16)
        yi = y[n2:].astype(jnp.bfloat16)
        cols = []
        for gi in range(d // cg):
            lhs = jnp.concatenate([yr[:, gi * cg:(gi + 1) * cg], yi[:, gi * cg:(gi + 1) * cg]], axis=1)
            cols.append(jnp.dot(lhs, cs, preferred_element_type=jnp.float32))
        zs.append(jnp.concatenate(cols, axis=1).astype(jnp.bfloat16))
    z = jnp.concatenate(zs, axis=0)
    m = jnp.dot(z, wo_ref[...], preferred_element_type=jnp.float32)
    r = m * _rms_scale(m) * gpost_ref[...]
    out_ref[...] = xres_ref[...] + pltpu.einshape("ikd->kid", r.reshape(kb, n2, d))


def _fourier_layer(x, g_pre, g_post, w_o):
    b, s, d = x.shape
    n2 = FFT_N2
    n1 = s // n2
    cg = d // N_FOURIER_GROUPS
    nb = FFT_S2_PER_STEP
    assert n1 * n2 == s and n2 % nb == 0

    s1 = np.arange(n1)
    th1 = 2.0 * np.pi * ((s1[:, None] * s1[None, :]) % n1) / n1
    f1 = np.concatenate([np.cos(th1), -np.sin(th1)], axis=0)
    s2 = np.arange(n2)
    kk = (np.arange(n1)[:, None, None] + n1 * np.arange(n2)[None, :, None])
    th2 = 2.0 * np.pi * ((kk * s2[None, None, :]) % s) / s
    gr, gi = np.cos(th2), -np.sin(th2)
    gm = np.concatenate([np.concatenate([gr, -gi], axis=2),
                         np.concatenate([gi, gr], axis=2)], axis=1)
    c = np.arange(cg)
    thc = 2.0 * np.pi * ((c[:, None] * c[None, :]) % cg) / cg
    norm = 1.0 / math.sqrt(s * cg)
    cs = np.concatenate([np.cos(thc), np.sin(thc)], axis=0) * norm

    f1 = jnp.asarray(_np_bf16(f1))
    gm = jnp.asarray(_np_bf16(gm))
    cs = jnp.asarray(_np_bf16(cs))

    x4 = x.reshape(b, n1, n2, d)
    ar, ai = pl.pallas_call(
        functools.partial(_fft_stage1_kernel, n1=n1, d=d, nb=nb),
        grid=(b, n2 // nb),
        in_specs=[
            pl.BlockSpec((None, n1, nb, d), lambda i, j: (i, 0, j, 0)),
            pl.BlockSpec((1, d), lambda i, j: (0, 0)),
            pl.BlockSpec((2 * n1, n1), lambda i, j: (0, 0)),
        ],
        out_specs=[
            pl.BlockSpec((None, n1, nb, d), lambda i, j: (i, 0, j, 0)),
            pl.BlockSpec((None, n1, nb, d), lambda i, j: (i, 0, j, 0)),
        ],
        out_shape=[jax.ShapeDtypeStruct((b, n1, n2, d), jnp.bfloat16)] * 2,
        compiler_params=pltpu.CompilerParams(
            dimension_semantics=("parallel", "parallel"), vmem_limit_bytes=VMEM_LIMIT_BYTES),
        name="fft_stage1",
    )(x4, g_pre.reshape(1, d), f1)

    kb = min(FFT_K1_PER_STEP, n1)
    assert n1 % kb == 0
    xk = x.reshape(b, n2, n1, d)
    out = pl.pallas_call(
        functools.partial(_fft_stage2_kernel, n2=n2, d=d, cg=cg, kb=kb),
        grid=(b, n1 // kb),
        in_specs=[
            pl.BlockSpec((None, kb, n2, d), lambda i, j: (i, j, 0, 0)),
            pl.BlockSpec((None, kb, n2, d), lambda i, j: (i, j, 0, 0)),
            pl.BlockSpec((kb, 2 * n2, 2 * n2), lambda i, j: (j, 0, 0)),
            pl.BlockSpec((2 * cg, cg), lambda i, j: (0, 0)),
            pl.BlockSpec((d, d), lambda i, j: (0, 0)),
            pl.BlockSpec((1, d), lambda i, j: (0, 0)),
            pl.BlockSpec((None, n2, kb, d), lambda i, j: (i, 0, j, 0)),
        ],
        out_specs=pl.BlockSpec((None, n2, kb, d), lambda i, j: (i, 0, j, 0)),
        out_shape=jax.ShapeDtypeStruct((b, n2, n1, d), jnp.float32),
        compiler_params=pltpu.CompilerParams(
            dimension_semantics=("parallel", "parallel"), vmem_limit_bytes=VMEM_LIMIT_BYTES),
        name="fft_stage2",
    )(ar, ai, gm, cs, w_o.astype(jnp.bfloat16), g_post.reshape(1, d), xk)
    return out.reshape(b, s, d)


def _ffn_kernel(*refs, f, chunk, mixer_proj):
    if mixer_proj:
        a_ref, wo_ref, gmix_ref, x_ref, gpre_ref, gpost_ref, wg_ref, wu_ref, wd_ref, out_ref = refs
        m = jnp.dot(a_ref[...], wo_ref[...], preferred_element_type=jnp.float32)
        x = x_ref[...] + m * _rms_scale(m) * gmix_ref[...]
    else:
        x_ref, gpre_ref, gpost_ref, wg_ref, wu_ref, wd_ref, out_ref = refs
        x = x_ref[...]
    xn = (x * _rms_scale(x) * gpre_ref[...]).astype(jnp.bfloat16)
    acc = jnp.zeros(x.shape, jnp.float32)
    for c0 in range(0, f, chunk):
        c1 = min(c0 + chunk, f)
        gt = jnp.dot(xn, wg_ref[:, c0:c1], preferred_element_type=jnp.float32)
        up = jnp.dot(xn, wu_ref[:, c0:c1], preferred_element_type=jnp.float32)
        hh = (gt * jax.nn.sigmoid(gt) * up).astype(jnp.bfloat16)
        acc = acc + jnp.dot(hh, wd_ref[c0:c1, :], preferred_element_type=jnp.float32)
    out_ref[...] = x + acc * _rms_scale(acc) * gpost_ref[...]


def _ffn_layer(h, g_pre, g_post, w_gate, w_up, w_down, mixer=None):
    b, s, d = h.shape
    f = w_gate.shape[1]
    t = b * s
    tm = min(FFN_TOKEN_TILE, t)
    assert t % tm == 0
    const = lambda i: (0, 0)
    rows = pl.BlockSpec((tm, d), lambda i: (i, 0))
    vec = pl.BlockSpec((1, d), const)
    in_specs = [rows, vec, vec,
                pl.BlockSpec((d, f), const, pipeline_mode=pl.Buffered(1)),
                pl.BlockSpec((d, f), const, pipeline_mode=pl.Buffered(1)),
                pl.BlockSpec((f, d), const, pipeline_mode=pl.Buffered(1))]
    args = [h.reshape(t, d), g_pre.reshape(1, d), g_post.reshape(1, d),
            w_gate.astype(jnp.bfloat16), w_up.astype(jnp.bfloat16), w_down.astype(jnp.bfloat16)]
    if mixer is not None:
        a, w_o, g_mix = mixer
        in_specs = [rows, pl.BlockSpec((d, d), const, pipeline_mode=pl.Buffered(1)), vec] + in_specs
        args = [a.reshape(t, d), w_o.astype(jnp.bfloat16), g_mix.reshape(1, d)] + args
    out = pl.pallas_call(
        functools.partial(_ffn_kernel, f=f, chunk=FFN_CHUNK, mixer_proj=mixer is not None),
        grid=(t // tm,),
        in_specs=in_specs,
        out_specs=rows,
        out_shape=jax.ShapeDtypeStruct((t, d), jnp.float32),
        compiler_params=pltpu.CompilerParams(
            dimension_semantics=("parallel",), vmem_limit_bytes=VMEM_LIMIT_BYTES),
        name="swiglu_ffn",
    )(*args)
    return out.reshape(b, s, d)


def _qkv_kernel(x_ref, gpre_ref, w_ref, qa_ref, ka_ref, q_ref, k_ref, v_ref, qn_ref, kn_ref,
                *, d, nh, tm, qscale):
    x = x_ref[...]
    xn = (x * _rms_scale(x) * gpre_ref[...]).astype(jnp.bfloat16)
    dh2 = d // nh
    row = pl.program_id(1) * tm + lax.broadcasted_iota(jnp.int32, (tm, 1), 0)
    pos_lo = (row & (POS_SPLIT - 1)).astype(jnp.float32)
    pos_hi = (row & (-POS_SPLIT)).astype(jnp.float32)
    lane = lax.broadcasted_iota(jnp.int32, (1, dh2), 1)
    half = dh2 // 2
    q_all = jnp.dot(xn, w_ref[:, 0:d], preferred_element_type=jnp.float32) * qscale
    k_all = jnp.dot(xn, w_ref[:, d:2 * d], preferred_element_type=jnp.float32)
    v_all = jnp.dot(xn, w_ref[:, 2 * d:3 * d], preferred_element_type=jnp.float32)
    ones_col = jnp.where(lane == 0, 1.0, 0.0).astype(jnp.bfloat16)
    lane_col = lax.broadcasted_iota(jnp.int32, (dh2, 1), 0)
    comp_sum = jnp.where((lane_col >= half).astype(jnp.int32) == lane, 1.0, 0.0).astype(jnp.bfloat16)
    for h in range(nh):
        qh = q_all[:, h * dh2:(h + 1) * dh2]
        kh = k_all[:, h * dh2:(h + 1) * dh2]
        for src, dst in ((qh, qn_ref), (kh, kn_ref)):
            vb = src.astype(jnp.bfloat16).astype(jnp.float32)
            sq = jnp.dot((vb * vb).astype(jnp.bfloat16), comp_sum, preferred_element_type=jnp.float32)
            dst[h] = jnp.max(sq, axis=0, keepdims=True)
        for comp in range(2):
            data = (lane < half) if comp == 0 else (lane >= half)
            qa = qa_ref[h, comp]
            ka = ka_ref[h, comp]
            q_aug = qa[0:1] + qa[1:2] * pos_lo + qa[2:3] * pos_hi
            k_aug = ka[0:1] + ka[1:2] * pos_lo + ka[2:3] * pos_hi
            q_ref[h, comp] = jnp.where(data, qh, q_aug).astype(jnp.bfloat16)
            k_ref[h, comp] = jnp.where(data, kh, k_aug).astype(jnp.bfloat16)
        vh = v_all[:, h * dh2:(h + 1) * dh2].astype(jnp.bfloat16)
        v_ref[h] = jnp.concatenate([vh, jnp.broadcast_to(ones_col, (tm, dh2))], axis=1)


def _aug_tables(nh, dh2, cvals):
    half = dh2 // 2
    qa = np.zeros((nh, 2, 3, dh2), np.float64)
    ka = np.zeros((nh, 2, 3, dh2), np.float64)
    for h in range(nh):
        pieces = _bf16_split3(cvals[h])
        for comp in range(2):
            base = half if comp == 0 else 0
            for p in range(3):
                for part in (1, 2):
                    qa[h, comp, 0, base + 3 * (part - 1) + p] = pieces[p]
                    ka[h, comp, part, base + 3 * (part - 1) + p] = 1.0
                    qa[h, comp, part, base + 6 + 3 * (part - 1) + p] = -1.0
                    ka[h, comp, 0, base + 6 + 3 * (part - 1) + p] = pieces[p]
    return jnp.asarray(qa, jnp.float32), jnp.asarray(ka, jnp.float32)


def _shift_kernel(qn_ref, kn_ref, shift_ref):
    kmax = jnp.max(kn_ref[...], axis=1, keepdims=True)
    prod = qn_ref[...] * kmax
    u = jnp.sqrt(jnp.maximum(prod[..., 0:1], prod[..., 1:2]))
    shift = jnp.where(2.0 * SKIP_SLACK * u <= SHIFT_MAX_LOG2_RANGE, SKIP_SLACK * u, -1.0)
    shift_ref[...] = jnp.broadcast_to(shift, shift_ref.shape)


def _attn_kernel(reach_ref, shift_ref, q_ref, k_ref, v_ref, bias_ref, lam_ref, g_ref, o_ref,
                 m_scr, acc_scr, *, s, dv, lam_init):
    tq, ck = ATT_TQ, ATT_CK
    nch = s // ck
    hd = pl.program_id(1)
    qi = pl.program_id(2)
    i0 = qi * tq
    lane = lax.broadcasted_iota(jnp.int32, (1, LANES), 1)
    half = LANES // 2

    q_left, q_right, q_diag = [], [], []
    for comp in range(2):
        data = (lane < half) if comp == 0 else (lane >= half)
        qc = q_ref[comp]
        q_left.append(qc)
        q_right.append(jnp.where(data, qc, -qc))
        q_diag.append(jnp.where(data, qc, jnp.zeros_like(qc)))

    def scores(j0, table_off, nk):
        j0 = pl.multiple_of(j0, POS_SPLIT)
        ts = []
        for comp in range(2):
            kc = k_ref[comp, pl.ds(j0, nk), :]
            if table_off is None:
                qv = jnp.where(j0 < i0, q_left[comp], q_right[comp])
            else:
                qv = q_diag[comp]
            t = lax.dot_general(qv, kc, (((1,), (1,)), ((), ())),
                                preferred_element_type=jnp.float32)
            if table_off is not None:
                t = t + bias_ref[:, pl.ds(pl.multiple_of(table_off, POS_SPLIT), nk)]
            ts.append(t)
        return j0, ts, nk

    def accumulate(j0, ts, nk, first):
        vc = v_ref[pl.ds(j0, nk), :]
        for comp in range(2):
            p = jnp.exp2(ts[comp] - shift).astype(jnp.bfloat16)
            pv = jnp.dot(p, vc, preferred_element_type=jnp.float32)
            acc_scr[comp] = pv if first else acc_scr[comp] + pv

    def accumulate_online(j0, ts, nk, first):
        vc = v_ref[pl.ds(j0, nk), :]
        for comp in range(2):
            t = ts[comp]
            m_new = jnp.max(t, axis=-1, keepdims=True)
            if not first:
                m_old = m_scr[comp]
                m_new = jnp.maximum(m_old, m_new)
            p = jnp.exp2(t - m_new).astype(jnp.bfloat16)
            pv = jnp.dot(p, vc, preferred_element_type=jnp.float32)
            if first:
                acc_scr[comp] = pv
            else:
                acc_scr[comp] = jnp.exp2(m_old - m_new) * acc_scr[comp] + pv
            m_scr[comp] = m_new

    tile = (pl.program_id(0) * pl.num_programs(1) + hd) * pl.num_programs(2) + qi
    shift = shift_ref[tile]

    reach = reach_ref[hd]
    lo_key = jnp.maximum(i0 - reach, 0)
    hi_key = jnp.minimum(i0 + tq + reach, s)
    start0 = lo_key & (-POS_SPLIT)
    start0 = jnp.where(((i0 - start0) & (ck - 1)) + tq > ck, start0 - POS_SPLIT, start0)
    width_need = (hi_key - start0 + (ck - 1)) // ck

    def sweep(width):
        def run():
            start = jnp.minimum(start0, s - width * ck)
            rel = i0 - start
            own = rel // ck
            pending = scores(start + own * ck, (ck - tq) - (rel & (ck - 1)), ck)
            for n in range(1, width):
                r = own + n
                nxt = scores(start + jnp.where(r >= width, r - width, r) * ck, None, ck)
                accumulate(*pending, first=(n == 1))
                pending = nxt
            accumulate(*pending, first=(width == 1))
        return run

    def dispatch(width):
        if width == nch:
            sweep(nch)()
        else:
            lax.cond(width_need <= width, sweep(width), lambda: dispatch(width + 1))

    def online_sweep():
        own = i0 // ck
        accumulate_online(*scores(own * ck, (ck - tq) - (i0 & (ck - 1)), ck), first=True)

        def body(n, carry):
            r = own + n
            accumulate_online(*scores(jnp.where(r >= nch, r - nch, r) * ck, None, ck), first=False)
            return carry

        lax.fori_loop(1, nch, body, 0)

    lax.cond(shift >= 0.0, lambda: dispatch(1), online_sweep)

    lam_p = lam_ref[...]
    lam = (jnp.exp(jnp.sum(lam_p[0:1] * lam_p[1:2], axis=-1, keepdims=True))
           - jnp.exp(jnp.sum(lam_p[2:3] * lam_p[3:4], axis=-1, keepdims=True)) + lam_init)
    a0 = acc_scr[0]
    a1 = acc_scr[1]
    o = a0[:, :dv] / a0[:, dv:dv + 1] - lam * (a1[:, :dv] / a1[:, dv:dv + 1])
    o = o * _rms_scale(o) * g_ref[...] * (1.0 - lam_init)
    o_ref[...] = o.astype(jnp.bfloat16)


def _diff_attention_mixer(hin, g_pre, w_qkv, lq1, lk1, lq2, lk2, subln_g, layer_idx):
    b, s, d = hin.shape
    nh = N_DIFF_HEADS
    dh2 = d // nh
    dh = dh2 // 2
    dv = dh2
    assert dh2 == LANES and s % ATT_CK == 0
    assert s <= POS_SPLIT * 256
    lam_init = 0.8 - 0.6 * math.exp(-0.3 * layer_idx)
    slopes = np.asarray([2.0 ** (-8.0 * (i + 1) / nh) for i in range(nh)], np.float64)
    cvals = slopes * LOG2E
    qa, ka = _aug_tables(nh, dh2, cvals)
    qscale = dh ** -0.5 * LOG2E

    tm = ATT_TQ
    const2 = lambda i, j: (0, 0)
    nt = s // tm
    norm_spec = pl.BlockSpec((None, nh, None, 1, dh2), lambda i, j: (i, 0, j, 0, 0))
    norm_shape = jax.ShapeDtypeStruct((b, nh, nt, 1, dh2), jnp.float32)
    q, k, v, qn, kn = pl.pallas_call(
        functools.partial(_qkv_kernel, d=d, nh=nh, tm=tm, qscale=qscale),
        grid=(b, s // tm),
        in_specs=[
            pl.BlockSpec((None, tm, d), lambda i, j: (i, j, 0)),
            pl.BlockSpec((1, d), const2),
            pl.BlockSpec((d, 3 * d), const2, pipeline_mode=pl.Buffered(1)),
            pl.BlockSpec((nh, 2, 3, dh2), lambda i, j: (0, 0, 0, 0)),
            pl.BlockSpec((nh, 2, 3, dh2), lambda i, j: (0, 0, 0, 0)),
        ],
        out_specs=[
            pl.BlockSpec((None, nh, 2, tm, dh2), lambda i, j: (i, 0, 0, j, 0)),
            pl.BlockSpec((None, nh, 2, tm, dh2), lambda i, j: (i, 0, 0, j, 0)),
            pl.BlockSpec((None, nh, tm, 2 * dv), lambda i, j: (i, 0, j, 0)),
            norm_spec,
            norm_spec,
        ],
        out_shape=[
            jax.ShapeDtypeStruct((b, nh, 2, s, dh2), jnp.bfloat16),
            jax.ShapeDtypeStruct((b, nh, 2, s, dh2), jnp.bfloat16),
            jax.ShapeDtypeStruct((b, nh, s, 2 * dv), jnp.bfloat16),
            norm_shape,
            norm_shape,
        ],
        compiler_params=pltpu.CompilerParams(
            dimension_semantics=("parallel", "parallel"), vmem_limit_bytes=VMEM_LIMIT_BYTES),
        name="qkv_proj",
    )(hin, g_pre.reshape(1, d), w_qkv.astype(jnp.bfloat16), qa, ka)

    tq, ck = ATT_TQ, ATT_CK
    wdt = ck + (ck - tq)
    dist = np.abs(np.arange(tq)[:, None] - np.arange(wdt)[None, :] + (ck - tq))
    bias = jnp.asarray(-cvals[:, None, None] * dist[None], jnp.float32)
    lam_p = jnp.stack([lq1, lk1, lq2, lk2]).astype(jnp.float32)

    full = pl.BlockSpec((None, nh, nt, 1, dh2), lambda i: (i, 0, 0, 0, 0))
    shift = pl.pallas_call(
        _shift_kernel,
        grid=(b,),
        in_specs=[full, full],
        out_specs=full,
        out_shape=jax.ShapeDtypeStruct((b, nh, nt, 1, dh2), jnp.float32),
        name="softmax_shift",
    )(qn, kn)
    shift = shift[..., 0, 0].reshape(-1)
    reach = jnp.asarray(np.minimum(np.floor(SKIP_EXP2_ZERO / cvals) + 1, s), jnp.int32)

    o = pl.pallas_call(
        functools.partial(_attn_kernel, s=s, dv=dv, lam_init=lam_init),
        grid=(b, nh, s // tq),
        in_specs=[
            pl.BlockSpec(memory_space=pltpu.SMEM),
            pl.BlockSpec(memory_space=pltpu.SMEM),
            pl.BlockSpec((None, None, 2, tq, dh2), lambda i, h, j: (i, h, 0, j, 0)),
            pl.BlockSpec((None, None, 2, s, dh2), lambda i, h, j: (i, h, 0, 0, 0)),
            pl.BlockSpec((None, None, s, 2 * dv), lambda i, h, j: (i, h, 0, 0)),
            pl.BlockSpec((None, tq, wdt), lambda i, h, j: (h, 0, 0)),
            pl.BlockSpec((4, dh), lambda i, h, j: (0, 0)),
            pl.BlockSpec((1, dv), lambda i, h, j: (0, 0)),
        ],
        out_specs=pl.BlockSpec((None, tq, dv), lambda i, h, j: (i, j, h)),
        out_shape=jax.ShapeDtypeStruct((b, s, nh * dv), jnp.bfloat16),
        scratch_shapes=[
            pltpu.VMEM((2, tq, 1), jnp.float32),
            pltpu.VMEM((2, tq, 2 * dv), jnp.float32),
        ],
        compiler_params=pltpu.CompilerParams(
            dimension_semantics=("parallel", "parallel", "arbitrary"),
            vmem_limit_bytes=VMEM_LIMIT_BYTES),
        name="diff_attention",
    )(reach, shift, q, k, v, bias, lam_p, subln_g.reshape(1, dv))
    return o


def kernel(x, norm_mix_pre, norm_mix_post, norm_ffn_pre, norm_ffn_post, fourier_w_o, diff_w_qkv,
           diff_lambda_q1, diff_lambda_k1, diff_lambda_q2, diff_lambda_k2, diff_subln_g, diff_w_o,
           ffn_w_gate, ffn_w_up, ffn_w_down):
    depth = norm_mix_pre.shape[0]
    h = x
    for i in range(depth):
        j = i // N_MIXERS
        mixer = None
        if i % N_MIXERS == 0:
            h = _fourier_layer(h, norm_mix_pre[i], norm_mix_post[i], fourier_w_o[j])
        else:
            a = _diff_attention_mixer(h, norm_mix_pre[i], diff_w_qkv[j], diff_lambda_q1[j],
                                      diff_lambda_k1[j], diff_lambda_q2[j], diff_lambda_k2[j],
                                      diff_subln_g[j], i)
            mixer = (a, diff_w_o[j], norm_mix_post[i])
        h = _ffn_layer(h, norm_ffn_pre[i], norm_ffn_post[i], ffn_w_gate[i], ffn_w_up[i], ffn_w_down[i],
                       mixer=mixer)
    return h
```

```python
import functools
import math

import numpy as np
import jax
import jax.numpy as jnp
from jax import lax
from jax.experimental import pallas as pl
from jax.experimental.pallas import tpu as pltpu

N_FOURIER_GROUPS = 8
N_DIFF_HEADS = 8
RMS_EPS = 1e-6
N_MIXERS = 2

LANES = 128
VMEM_LIMIT_BYTES = 56 * 1024 * 1024

FFT_N2 = 128
FFT_S2_PER_STEP = 16
FFT_K1_PER_STEP = 8
FFN_TOKEN_TILE = 1024
FFN_CHUNK = 256
ATT_TQ = 512
ATT_CK = 2048
ATT_WINDOW_HALVES = (1, 2, 3, 4)
POS_SPLIT = 256
LOG2E = 1.4426950408889634
SKIP_EXP2_ZERO = 153.0
SKIP_SLACK = 1.01
SHIFT_MAX_LOG2_RANGE = 100.0


def _rms_scale(x):
    return lax.rsqrt(jnp.mean(x * x, axis=-1, keepdims=True) + RMS_EPS)


def _np_bf16(v):
    return np.asarray(v, np.float32).astype(jnp.bfloat16)


def _bf16_split3(v):
    v = np.asarray(v, np.float64)
    hi = _np_bf16(v).astype(np.float64)
    mid = _np_bf16(v - hi).astype(np.float64)
    lo = _np_bf16(v - hi - mid).astype(np.float64)
    return hi, mid, lo


def _fft_stage1_kernel(x_ref, g_ref, f1_ref, ar_ref, ai_ref, *, n1, d, nb):
    x = pltpu.einshape("sjd->jsd", x_ref[...]).reshape(nb * n1, d)
    xn = (x * _rms_scale(x) * g_ref[...]).astype(jnp.bfloat16)
    f1 = f1_ref[...]
    a = jnp.stack([jnp.dot(f1, xn[j * n1:(j + 1) * n1], preferred_element_type=jnp.float32)
                   for j in range(nb)], axis=0)
    a = pltpu.einshape("jkd->kjd", a)
    ar_ref[...] = a[:n1].astype(jnp.bfloat16)
    ai_ref[...] = a[n1:].astype(jnp.bfloat16)


def _fft_stage2_kernel(ar_ref, ai_ref, gm_ref, cs_ref, wo_ref, gpost_ref, xres_ref, out_ref,
                       *, n2, d, cg, kb):
    cs = cs_ref[...]
    zs = []
    for i in range(kb):
        a = jnp.concatenate([ar_ref[i], ai_ref[i]], axis=0)
        y = jnp.dot(gm_ref[i], a, preferred_element_type=jnp.float32)
        yr = y[:n2].astype(jnp.bfloat16)
        yi = y[n2:].astype(jnp.bfloat16)
        cols = []
        for gi in range(d // cg):
            lhs = jnp.concatenate([yr[:, gi * cg:(gi + 1) * cg], yi[:, gi * cg:(gi + 1) * cg]], axis=1)
            cols.append(jnp.dot(lhs, cs, preferred_element_type=jnp.float32))
        zs.append(jnp.concatenate(cols, axis=1).astype(jnp.bfloat16))
    z = jnp.concatenate(zs, axis=0)
    m = jnp.dot(z, wo_ref[...], preferred_element_type=jnp.float32)
    r = m * _rms_scale(m) * gpost_ref[...]
    out_ref[...] = xres_ref[...] + pltpu.einshape("ikd->kid", r.reshape(kb, n2, d))


def _fourier_layer(x, g_pre, g_post, w_o):
    b, s, d = x.shape
    n2 = FFT_N2
    n1 = s // n2
    cg = d // N_FOURIER_GROUPS
    nb = FFT_S2_PER_STEP
    assert n1 * n2 == s and n2 % nb == 0

    s1 = np.arange(n1)
    th1 = 2.0 * np.pi * ((s1[:, None] * s1[None, :]) % n1) / n1
    f1 = np.concatenate([np.cos(th1), -np.sin(th1)], axis=0)
    s2 = np.arange(n2)
    kk = (np.arange(n1)[:, None, None] + n1 * np.arange(n2)[None, :, None])
    th2 = 2.0 * np.pi * ((kk * s2[None, None, :]) % s) / s
    gr, gi = np.cos(th2), -np.sin(th2)
    gm = np.concatenate([np.concatenate([gr, -gi], axis=2),
                         np.concatenate([gi, gr], axis=2)], axis=1)
    c = np.arange(cg)
    thc = 2.0 * np.pi * ((c[:, None] * c[None, :]) % cg) / cg
    norm = 1.0 / math.sqrt(s * cg)
    cs = np.concatenate([np.cos(thc), np.sin(thc)], axis=0) * norm

    f1 = jnp.asarray(_np_bf16(f1))
    gm = jnp.asarray(_np_bf16(gm))
    cs = jnp.asarray(_np_bf16(cs))

    x4 = x.reshape(b, n1, n2, d)
    ar, ai = pl.pallas_call(
        functools.partial(_fft_stage1_kernel, n1=n1, d=d, nb=nb),
        grid=(b, n2 // nb),
        in_specs=[
            pl.BlockSpec((None, n1, nb, d), lambda i, j: (i, 0, j, 0)),
            pl.BlockSpec((1, d), lambda i, j: (0, 0)),
            pl.BlockSpec((2 * n1, n1), lambda i, j: (0, 0)),
        ],
        out_specs=[
            pl.BlockSpec((None, n1, nb, d), lambda i, j: (i, 0, j, 0)),
            pl.BlockSpec((None, n1, nb, d), lambda i, j: (i, 0, j, 0)),
        ],
        out_shape=[jax.ShapeDtypeStruct((b, n1, n2, d), jnp.bfloat16)] * 2,
        compiler_params=pltpu.CompilerParams(
            dimension_semantics=("parallel", "parallel"), vmem_limit_bytes=VMEM_LIMIT_BYTES),
        name="fft_stage1",
    )(x4, g_pre.reshape(1, d), f1)

    kb = min(FFT_K1_PER_STEP, n1)
    assert n1 % kb == 0
    xk = x.reshape(b, n2, n1, d)
    out = pl.pallas_call(
        functools.partial(_fft_stage2_kernel, n2=n2, d=d, cg=cg, kb=kb),
        grid=(b, n1 // kb),
        in_specs=[
            pl.BlockSpec((None, kb, n2, d), lambda i, j: (i, j, 0, 0)),
            pl.BlockSpec((None, kb, n2, d), lambda i, j: (i, j, 0, 0)),
            pl.BlockSpec((kb, 2 * n2, 2 * n2), lambda i, j: (j, 0, 0)),
            pl.BlockSpec((2 * cg, cg), lambda i, j: (0, 0)),
            pl.BlockSpec((d, d), lambda i, j: (0, 0)),
            pl.BlockSpec((1, d), lambda i, j: (0, 0)),
            pl.BlockSpec((None, n2, kb, d), lambda i, j: (i, 0, j, 0)),
        ],
        out_specs=pl.BlockSpec((None, n2, kb, d), lambda i, j: (i, 0, j, 0)),
        out_shape=jax.ShapeDtypeStruct((b, n2, n1, d), jnp.float32),
        compiler_params=pltpu.CompilerParams(
            dimension_semantics=("parallel", "parallel"), vmem_limit_bytes=VMEM_LIMIT_BYTES),
        name="fft_stage2",
    )(ar, ai, gm, cs, w_o.astype(jnp.bfloat16), g_post.reshape(1, d), xk)
    return out.reshape(b, s, d)


def _ffn_kernel(*refs, f, chunk, mixer_proj):
    if mixer_proj:
        a_ref, wo_ref, gmix_ref, x_ref, gpre_ref, gpost_ref, wg_ref, wu_ref, wd_ref, out_ref = refs
        m = jnp.dot(a_ref[...], wo_ref[...], preferred_element_type=jnp.float32)
        x = x_ref[...] + m * _rms_scale(m) * gmix_ref[...]
    else:
        x_ref, gpre_ref, gpost_ref, wg_ref, wu_ref, wd_ref, out_ref = refs
        x = x_ref[...]
    xn = (x * _rms_scale(x) * gpre_ref[...]).astype(jnp.bfloat16)
    acc = jnp.zeros(x.shape, jnp.float32)
    for c0 in range(0, f, chunk):
        c1 = min(c0 + chunk, f)
        gt = jnp.dot(xn, wg_ref[:, c0:c1], preferred_element_type=jnp.float32)
        up = jnp.dot(xn, wu_ref[:, c0:c1], preferred_element_type=jnp.float32)
        hh = (gt * jax.nn.sigmoid(gt) * up).astype(jnp.bfloat16)
        acc = acc + jnp.dot(hh, wd_ref[c0:c1, :], preferred_element_type=jnp.float32)
    out_ref[...] = x + acc * _rms_scale(acc) * gpost_ref[...]


def _ffn_layer(h, g_pre, g_post, w_gate, w_up, w_down, mixer=None):
    b, s, d = h.shape
    f = w_gate.shape[1]
    t = b * s
    tm = min(FFN_TOKEN_TILE, t)
    assert t % tm == 0
    const = lambda i: (0, 0)
    rows = pl.BlockSpec((tm, d), lambda i: (i, 0))
    vec = pl.BlockSpec((1, d), const)
    in_specs = [rows, vec, vec,
                pl.BlockSpec((d, f), const, pipeline_mode=pl.Buffered(1)),
                pl.BlockSpec((d, f), const, pipeline_mode=pl.Buffered(1)),
                pl.BlockSpec((f, d), const, pipeline_mode=pl.Buffered(1))]
    args = [h.reshape(t, d), g_pre.reshape(1, d), g_post.reshape(1, d),
            w_gate.astype(jnp.bfloat16), w_up.astype(jnp.bfloat16), w_down.astype(jnp.bfloat16)]
    if mixer is not None:
        a, w_o, g_mix = mixer
        in_specs = [rows, pl.BlockSpec((d, d), const, pipeline_mode=pl.Buffered(1)), vec] + in_specs
        args = [a.reshape(t, d), w_o.astype(jnp.bfloat16), g_mix.reshape(1, d)] + args
    out = pl.pallas_call(
        functools.partial(_ffn_kernel, f=f, chunk=FFN_CHUNK, mixer_proj=mixer is not None),
        grid=(t // tm,),
        in_specs=in_specs,
        out_specs=rows,
        out_shape=jax.ShapeDtypeStruct((t, d), jnp.float32),
        compiler_params=pltpu.CompilerParams(
            dimension_semantics=("parallel",), vmem_limit_bytes=VMEM_LIMIT_BYTES),
        name="swiglu_ffn",
    )(*args)
    return out.reshape(b, s, d)


def _qkv_kernel(x_ref, gpre_ref, w_ref, qa_ref, ka_ref, q_ref, k_ref, v_ref, qn_ref, kn_ref,
                *, d, nh, tm, qscale):
    x = x_ref[...]
    xn = (x * _rms_scale(x) * gpre_ref[...]).astype(jnp.bfloat16)
    dh2 = d // nh
    row = pl.program_id(1) * tm + lax.broadcasted_iota(jnp.int32, (tm, 1), 0)
    pos_lo = (row & (POS_SPLIT - 1)).astype(jnp.float32)
    pos_hi = (row & (-POS_SPLIT)).astype(jnp.float32)
    lane = lax.broadcasted_iota(jnp.int32, (1, dh2), 1)
    half = dh2 // 2
    q_all = jnp.dot(xn, w_ref[:, 0:d], preferred_element_type=jnp.float32) * qscale
    k_all = jnp.dot(xn, w_ref[:, d:2 * d], preferred_element_type=jnp.float32)
    v_all = jnp.dot(xn, w_ref[:, 2 * d:3 * d], preferred_element_type=jnp.float32)
    ones_col = jnp.where(lane == 0, 1.0, 0.0).astype(jnp.bfloat16)
    lane_col = lax.broadcasted_iota(jnp.int32, (dh2, 1), 0)
    comp_sum = jnp.where((lane_col >= half).astype(jnp.int32) == lane, 1.0, 0.0).astype(jnp.bfloat16)
    for h in range(nh):
        qh = q_all[:, h * dh2:(h + 1) * dh2]
        kh = k_all[:, h * dh2:(h + 1) * dh2]
        for src, dst in ((qh, qn_ref), (kh, kn_ref)):
            vb = src.astype(jnp.bfloat16).astype(jnp.float32)
            sq = jnp.dot((vb * vb).astype(jnp.bfloat16), comp_sum, preferred_element_type=jnp.float32)
            dst[h] = jnp.max(sq, axis=0, keepdims=True)
        for comp in range(2):
            data = (lane < half) if comp == 0 else (lane >= half)
            qa = qa_ref[h, comp]
            ka = ka_ref[h, comp]
            q_aug = qa[0:1] + qa[1:2] * pos_lo + qa[2:3] * pos_hi
            k_aug = ka[0:1] + ka[1:2] * pos_lo + ka[2:3] * pos_hi
            q_ref[h, comp] = jnp.where(data, qh, q_aug).astype(jnp.bfloat16)
            k_ref[h, comp] = jnp.where(data, kh, k_aug).astype(jnp.bfloat16)
        vh = v_all[:, h * dh2:(h + 1) * dh2].astype(jnp.bfloat16)
        v_ref[h] = jnp.concatenate([vh, jnp.broadcast_to(ones_col, (tm, dh2))], axis=1)


def _aug_tables(nh, dh2, cvals):
    half = dh2 // 2
    qa = np.zeros((nh, 2, 3, dh2), np.float64)
    ka = np.zeros((nh, 2, 3, dh2), np.float64)
    for h in range(nh):
        pieces = _bf16_split3(cvals[h])
        for comp in range(2):
            base = half if comp == 0 else 0
            for p in range(3):
                for part in (1, 2):
                    qa[h, comp, 0, base + 3 * (part - 1) + p] = pieces[p]
                    ka[h, comp, part, base + 3 * (part - 1) + p] = 1.0
                    qa[h, comp, part, base + 6 + 3 * (part - 1) + p] = -1.0
                    ka[h, comp, 0, base + 6 + 3 * (part - 1) + p] = pieces[p]
    return jnp.asarray(qa, jnp.float32), jnp.asarray(ka, jnp.float32)


def _shift_kernel(qn_ref, kn_ref, shift_ref):
    kmax = jnp.max(kn_ref[...], axis=1, keepdims=True)
    prod = qn_ref[...] * kmax
    u = jnp.sqrt(jnp.maximum(prod[..., 0:1], prod[..., 1:2]))
    shift = jnp.where(2.0 * SKIP_SLACK * u <= SHIFT_MAX_LOG2_RANGE, SKIP_SLACK * u, -1.0)
    shift_ref[...] = jnp.broadcast_to(shift, shift_ref.shape)


def _attn_kernel(reach_ref, shift_ref, q_ref, k_ref, v_ref, bias_ref, lam_ref, g_ref, o_ref,
                 m_scr, acc_scr, *, s, dv, lam_init):
    tq, ck = ATT_TQ, ATT_CK
    nch = s // ck
    hd = pl.program_id(1)
    qi = pl.program_id(2)
    i0 = qi * tq
    lane = lax.broadcasted_iota(jnp.int32, (1, LANES), 1)
    half = LANES // 2

    q_left, q_right, q_diag = [], [], []
    for comp in range(2):
        data = (lane < half) if comp == 0 else (lane >= half)
        qc = q_ref[comp]
        q_left.append(qc)
        q_right.append(jnp.where(data, qc, -qc))
        q_diag.append(jnp.where(data, qc, jnp.zeros_like(qc)))

    def scores(j0, table_off, nk):
        j0 = pl.multiple_of(j0, POS_SPLIT)
        ts = []
        for comp in range(2):
            kc = k_ref[comp, pl.ds(j0, nk), :]
            if table_off is None:
                qv = jnp.where(j0 < i0, q_left[comp], q_right[comp])
            else:
                qv = q_diag[comp]
            t = lax.dot_general(qv, kc, (((1,), (1,)), ((), ())),
                                preferred_element_type=jnp.float32)
            if table_off is not None:
                t = t + bias_ref[:, pl.ds(pl.multiple_of(table_off, POS_SPLIT), nk)]
            ts.append(t)
        return j0, ts, nk

    def accumulate(j0, ts, nk, first):
        vc = v_ref[pl.ds(j0, nk), :]
        for comp in range(2):
            p = jnp.exp2(ts[comp] - shift).astype(jnp.bfloat16)
            pv = jnp.dot(p, vc, preferred_element_type=jnp.float32)
            acc_scr[comp] = pv if first else acc_scr[comp] + pv

    def accumulate_online(j0, ts, nk, first):
        vc = v_ref[pl.ds(j0, nk), :]
        for comp in range(2):
            t = ts[comp]
            m_new = jnp.max(t, axis=-1, keepdims=True)
            if not first:
                m_old = m_scr[comp]
                m_new = jnp.maximum(m_old, m_new)
            p = jnp.exp2(t - m_new).astype(jnp.bfloat16)
            pv = jnp.dot(p, vc, preferred_element_type=jnp.float32)
            if first:
                acc_scr[comp] = pv
            else:
                acc_scr[comp] = jnp.exp2(m_old - m_new) * acc_scr[comp] + pv
            m_scr[comp] = m_new

    tile = (pl.program_id(0) * pl.num_programs(1) + hd) * pl.num_programs(2) + qi
    shift = shift_ref[tile]

    reach = reach_ref[hd]
    lo_key = jnp.maximum(i0 - reach, 0)
    hi_key = jnp.minimum(i0 + tq + reach, s)
    start0 = lo_key & (-POS_SPLIT)
    start0 = jnp.where(((i0 - start0) & (ck - 1)) + tq > ck, start0 - POS_SPLIT, start0)
    hk = ck // 2
    halves_need = (hi_key - start0 + (hk - 1)) // hk

    def window_start(halves):
        return jnp.minimum(start0, s - halves * hk)

    def sweep(halves):
        whole, tail = halves // 2, halves % 2

        def run():
            start = window_start(halves)
            rel = i0 - start
            if whole == 0:
                accumulate(*scores(start, (ck - tq) - rel, hk), first=True)
                return
            own = rel // ck
            pending = scores(start + own * ck, (ck - tq) - (rel & (ck - 1)), ck)
            for n in range(1, whole):
                r = own + n
                nxt = scores(start + jnp.where(r >= whole, r - whole, r) * ck, None, ck)
                accumulate(*pending, first=(n == 1))
                pending = nxt
            if tail:
                nxt = scores(start + whole * ck, None, hk)
                accumulate(*pending, first=(whole == 1))
                pending = nxt
            accumulate(*pending, first=(whole == 1 and not tail))
        return run

    sizes = [h for h in ATT_WINDOW_HALVES if h < 2 * nch] + [2 * nch]

    def dispatch(idx=0):
        halves = sizes[idx]
        if idx == len(sizes) - 1:
            sweep(halves)()
            return
        ok = halves_need <= halves
        if halves % 2:
            ok = ok & (i0 - window_start(halves) + tq <= max(halves // 2 * ck, hk))
        lax.cond(ok, sweep(halves), lambda: dispatch(idx + 1))

    def online_sweep():
        own = i0 // ck
        accumulate_online(*scores(own * ck, (ck - tq) - (i0 & (ck - 1)), ck), first=True)

        def body(n, carry):
            r = own + n
            accumulate_online(*scores(jnp.where(r >= nch, r - nch, r) * ck, None, ck), first=False)
            return carry

        lax.fori_loop(1, nch, body, 0)

    lax.cond(shift >= 0.0, dispatch, online_sweep)

    lam_p = lam_ref[...]
    lam = (jnp.exp(jnp.sum(lam_p[0:1] * lam_p[1:2], axis=-1, keepdims=True))
           - jnp.exp(jnp.sum(lam_p[2:3] * lam_p[3:4], axis=-1, keepdims=True)) + lam_init)
    a0 = acc_scr[0]
    a1 = acc_scr[1]
    o = a0[:, :dv] / a0[:, dv:dv + 1] - lam * (a1[:, :dv] / a1[:, dv:dv + 1])
    o = o * _rms_scale(o) * g_ref[...] * (1.0 - lam_init)
    o_ref[...] = o.astype(jnp.bfloat16)


def _diff_attention_mixer(hin, g_pre, w_qkv, lq1, lk1, lq2, lk2, subln_g, layer_idx):
    b, s, d = hin.shape
    nh = N_DIFF_HEADS
    dh2 = d // nh
    dh = dh2 // 2
    dv = dh2
    assert dh2 == LANES and s % ATT_CK == 0
    assert s <= POS_SPLIT * 256
    lam_init = 0.8 - 0.6 * math.exp(-0.3 * layer_idx)
    slopes = np.asarray([2.0 ** (-8.0 * (i + 1) / nh) for i in range(nh)], np.float64)
    cvals = slopes * LOG2E
    qa, ka = _aug_tables(nh, dh2, cvals)
    qscale = dh ** -0.5 * LOG2E

    tm = ATT_TQ
    const2 = lambda i, j: (0, 0)
    nt = s // tm
    norm_spec = pl.BlockSpec((None, nh, None, 1, dh2), lambda i, j: (i, 0, j, 0, 0))
    norm_shape = jax.ShapeDtypeStruct((b, nh, nt, 1, dh2), jnp.float32)
    q, k, v, qn, kn = pl.pallas_call(
        functools.partial(_qkv_kernel, d=d, nh=nh, tm=tm, qscale=qscale),
        grid=(b, s // tm),
        in_specs=[
            pl.BlockSpec((None, tm, d), lambda i, j: (i, j, 0)),
            pl.BlockSpec((1, d), const2),
            pl.BlockSpec((d, 3 * d), const2, pipeline_mode=pl.Buffered(1)),
            pl.BlockSpec((nh, 2, 3, dh2), lambda i, j: (0, 0, 0, 0)),
            pl.BlockSpec((nh, 2, 3, dh2), lambda i, j: (0, 0, 0, 0)),
        ],
        out_specs=[
            pl.BlockSpec((None, nh, 2, tm, dh2), lambda i, j: (i, 0, 0, j, 0)),
            pl.BlockSpec((None, nh, 2, tm, dh2), lambda i, j: (i, 0, 0, j, 0)),
            pl.BlockSpec((None, nh, tm, 2 * dv), lambda i, j: (i, 0, j, 0)),
            norm_spec,
            norm_spec,
        ],
        out_shape=[
            jax.ShapeDtypeStruct((b, nh, 2, s, dh2), jnp.bfloat16),
            jax.ShapeDtypeStruct((b, nh, 2, s, dh2), jnp.bfloat16),
            jax.ShapeDtypeStruct((b, nh, s, 2 * dv), jnp.bfloat16),
            norm_shape,
            norm_shape,
        ],
        compiler_params=pltpu.CompilerParams(
            dimension_semantics=("parallel", "parallel"), vmem_limit_bytes=VMEM_LIMIT_BYTES),
        name="qkv_proj",
    )(hin, g_pre.reshape(1, d), w_qkv.astype(jnp.bfloat16), qa, ka)

    tq, ck = ATT_TQ, ATT_CK
    wdt = ck + (ck - tq)
    dist = np.abs(np.arange(tq)[:, None] - np.arange(wdt)[None, :] + (ck - tq))
    bias = jnp.asarray(-cvals[:, None, None] * dist[None], jnp.float32)
    lam_p = jnp.stack([lq1, lk1, lq2, lk2]).astype(jnp.float32)

    full = pl.BlockSpec((None, nh, nt, 1, dh2), lambda i: (i, 0, 0, 0, 0))
    shift = pl.pallas_call(
        _shift_kernel,
        grid=(b,),
        in_specs=[full, full],
        out_specs=full,
        out_shape=jax.ShapeDtypeStruct((b, nh, nt, 1, dh2), jnp.float32),
        name="softmax_shift",
    )(qn, kn)
    shift = shift[..., 0, 0].reshape(-1)
    reach = jnp.asarray(np.minimum(np.floor(SKIP_EXP2_ZERO / cvals) + 1, s), jnp.int32)

    o = pl.pallas_call(
        functools.partial(_attn_kernel, s=s, dv=dv, lam_init=lam_init),
        grid=(b, nh, s // tq),
        in_specs=[
            pl.BlockSpec(memory_space=pltpu.SMEM),
            pl.BlockSpec(memory_space=pltpu.SMEM),
            pl.BlockSpec((None, None, 2, tq, dh2), lambda i, h, j: (i, h, 0, j, 0)),
            pl.BlockSpec((None, None, 2, s, dh2), lambda i, h, j: (i, h, 0, 0, 0)),
            pl.BlockSpec((None, None, s, 2 * dv), lambda i, h, j: (i, h, 0, 0)),
            pl.BlockSpec((None, tq, wdt), lambda i, h, j: (h, 0, 0)),
            pl.BlockSpec((4, dh), lambda i, h, j: (0, 0)),
            pl.BlockSpec((1, dv), lambda i, h, j: (0, 0)),
        ],
        out_specs=pl.BlockSpec((None, tq, dv), lambda i, h, j: (i, j, h)),
        out_shape=jax.ShapeDtypeStruct((b, s, nh * dv), jnp.bfloat16),
        scratch_shapes=[
            pltpu.VMEM((2, tq, 1), jnp.float32),
            pltpu.VMEM((2, tq, 2 * dv), jnp.float32),
        ],
        compiler_params=pltpu.CompilerParams(
            dimension_semantics=("parallel", "parallel", "arbitrary"),
            vmem_limit_bytes=VMEM_LIMIT_BYTES),
        name="diff_attention",
    )(reach, shift, q, k, v, bias, lam_p, subln_g.reshape(1, dv))
    return o


def kernel(x, norm_mix_pre, norm_mix_post, norm_ffn_pre, norm_ffn_post, fourier_w_o, diff_w_qkv,
           diff_lambda_q1, diff_lambda_k1, diff_lambda_q2, diff_lambda_k2, diff_subln_g, diff_w_o,
           ffn_w_gate, ffn_w_up, ffn_w_down):
    depth = norm_mix_pre.shape[0]
    h = x
    for i in range(depth):
        j = i // N_MIXERS
        mixer = None
        if i % N_MIXERS == 0:
            h = _fourier_layer(h, norm_mix_pre[i], norm_mix_post[i], fourier_w_o[j])
        else:
            a = _diff_attention_mixer(h, norm_mix_pre[i], diff_w_qkv[j], diff_lambda_q1[j],
                                      diff_lambda_k1[j], diff_lambda_q2[j], diff_lambda_k2[j],
                                      diff_subln_g[j], i)
            mixer = (a, diff_w_o[j], norm_mix_post[i])
        h = _ffn_layer(h, norm_ffn_pre[i], norm_ffn_post[i], ffn_w_gate[i], ffn_w_up[i], ffn_w_down[i],
                       mixer=mixer)
    return h
```

```python
import functools
import math

import numpy as np
import jax
import jax.numpy as jnp
from jax import lax
from jax.experimental import pallas as pl
from jax.experimental.pallas import tpu as pltpu

N_FOURIER_GROUPS = 8
N_DIFF_HEADS = 8
RMS_EPS = 1e-6
N_MIXERS = 2

LANES = 128
VMEM_LIMIT_BYTES = 56 * 1024 * 1024

FFT_N2 = 128
FFT_S2_PER_STEP = 16
FFT_K1_PER_STEP = 8
FFN_TOKEN_TILE = 1024
FFN_CHUNK = 256
ATT_TQ = 512
ATT_TILES_PER_STEP = 2
ATT_CK = 2048
POS_SPLIT = 256
LOG2E = 1.4426950408889634
SKIP_EXP2_ZERO = 153.0
SKIP_SLACK = 1.01
SHIFT_MAX_LOG2_RANGE = 100.0


def _rms_scale(x):
    return lax.rsqrt(jnp.mean(x * x, axis=-1, keepdims=True) + RMS_EPS)


def _np_bf16(v):
    return np.asarray(v, np.float32).astype(jnp.bfloat16)


def _bf16_split3(v):
    v = np.asarray(v, np.float64)
    hi = _np_bf16(v).astype(np.float64)
    mid = _np_bf16(v - hi).astype(np.float64)
    lo = _np_bf16(v - hi - mid).astype(np.float64)
    return hi, mid, lo


def _fft_stage1_kernel(x_ref, g_ref, f1_ref, ar_ref, ai_ref, *, n1, d, nb):
    x = pltpu.einshape("sjd->jsd", x_ref[...]).reshape(nb * n1, d)
    xn = (x * _rms_scale(x) * g_ref[...]).astype(jnp.bfloat16)
    f1 = f1_ref[...]
    a = jnp.stack([jnp.dot(f1, xn[j * n1:(j + 1) * n1], preferred_element_type=jnp.float32)
                   for j in range(nb)], axis=0)
    a = pltpu.einshape("jkd->kjd", a)
    ar_ref[...] = a[:n1].astype(jnp.bfloat16)
    ai_ref[...] = a[n1:].astype(jnp.bfloat16)


def _fft_stage2_kernel(ar_ref, ai_ref, gm_ref, cs_ref, wo_ref, gpost_ref, xres_ref, out_ref,
                       *, n2, d, cg, kb):
    cs = cs_ref[...]
    zs = []
    for i in range(kb):
        a = jnp.concatenate([ar_ref[i], ai_ref[i]], axis=0)
        y = jnp.dot(gm_ref[i], a, preferred_element_type=jnp.float32)
        yr = y[:n2].astype(jnp.bfloat16)
        yi = y[n2:].astype(jnp.bfloat16)
        cols = []
        for gi in range(d // cg):
            lhs = jnp.concatenate([yr[:, gi * cg:(gi + 1) * cg], yi[:, gi * cg:(gi + 1) * cg]], axis=1)
            cols.append(jnp.dot(lhs, cs, preferred_element_type=jnp.float32))
        zs.append(jnp.concatenate(cols, axis=1).astype(jnp.bfloat16))
    z = jnp.concatenate(zs, axis=0)
    m = jnp.dot(z, wo_ref[...], preferred_element_type=jnp.float32)
    r = m * _rms_scale(m) * gpost_ref[...]
    out_ref[...] = xres_ref[...] + pltpu.einshape("ikd->kid", r.reshape(kb, n2, d))


def _fourier_layer(x, g_pre, g_post, w_o):
    b, s, d = x.shape
    n2 = FFT_N2
    n1 = s // n2
    cg = d // N_FOURIER_GROUPS
    nb = FFT_S2_PER_STEP
    assert n1 * n2 == s and n2 % nb == 0

    s1 = np.arange(n1)
    th1 = 2.0 * np.pi * ((s1[:, None] * s1[None, :]) % n1) / n1
    f1 = np.concatenate([np.cos(th1), -np.sin(th1)], axis=0)
    s2 = np.arange(n2)
    kk = (np.arange(n1)[:, None, None] + n1 * np.arange(n2)[None, :, None])
    th2 = 2.0 * np.pi * ((kk * s2[None, None, :]) % s) / s
    gr, gi = np.cos(th2), -np.sin(th2)
    gm = np.concatenate([np.concatenate([gr, -gi], axis=2),
                         np.concatenate([gi, gr], axis=2)], axis=1)
    c = np.arange(cg)
    thc = 2.0 * np.pi * ((c[:, None] * c[None, :]) % cg) / cg
    norm = 1.0 / math.sqrt(s * cg)
    cs = np.concatenate([np.cos(thc), np.sin(thc)], axis=0) * norm

    f1 = jnp.asarray(_np_bf16(f1))
    gm = jnp.asarray(_np_bf16(gm))
    cs = jnp.asarray(_np_bf16(cs))

    x4 = x.reshape(b, n1, n2, d)
    ar, ai = pl.pallas_call(
        functools.partial(_fft_stage1_kernel, n1=n1, d=d, nb=nb),
        grid=(b, n2 // nb),
        in_specs=[
            pl.BlockSpec((None, n1, nb, d), lambda i, j: (i, 0, j, 0)),
            pl.BlockSpec((1, d), lambda i, j: (0, 0)),
            pl.BlockSpec((2 * n1, n1), lambda i, j: (0, 0)),
        ],
        out_specs=[
            pl.BlockSpec((None, n1, nb, d), lambda i, j: (i, 0, j, 0)),
            pl.BlockSpec((None, n1, nb, d), lambda i, j: (i, 0, j, 0)),
        ],
        out_shape=[jax.ShapeDtypeStruct((b, n1, n2, d), jnp.bfloat16)] * 2,
        compiler_params=pltpu.CompilerParams(
            dimension_semantics=("parallel", "parallel"), vmem_limit_bytes=VMEM_LIMIT_BYTES),
        name="fft_stage1",
    )(x4, g_pre.reshape(1, d), f1)

    kb = min(FFT_K1_PER_STEP, n1)
    assert n1 % kb == 0
    xk = x.reshape(b, n2, n1, d)
    out = pl.pallas_call(
        functools.partial(_fft_stage2_kernel, n2=n2, d=d, cg=cg, kb=kb),
        grid=(b, n1 // kb),
        in_specs=[
            pl.BlockSpec((None, kb, n2, d), lambda i, j: (i, j, 0, 0)),
            pl.BlockSpec((None, kb, n2, d), lambda i, j: (i, j, 0, 0)),
            pl.BlockSpec((kb, 2 * n2, 2 * n2), lambda i, j: (j, 0, 0)),
            pl.BlockSpec((2 * cg, cg), lambda i, j: (0, 0)),
            pl.BlockSpec((d, d), lambda i, j: (0, 0)),
            pl.BlockSpec((1, d), lambda i, j: (0, 0)),
            pl.BlockSpec((None, n2, kb, d), lambda i, j: (i, 0, j, 0)),
        ],
        out_specs=pl.BlockSpec((None, n2, kb, d), lambda i, j: (i, 0, j, 0)),
        out_shape=jax.ShapeDtypeStruct((b, n2, n1, d), jnp.float32),
        compiler_params=pltpu.CompilerParams(
            dimension_semantics=("parallel", "parallel"), vmem_limit_bytes=VMEM_LIMIT_BYTES),
        name="fft_stage2",
    )(ar, ai, gm, cs, w_o.astype(jnp.bfloat16), g_post.reshape(1, d), xk)
    return out.reshape(b, s, d)


def _ffn_kernel(*refs, f, chunk, mixer_proj):
    if mixer_proj:
        a_ref, wo_ref, gmix_ref, x_ref, gpre_ref, gpost_ref, wg_ref, wu_ref, wd_ref, out_ref = refs
        m = jnp.dot(a_ref[...], wo_ref[...], preferred_element_type=jnp.float32)
        x = x_ref[...] + m * _rms_scale(m) * gmix_ref[...]
    else:
        x_ref, gpre_ref, gpost_ref, wg_ref, wu_ref, wd_ref, out_ref = refs
        x = x_ref[...]
    xn = (x * _rms_scale(x) * gpre_ref[...]).astype(jnp.bfloat16)
    acc = jnp.zeros(x.shape, jnp.float32)
    for c0 in range(0, f, chunk):
        c1 = min(c0 + chunk, f)
        gt = jnp.dot(xn, wg_ref[:, c0:c1], preferred_element_type=jnp.float32)
        up = jnp.dot(xn, wu_ref[:, c0:c1], preferred_element_type=jnp.float32)
        hh = (gt * jax.nn.sigmoid(gt) * up).astype(jnp.bfloat16)
        acc = acc + jnp.dot(hh, wd_ref[c0:c1, :], preferred_element_type=jnp.float32)
    out_ref[...] = x + acc * _rms_scale(acc) * gpost_ref[...]


def _ffn_layer(h, g_pre, g_post, w_gate, w_up, w_down, mixer=None):
    b, s, d = h.shape
    f = w_gate.shape[1]
    t = b * s
    tm = min(FFN_TOKEN_TILE, t)
    assert t % tm == 0
    const = lambda i: (0, 0)
    rows = pl.BlockSpec((tm, d), lambda i: (i, 0))
    vec = pl.BlockSpec((1, d), const)
    in_specs = [rows, vec, vec,
                pl.BlockSpec((d, f), const, pipeline_mode=pl.Buffered(1)),
                pl.BlockSpec((d, f), const, pipeline_mode=pl.Buffered(1)),
                pl.BlockSpec((f, d), const, pipeline_mode=pl.Buffered(1))]
    args = [h.reshape(t, d), g_pre.reshape(1, d), g_post.reshape(1, d),
            w_gate.astype(jnp.bfloat16), w_up.astype(jnp.bfloat16), w_down.astype(jnp.bfloat16)]
    if mixer is not None:
        a, w_o, g_mix = mixer
        in_specs = [rows, pl.BlockSpec((d, d), const, pipeline_mode=pl.Buffered(1)), vec] + in_specs
        args = [a.reshape(t, d), w_o.astype(jnp.bfloat16), g_mix.reshape(1, d)] + args
    out = pl.pallas_call(
        functools.partial(_ffn_kernel, f=f, chunk=FFN_CHUNK, mixer_proj=mixer is not None),
        grid=(t // tm,),
        in_specs=in_specs,
        out_specs=rows,
        out_shape=jax.ShapeDtypeStruct((t, d), jnp.float32),
        compiler_params=pltpu.CompilerParams(
            dimension_semantics=("parallel",), vmem_limit_bytes=VMEM_LIMIT_BYTES),
        name="swiglu_ffn",
    )(*args)
    return out.reshape(b, s, d)


def _qkv_kernel(x_ref, gpre_ref, w_ref, qa_ref, ka_ref, q_ref, k_ref, v_ref, qn_ref, kn_ref,
                *, d, nh, tm, qscale):
    x = x_ref[...]
    xn = (x * _rms_scale(x) * gpre_ref[...]).astype(jnp.bfloat16)
    dh2 = d // nh
    row = pl.program_id(1) * tm + lax.broadcasted_iota(jnp.int32, (tm, 1), 0)
    pos_lo = (row & (POS_SPLIT - 1)).astype(jnp.float32)
    pos_hi = (row & (-POS_SPLIT)).astype(jnp.float32)
    lane = lax.broadcasted_iota(jnp.int32, (1, dh2), 1)
    half = dh2 // 2
    q_all = jnp.dot(xn, w_ref[:, 0:d], preferred_element_type=jnp.float32) * qscale
    k_all = jnp.dot(xn, w_ref[:, d:2 * d], preferred_element_type=jnp.float32)
    v_all = jnp.dot(xn, w_ref[:, 2 * d:3 * d], preferred_element_type=jnp.float32)
    ones_col = jnp.where(lane == 0, 1.0, 0.0).astype(jnp.bfloat16)
    lane_col = lax.broadcasted_iota(jnp.int32, (dh2, 1), 0)
    comp_sum = jnp.where((lane_col >= half).astype(jnp.int32) == lane, 1.0, 0.0).astype(jnp.bfloat16)
    for h in range(nh):
        qh = q_all[:, h * dh2:(h + 1) * dh2]
        kh = k_all[:, h * dh2:(h + 1) * dh2]
        for src, dst in ((qh, qn_ref), (kh, kn_ref)):
            vb = src.astype(jnp.bfloat16).astype(jnp.float32)
            sq = jnp.dot((vb * vb).astype(jnp.bfloat16), comp_sum, preferred_element_type=jnp.float32)
            dst[h] = jnp.max(sq, axis=0, keepdims=True)
        for comp in range(2):
            data = (lane < half) if comp == 0 else (lane >= half)
            qa = qa_ref[h, comp]
            ka = ka_ref[h, comp]
            q_aug = qa[0:1] + qa[1:2] * pos_lo + qa[2:3] * pos_hi
            k_aug = ka[0:1] + ka[1:2] * pos_lo + ka[2:3] * pos_hi
            q_ref[h, comp] = jnp.where(data, qh, q_aug).astype(jnp.bfloat16)
            k_ref[h, comp] = jnp.where(data, kh, k_aug).astype(jnp.bfloat16)
        vh = v_all[:, h * dh2:(h + 1) * dh2].astype(jnp.bfloat16)
        v_ref[h] = jnp.concatenate([vh, jnp.broadcast_to(ones_col, (tm, dh2))], axis=1)


def _aug_tables(nh, dh2, cvals):
    half = dh2 // 2
    qa = np.zeros((nh, 2, 3, dh2), np.float64)
    ka = np.zeros((nh, 2, 3, dh2), np.float64)
    for h in range(nh):
        pieces = _bf16_split3(cvals[h])
        for comp in range(2):
            base = half if comp == 0 else 0
            for p in range(3):
                for part in (1, 2):
                    qa[h, comp, 0, base + 3 * (part - 1) + p] = pieces[p]
                    ka[h, comp, part, base + 3 * (part - 1) + p] = 1.0
                    qa[h, comp, part, base + 6 + 3 * (part - 1) + p] = -1.0
                    ka[h, comp, 0, base + 6 + 3 * (part - 1) + p] = pieces[p]
    return jnp.asarray(qa, jnp.float32), jnp.asarray(ka, jnp.float32)


def _shift_kernel(qn_ref, kn_ref, shift_ref):
    kmax = jnp.max(kn_ref[...], axis=1, keepdims=True)
    prod = qn_ref[...] * kmax
    u = jnp.sqrt(jnp.maximum(prod[..., 0:1], prod[..., 1:2]))
    shift = jnp.where(2.0 * SKIP_SLACK * u <= SHIFT_MAX_LOG2_RANGE, SKIP_SLACK * u, -1.0)
    shift_ref[...] = jnp.broadcast_to(shift, shift_ref.shape)


def _attn_kernel(reach_ref, shift_ref, q_ref, k_ref, v_ref, bias_ref, lam_ref, g_ref, o_ref,
                 m_scr, acc_scr, *, s, dv, lam_init):
    lax.fori_loop(0, ATT_TILES_PER_STEP,
                  functools.partial(_attn_tile, reach_ref, shift_ref, q_ref, k_ref, v_ref, bias_ref,
                                    lam_ref, g_ref, o_ref, m_scr, acc_scr, s, dv, lam_init), 0)


def _attn_tile(reach_ref, shift_ref, q_ref, k_ref, v_ref, bias_ref, lam_ref, g_ref, o_ref,
               m_scr, acc_scr, s, dv, lam_init, t, carry):
    tq, ck = ATT_TQ, ATT_CK
    nch = s // ck
    hd = pl.program_id(1)
    qi = pl.program_id(2) * ATT_TILES_PER_STEP + t
    nq = pl.num_programs(2) * ATT_TILES_PER_STEP
    i0 = qi * tq
    rows = pl.ds(pl.multiple_of(t * tq, tq), tq)
    lane = lax.broadcasted_iota(jnp.int32, (1, LANES), 1)
    half = LANES // 2

    q_left, q_right, q_diag = [], [], []
    for comp in range(2):
        data = (lane < half) if comp == 0 else (lane >= half)
        qc = q_ref[comp, rows, :]
        q_left.append(qc)
        q_right.append(jnp.where(data, qc, -qc))
        q_diag.append(jnp.where(data, qc, jnp.zeros_like(qc)))

    def scores(j0, table_off, nk):
        j0 = pl.multiple_of(j0, POS_SPLIT)
        ts = []
        for comp in range(2):
            kc = k_ref[comp, pl.ds(j0, nk), :]
            if table_off is None:
                qv = jnp.where(j0 < i0, q_left[comp], q_right[comp])
            else:
                qv = q_diag[comp]
            t = lax.dot_general(qv, kc, (((1,), (1,)), ((), ())),
                                preferred_element_type=jnp.float32)
            if table_off is not None:
                t = t + bias_ref[:, pl.ds(pl.multiple_of(table_off, POS_SPLIT), nk)]
            ts.append(t)
        return j0, ts, nk

    def accumulate(j0, ts, nk, first):
        vc = v_ref[pl.ds(j0, nk), :]
        for comp in range(2):
            p = jnp.exp2(ts[comp] - shift).astype(jnp.bfloat16)
            pv = jnp.dot(p, vc, preferred_element_type=jnp.float32)
            acc_scr[comp] = pv if first else acc_scr[comp] + pv

    def accumulate_online(j0, ts, nk, first):
        vc = v_ref[pl.ds(j0, nk), :]
        for comp in range(2):
            t = ts[comp]
            m_new = jnp.max(t, axis=-1, keepdims=True)
            if not first:
                m_old = m_scr[comp]
                m_new = jnp.maximum(m_old, m_new)
            p = jnp.exp2(t - m_new).astype(jnp.bfloat16)
            pv = jnp.dot(p, vc, preferred_element_type=jnp.float32)
            if first:
                acc_scr[comp] = pv
            else:
                acc_scr[comp] = jnp.exp2(m_old - m_new) * acc_scr[comp] + pv
            m_scr[comp] = m_new

    shift = shift_ref[(pl.program_id(0) * pl.num_programs(1) + hd) * nq + qi]

    reach = reach_ref[hd]
    lo_key = jnp.maximum(i0 - reach, 0)
    hi_key = jnp.minimum(i0 + tq + reach, s)
    start0 = lo_key & (-POS_SPLIT)
    start0 = jnp.where(((i0 - start0) & (ck - 1)) + tq > ck, start0 - POS_SPLIT, start0)
    width_need = (hi_key - start0 + (ck - 1)) // ck

    def sweep(width):
        def run():
            start = jnp.minimum(start0, s - width * ck)
            rel = i0 - start
            own = rel // ck
            pending = scores(start + own * ck, (ck - tq) - (rel & (ck - 1)), ck)
            for n in range(1, width):
                r = own + n
                nxt = scores(start + jnp.where(r >= width, r - width, r) * ck, None, ck)
                accumulate(*pending, first=(n == 1))
                pending = nxt
            accumulate(*pending, first=(width == 1))
        return run

    def dispatch(width):
        if width == nch:
            sweep(nch)()
        else:
            lax.cond(width_need <= width, sweep(width), lambda: dispatch(width + 1))

    def online_sweep():
        own = i0 // ck
        accumulate_online(*scores(own * ck, (ck - tq) - (i0 & (ck - 1)), ck), first=True)

        def body(n, carry):
            r = own + n
            accumulate_online(*scores(jnp.where(r >= nch, r - nch, r) * ck, None, ck), first=False)
            return carry

        lax.fori_loop(1, nch, body, 0)

    lax.cond(shift >= 0.0, lambda: dispatch(1), online_sweep)

    lam_p = lam_ref[...]
    lam = (jnp.exp(jnp.sum(lam_p[0:1] * lam_p[1:2], axis=-1, keepdims=True))
           - jnp.exp(jnp.sum(lam_p[2:3] * lam_p[3:4], axis=-1, keepdims=True)) + lam_init)
    a0 = acc_scr[0]
    a1 = acc_scr[1]
    o = a0[:, :dv] / a0[:, dv:dv + 1] - lam * (a1[:, :dv] / a1[:, dv:dv + 1])
    o = o * _rms_scale(o) * g_ref[...] * (1.0 - lam_init)
    o_ref[rows, :] = o.astype(jnp.bfloat16)
    return carry


def _diff_attention_mixer(hin, g_pre, w_qkv, lq1, lk1, lq2, lk2, subln_g, layer_idx):
    b, s, d = hin.shape
    nh = N_DIFF_HEADS
    dh2 = d // nh
    dh = dh2 // 2
    dv = dh2
    assert dh2 == LANES and s % ATT_CK == 0
    assert s <= POS_SPLIT * 256
    lam_init = 0.8 - 0.6 * math.exp(-0.3 * layer_idx)
    slopes = np.asarray([2.0 ** (-8.0 * (i + 1) / nh) for i in range(nh)], np.float64)
    cvals = slopes * LOG2E
    qa, ka = _aug_tables(nh, dh2, cvals)
    qscale = dh ** -0.5 * LOG2E

    tm = ATT_TQ
    const2 = lambda i, j: (0, 0)
    nt = s // tm
    norm_spec = pl.BlockSpec((None, nh, None, 1, dh2), lambda i, j: (i, 0, j, 0, 0))
    norm_shape = jax.ShapeDtypeStruct((b, nh, nt, 1, dh2), jnp.float32)
    q, k, v, qn, kn = pl.pallas_call(
        functools.partial(_qkv_kernel, d=d, nh=nh, tm=tm, qscale=qscale),
        grid=(b, s // tm),
        in_specs=[
            pl.BlockSpec((None, tm, d), lambda i, j: (i, j, 0)),
            pl.BlockSpec((1, d), const2),
            pl.BlockSpec((d, 3 * d), const2, pipeline_mode=pl.Buffered(1)),
            pl.BlockSpec((nh, 2, 3, dh2), lambda i, j: (0, 0, 0, 0)),
            pl.BlockSpec((nh, 2, 3, dh2), lambda i, j: (0, 0, 0, 0)),
        ],
        out_specs=[
            pl.BlockSpec((None, nh, 2, tm, dh2), lambda i, j: (i, 0, 0, j, 0)),
            pl.BlockSpec((None, nh, 2, tm, dh2), lambda i, j: (i, 0, 0, j, 0)),
            pl.BlockSpec((None, nh, tm, 2 * dv), lambda i, j: (i, 0, j, 0)),
            norm_spec,
            norm_spec,
        ],
        out_shape=[
            jax.ShapeDtypeStruct((b, nh, 2, s, dh2), jnp.bfloat16),
            jax.ShapeDtypeStruct((b, nh, 2, s, dh2), jnp.bfloat16),
            jax.ShapeDtypeStruct((b, nh, s, 2 * dv), jnp.bfloat16),
            norm_shape,
            norm_shape,
        ],
        compiler_params=pltpu.CompilerParams(
            dimension_semantics=("parallel", "parallel"), vmem_limit_bytes=VMEM_LIMIT_BYTES),
        name="qkv_proj",
    )(hin, g_pre.reshape(1, d), w_qkv.astype(jnp.bfloat16), qa, ka)

    tq, ck = ATT_TQ, ATT_CK
    tstep = tq * ATT_TILES_PER_STEP
    assert s % tstep == 0
    wdt = ck + (ck - tq)
    dist = np.abs(np.arange(tq)[:, None] - np.arange(wdt)[None, :] + (ck - tq))
    bias = jnp.asarray(-cvals[:, None, None] * dist[None], jnp.float32)
    lam_p = jnp.stack([lq1, lk1, lq2, lk2]).astype(jnp.float32)

    full = pl.BlockSpec((None, nh, nt, 1, dh2), lambda i: (i, 0, 0, 0, 0))
    shift = pl.pallas_call(
        _shift_kernel,
        grid=(b,),
        in_specs=[full, full],
        out_specs=full,
        out_shape=jax.ShapeDtypeStruct((b, nh, nt, 1, dh2), jnp.float32),
        name="softmax_shift",
    )(qn, kn)
    shift = shift[..., 0, 0].reshape(-1)
    reach = jnp.asarray(np.minimum(np.floor(SKIP_EXP2_ZERO / cvals) + 1, s), jnp.int32)

    o = pl.pallas_call(
        functools.partial(_attn_kernel, s=s, dv=dv, lam_init=lam_init),
        grid=(b, nh, s // tstep),
        in_specs=[
            pl.BlockSpec(memory_space=pltpu.SMEM),
            pl.BlockSpec(memory_space=pltpu.SMEM),
            pl.BlockSpec((None, None, 2, tstep, dh2), lambda i, h, j: (i, h, 0, j, 0)),
            pl.BlockSpec((None, None, 2, s, dh2), lambda i, h, j: (i, h, 0, 0, 0)),
            pl.BlockSpec((None, None, s, 2 * dv), lambda i, h, j: (i, h, 0, 0)),
            pl.BlockSpec((None, tq, wdt), lambda i, h, j: (h, 0, 0)),
            pl.BlockSpec((4, dh), lambda i, h, j: (0, 0)),
            pl.BlockSpec((1, dv), lambda i, h, j: (0, 0)),
        ],
        out_specs=pl.BlockSpec((None, tstep, dv), lambda i, h, j: (i, j, h)),
        out_shape=jax.ShapeDtypeStruct((b, s, nh * dv), jnp.bfloat16),
        scratch_shapes=[
            pltpu.VMEM((2, tq, 1), jnp.float32),
            pltpu.VMEM((2, tq, 2 * dv), jnp.float32),
        ],
        compiler_params=pltpu.CompilerParams(
            dimension_semantics=("parallel", "parallel", "arbitrary"),
            vmem_limit_bytes=VMEM_LIMIT_BYTES),
        name="diff_attention",
    )(reach, shift, q, k, v, bias, lam_p, subln_g.reshape(1, dv))
    return o


def kernel(x, norm_mix_pre, norm_mix_post, norm_ffn_pre, norm_ffn_post, fourier_w_o, diff_w_qkv,
           diff_lambda_q1, diff_lambda_k1, diff_lambda_q2, diff_lambda_k2, diff_subln_g, diff_w_o,
           ffn_w_gate, ffn_w_up, ffn_w_down):
    depth = norm_mix_pre.shape[0]
    h = x
    for i in range(depth):
        j = i // N_MIXERS
        mixer = None
        if i % N_MIXERS == 0:
            h = _fourier_layer(h, norm_mix_pre[i], norm_mix_post[i], fourier_w_o[j])
        else:
            a = _diff_attention_mixer(h, norm_mix_pre[i], diff_w_qkv[j], diff_lambda_q1[j],
                                      diff_lambda_k1[j], diff_lambda_q2[j], diff_lambda_k2[j],
                                      diff_subln_g[j], i)
            mixer = (a, diff_w_o[j], norm_mix_post[i])
        h = _ffn_layer(h, norm_ffn_pre[i], norm_ffn_post[i], ffn_w_gate[i], ffn_w_up[i], ffn_w_down[i],
                       mixer=mixer)
    return h
```

```python
import functools
import math

import numpy as np
import jax
import jax.numpy as jnp
from jax import lax
from jax.experimental import pallas as pl
from jax.experimental.pallas import tpu as pltpu

N_FOURIER_GROUPS = 8
N_DIFF_HEADS = 8
RMS_EPS = 1e-6
N_MIXERS = 2

LANES = 128
VMEM_LIMIT_BYTES = 56 * 1024 * 1024

FFT_N2 = 128
FFT_S2_PER_STEP = 16
FFT_K1_PER_STEP = 8
FFN_TOKEN_TILE = 1024
FFN_CHUNK = 256
ATT_TQ = 512
ATT_TILES_PER_STEP = 4
ATT_CK = 2048
POS_SPLIT = 256
LOG2E = 1.4426950408889634
SKIP_EXP2_ZERO = 153.0
SKIP_SLACK = 1.01
SHIFT_MAX_LOG2_RANGE = 100.0


def _rms_scale(x):
    return lax.rsqrt(jnp.mean(x * x, axis=-1, keepdims=True) + RMS_EPS)


def _np_bf16(v):
    return np.asarray(v, np.float32).astype(jnp.bfloat16)


def _bf16_split3(v):
    v = np.asarray(v, np.float64)
    hi = _np_bf16(v).astype(np.float64)
    mid = _np_bf16(v - hi).astype(np.float64)
    lo = _np_bf16(v - hi - mid).astype(np.float64)
    return hi, mid, lo


def _fft_stage1_kernel(x_ref, g_ref, f1_ref, ar_ref, ai_ref, *, n1, d, nb):
    x = pltpu.einshape("sjd->jsd", x_ref[...]).reshape(nb * n1, d)
    xn = (x * _rms_scale(x) * g_ref[...]).astype(jnp.bfloat16)
    f1 = f1_ref[...]
    a = jnp.stack([jnp.dot(f1, xn[j * n1:(j + 1) * n1], preferred_element_type=jnp.float32)
                   for j in range(nb)], axis=0)
    a = pltpu.einshape("jkd->kjd", a)
    ar_ref[...] = a[:n1].astype(jnp.bfloat16)
    ai_ref[...] = a[n1:].astype(jnp.bfloat16)


def _fft_stage2_kernel(ar_ref, ai_ref, gm_ref, cs_ref, wo_ref, gpost_ref, xres_ref, out_ref,
                       *, n2, d, cg, kb):
    cs = cs_ref[...]
    zs = []
    for i in range(kb):
        a = jnp.concatenate([ar_ref[i], ai_ref[i]], axis=0)
        y = jnp.dot(gm_ref[i], a, preferred_element_type=jnp.float32)
        yr = y[:n2].astype(jnp.bfloat16)
        yi = y[n2:].astype(jnp.bfloat16)
        cols = []
        for gi in range(d // cg):
            lhs = jnp.concatenate([yr[:, gi * cg:(gi + 1) * cg], yi[:, gi * cg:(gi + 1) * cg]], axis=1)
            cols.append(jnp.dot(lhs, cs, preferred_element_type=jnp.float32))
        zs.append(jnp.concatenate(cols, axis=1).astype(jnp.bfloat16))
    z = jnp.concatenate(zs, axis=0)
    m = jnp.dot(z, wo_ref[...], preferred_element_type=jnp.float32)
    r = m * _rms_scale(m) * gpost_ref[...]
    out_ref[...] = xres_ref[...] + pltpu.einshape("ikd->kid", r.reshape(kb, n2, d))


def _fourier_layer(x, g_pre, g_post, w_o):
    b, s, d = x.shape
    n2 = FFT_N2
    n1 = s // n2
    cg = d // N_FOURIER_GROUPS
    nb = FFT_S2_PER_STEP
    assert n1 * n2 == s and n2 % nb == 0

    s1 = np.arange(n1)
    th1 = 2.0 * np.pi * ((s1[:, None] * s1[None, :]) % n1) / n1
    f1 = np.concatenate([np.cos(th1), -np.sin(th1)], axis=0)
    s2 = np.arange(n2)
    kk = (np.arange(n1)[:, None, None] + n1 * np.arange(n2)[None, :, None])
    th2 = 2.0 * np.pi * ((kk * s2[None, None, :]) % s) / s
    gr, gi = np.cos(th2), -np.sin(th2)
    gm = np.concatenate([np.concatenate([gr, -gi], axis=2),
                         np.concatenate([gi, gr], axis=2)], axis=1)
    c = np.arange(cg)
    thc = 2.0 * np.pi * ((c[:, None] * c[None, :]) % cg) / cg
    norm = 1.0 / math.sqrt(s * cg)
    cs = np.concatenate([np.cos(thc), np.sin(thc)], axis=0) * norm

    f1 = jnp.asarray(_np_bf16(f1))
    gm = jnp.asarray(_np_bf16(gm))
    cs = jnp.asarray(_np_bf16(cs))

    x4 = x.reshape(b, n1, n2, d)
    ar, ai = pl.pallas_call(
        functools.partial(_fft_stage1_kernel, n1=n1, d=d, nb=nb),
        grid=(b, n2 // nb),
        in_specs=[
            pl.BlockSpec((None, n1, nb, d), lambda i, j: (i, 0, j, 0)),
            pl.BlockSpec((1, d), lambda i, j: (0, 0)),
            pl.BlockSpec((2 * n1, n1), lambda i, j: (0, 0)),
        ],
        out_specs=[
            pl.BlockSpec((None, n1, nb, d), lambda i, j: (i, 0, j, 0)),
            pl.BlockSpec((None, n1, nb, d), lambda i, j: (i, 0, j, 0)),
        ],
        out_shape=[jax.ShapeDtypeStruct((b, n1, n2, d), jnp.bfloat16)] * 2,
        compiler_params=pltpu.CompilerParams(
            dimension_semantics=("parallel", "parallel"), vmem_limit_bytes=VMEM_LIMIT_BYTES),
        name="fft_stage1",
    )(x4, g_pre.reshape(1, d), f1)

    kb = min(FFT_K1_PER_STEP, n1)
    assert n1 % kb == 0
    xk = x.reshape(b, n2, n1, d)
    out = pl.pallas_call(
        functools.partial(_fft_stage2_kernel, n2=n2, d=d, cg=cg, kb=kb),
        grid=(b, n1 // kb),
        in_specs=[
            pl.BlockSpec((None, kb, n2, d), lambda i, j: (i, j, 0, 0)),
            pl.BlockSpec((None, kb, n2, d), lambda i, j: (i, j, 0, 0)),
            pl.BlockSpec((kb, 2 * n2, 2 * n2), lambda i, j: (j, 0, 0)),
            pl.BlockSpec((2 * cg, cg), lambda i, j: (0, 0)),
            pl.BlockSpec((d, d), lambda i, j: (0, 0)),
            pl.BlockSpec((1, d), lambda i, j: (0, 0)),
            pl.BlockSpec((None, n2, kb, d), lambda i, j: (i, 0, j, 0)),
        ],
        out_specs=pl.BlockSpec((None, n2, kb, d), lambda i, j: (i, 0, j, 0)),
        out_shape=jax.ShapeDtypeStruct((b, n2, n1, d), jnp.float32),
        compiler_params=pltpu.CompilerParams(
            dimension_semantics=("parallel", "parallel"), vmem_limit_bytes=VMEM_LIMIT_BYTES),
        name="fft_stage2",
    )(ar, ai, gm, cs, w_o.astype(jnp.bfloat16), g_post.reshape(1, d), xk)
    return out.reshape(b, s, d)


def _ffn_kernel(*refs, f, chunk, mixer_proj):
    if mixer_proj:
        a_ref, wo_ref, gmix_ref, x_ref, gpre_ref, gpost_ref, wg_ref, wu_ref, wd_ref, out_ref = refs
        m = jnp.dot(a_ref[...], wo_ref[...], preferred_element_type=jnp.float32)
        x = x_ref[...] + m * _rms_scale(m) * gmix_ref[...]
    else:
        x_ref, gpre_ref, gpost_ref, wg_ref, wu_ref, wd_ref, out_ref = refs
        x = x_ref[...]
    xn = (x * _rms_scale(x) * gpre_ref[...]).astype(jnp.bfloat16)
    acc = jnp.zeros(x.shape, jnp.float32)
    for c0 in range(0, f, chunk):
        c1 = min(c0 + chunk, f)
        gt = jnp.dot(xn, wg_ref[:, c0:c1], preferred_element_type=jnp.float32)
        up = jnp.dot(xn, wu_ref[:, c0:c1], preferred_element_type=jnp.float32)
        hh = (gt * jax.nn.sigmoid(gt) * up).astype(jnp.bfloat16)
        acc = acc + jnp.dot(hh, wd_ref[c0:c1, :], preferred_element_type=jnp.float32)
    out_ref[...] = x + acc * _rms_scale(acc) * gpost_ref[...]


def _ffn_layer(h, g_pre, g_post, w_gate, w_up, w_down, mixer=None):
    b, s, d = h.shape
    f = w_gate.shape[1]
    t = b * s
    tm = min(FFN_TOKEN_TILE, t)
    assert t % tm == 0
    const = lambda i: (0, 0)
    rows = pl.BlockSpec((tm, d), lambda i: (i, 0))
    vec = pl.BlockSpec((1, d), const)
    in_specs = [rows, vec, vec,
                pl.BlockSpec((d, f), const, pipeline_mode=pl.Buffered(1)),
                pl.BlockSpec((d, f), const, pipeline_mode=pl.Buffered(1)),
                pl.BlockSpec((f, d), const, pipeline_mode=pl.Buffered(1))]
    args = [h.reshape(t, d), g_pre.reshape(1, d), g_post.reshape(1, d),
            w_gate.astype(jnp.bfloat16), w_up.astype(jnp.bfloat16), w_down.astype(jnp.bfloat16)]
    if mixer is not None:
        a, w_o, g_mix = mixer
        in_specs = [rows, pl.BlockSpec((d, d), const, pipeline_mode=pl.Buffered(1)), vec] + in_specs
        args = [a.reshape(t, d), w_o.astype(jnp.bfloat16), g_mix.reshape(1, d)] + args
    out = pl.pallas_call(
        functools.partial(_ffn_kernel, f=f, chunk=FFN_CHUNK, mixer_proj=mixer is not None),
        grid=(t // tm,),
        in_specs=in_specs,
        out_specs=rows,
        out_shape=jax.ShapeDtypeStruct((t, d), jnp.float32),
        compiler_params=pltpu.CompilerParams(
            dimension_semantics=("parallel",), vmem_limit_bytes=VMEM_LIMIT_BYTES),
        name="swiglu_ffn",
    )(*args)
    return out.reshape(b, s, d)


def _qkv_kernel(x_ref, gpre_ref, w_ref, qa_ref, ka_ref, q_ref, k_ref, v_ref, qn_ref, kn_ref,
                *, d, nh, tm, qscale):
    x = x_ref[...]
    xn = (x * _rms_scale(x) * gpre_ref[...]).astype(jnp.bfloat16)
    dh2 = d // nh
    row = pl.program_id(1) * tm + lax.broadcasted_iota(jnp.int32, (tm, 1), 0)
    pos_lo = (row & (POS_SPLIT - 1)).astype(jnp.float32)
    pos_hi = (row & (-POS_SPLIT)).astype(jnp.float32)
    lane = lax.broadcasted_iota(jnp.int32, (1, dh2), 1)
    half = dh2 // 2
    q_all = jnp.dot(xn, w_ref[:, 0:d], preferred_element_type=jnp.float32) * qscale
    k_all = jnp.dot(xn, w_ref[:, d:2 * d], preferred_element_type=jnp.float32)
    v_all = jnp.dot(xn, w_ref[:, 2 * d:3 * d], preferred_element_type=jnp.float32)
    ones_col = jnp.where(lane == 0, 1.0, 0.0).astype(jnp.bfloat16)
    lane_col = lax.broadcasted_iota(jnp.int32, (dh2, 1), 0)
    comp_sum = jnp.where((lane_col >= half).astype(jnp.int32) == lane, 1.0, 0.0).astype(jnp.bfloat16)
    for h in range(nh):
        qh = q_all[:, h * dh2:(h + 1) * dh2]
        kh = k_all[:, h * dh2:(h + 1) * dh2]
        for src, dst in ((qh, qn_ref), (kh, kn_ref)):
            vb = src.astype(jnp.bfloat16).astype(jnp.float32)
            sq = jnp.dot((vb * vb).astype(jnp.bfloat16), comp_sum, preferred_element_type=jnp.float32)
            dst[h] = jnp.max(sq, axis=0, keepdims=True)
        for comp in range(2):
            data = (lane < half) if comp == 0 else (lane >= half)
            qa = qa_ref[h, comp]
            ka = ka_ref[h, comp]
            q_aug = qa[0:1] + qa[1:2] * pos_lo + qa[2:3] * pos_hi
            k_aug = ka[0:1] + ka[1:2] * pos_lo + ka[2:3] * pos_hi
            q_ref[h, comp] = jnp.where(data, qh, q_aug).astype(jnp.bfloat16)
            k_ref[h, comp] = jnp.where(data, kh, k_aug).astype(jnp.bfloat16)
        vh = v_all[:, h * dh2:(h + 1) * dh2].astype(jnp.bfloat16)
        v_ref[h] = jnp.concatenate([vh, jnp.broadcast_to(ones_col, (tm, dh2))], axis=1)


def _aug_tables(nh, dh2, cvals):
    half = dh2 // 2
    qa = np.zeros((nh, 2, 3, dh2), np.float64)
    ka = np.zeros((nh, 2, 3, dh2), np.float64)
    for h in range(nh):
        pieces = _bf16_split3(cvals[h])
        for comp in range(2):
            base = half if comp == 0 else 0
            for p in range(3):
                for part in (1, 2):
                    qa[h, comp, 0, base + 3 * (part - 1) + p] = pieces[p]
                    ka[h, comp, part, base + 3 * (part - 1) + p] = 1.0
                    qa[h, comp, part, base + 6 + 3 * (part - 1) + p] = -1.0
                    ka[h, comp, 0, base + 6 + 3 * (part - 1) + p] = pieces[p]
    return jnp.asarray(qa, jnp.float32), jnp.asarray(ka, jnp.float32)


def _shift_kernel(qn_ref, kn_ref, shift_ref):
    kmax = jnp.max(kn_ref[...], axis=1, keepdims=True)
    prod = qn_ref[...] * kmax
    u = jnp.sqrt(jnp.maximum(prod[..., 0:1], prod[..., 1:2]))
    shift = jnp.where(2.0 * SKIP_SLACK * u <= SHIFT_MAX_LOG2_RANGE, SKIP_SLACK * u, -1.0)
    shift_ref[...] = jnp.broadcast_to(shift, shift_ref.shape)


def _attn_kernel(reach_ref, shift_ref, q_ref, k_ref, v_ref, bias_ref, lam_ref, g_ref, o_ref,
                 m_scr, acc_scr, *, s, dv, lam_init):
    lax.fori_loop(0, ATT_TILES_PER_STEP,
                  functools.partial(_attn_tile, reach_ref, shift_ref, q_ref, k_ref, v_ref, bias_ref,
                                    lam_ref, g_ref, o_ref, m_scr, acc_scr, s, dv, lam_init), 0)


def _attn_tile(reach_ref, shift_ref, q_ref, k_ref, v_ref, bias_ref, lam_ref, g_ref, o_ref,
               m_scr, acc_scr, s, dv, lam_init, t, carry):
    tq, ck = ATT_TQ, ATT_CK
    nch = s // ck
    hd = pl.program_id(1)
    qi = pl.program_id(2) * ATT_TILES_PER_STEP + t
    nq = pl.num_programs(2) * ATT_TILES_PER_STEP
    i0 = qi * tq
    rows = pl.ds(pl.multiple_of(t * tq, tq), tq)
    lane = lax.broadcasted_iota(jnp.int32, (1, LANES), 1)
    half = LANES // 2

    q_left, q_right, q_diag = [], [], []
    for comp in range(2):
        data = (lane < half) if comp == 0 else (lane >= half)
        qc = q_ref[comp, rows, :]
        q_left.append(qc)
        q_right.append(jnp.where(data, qc, -qc))
        q_diag.append(jnp.where(data, qc, jnp.zeros_like(qc)))

    def scores(j0, table_off, nk):
        j0 = pl.multiple_of(j0, POS_SPLIT)
        ts = []
        for comp in range(2):
            kc = k_ref[comp, pl.ds(j0, nk), :]
            if table_off is None:
                qv = jnp.where(j0 < i0, q_left[comp], q_right[comp])
            else:
                qv = q_diag[comp]
            t = lax.dot_general(qv, kc, (((1,), (1,)), ((), ())),
                                preferred_element_type=jnp.float32)
            if table_off is not None:
                t = t + bias_ref[:, pl.ds(pl.multiple_of(table_off, POS_SPLIT), nk)]
            ts.append(t)
        return j0, ts, nk

    def accumulate(j0, ts, nk, first):
        vc = v_ref[pl.ds(j0, nk), :]
        for comp in range(2):
            p = jnp.exp2(ts[comp] - shift).astype(jnp.bfloat16)
            pv = jnp.dot(p, vc, preferred_element_type=jnp.float32)
            acc_scr[comp] = pv if first else acc_scr[comp] + pv

    def accumulate_online(j0, ts, nk, first):
        vc = v_ref[pl.ds(j0, nk), :]
        for comp in range(2):
            t = ts[comp]
            m_new = jnp.max(t, axis=-1, keepdims=True)
            if not first:
                m_old = m_scr[comp]
                m_new = jnp.maximum(m_old, m_new)
            p = jnp.exp2(t - m_new).astype(jnp.bfloat16)
            pv = jnp.dot(p, vc, preferred_element_type=jnp.float32)
            if first:
                acc_scr[comp] = pv
            else:
                acc_scr[comp] = jnp.exp2(m_old - m_new) * acc_scr[comp] + pv
            m_scr[comp] = m_new

    shift = shift_ref[(pl.program_id(0) * pl.num_programs(1) + hd) * nq + qi]

    reach = reach_ref[hd]
    lo_key = jnp.maximum(i0 - reach, 0)
    hi_key = jnp.minimum(i0 + tq + reach, s)
    start0 = lo_key & (-POS_SPLIT)
    start0 = jnp.where(((i0 - start0) & (ck - 1)) + tq > ck, start0 - POS_SPLIT, start0)
    width_need = (hi_key - start0 + (ck - 1)) // ck

    def sweep(width):
        def run():
            start = jnp.minimum(start0, s - width * ck)
            rel = i0 - start
            own = rel // ck
            pending = scores(start + own * ck, (ck - tq) - (rel & (ck - 1)), ck)
            for n in range(1, width):
                r = own + n
                nxt = scores(start + jnp.where(r >= width, r - width, r) * ck, None, ck)
                accumulate(*pending, first=(n == 1))
                pending = nxt
            accumulate(*pending, first=(width == 1))
        return run

    def dispatch(width):
        if width == nch:
            sweep(nch)()
        else:
            lax.cond(width_need <= width, sweep(width), lambda: dispatch(width + 1))

    def online_sweep():
        own = i0 // ck
        accumulate_online(*scores(own * ck, (ck - tq) - (i0 & (ck - 1)), ck), first=True)

        def body(n, carry):
            r = own + n
            accumulate_online(*scores(jnp.where(r >= nch, r - nch, r) * ck, None, ck), first=False)
            return carry

        lax.fori_loop(1, nch, body, 0)

    lax.cond(shift >= 0.0, lambda: dispatch(1), online_sweep)

    lam_p = lam_ref[...]
    lam = (jnp.exp(jnp.sum(lam_p[0:1] * lam_p[1:2], axis=-1, keepdims=True))
           - jnp.exp(jnp.sum(lam_p[2:3] * lam_p[3:4], axis=-1, keepdims=True)) + lam_init)
    a0 = acc_scr[0]
    a1 = acc_scr[1]
    o = a0[:, :dv] / a0[:, dv:dv + 1] - lam * (a1[:, :dv] / a1[:, dv:dv + 1])
    o = o * _rms_scale(o) * g_ref[...] * (1.0 - lam_init)
    o_ref[rows, :] = o.astype(jnp.bfloat16)
    return carry


def _diff_attention_mixer(hin, g_pre, w_qkv, lq1, lk1, lq2, lk2, subln_g, layer_idx):
    b, s, d = hin.shape
    nh = N_DIFF_HEADS
    dh2 = d // nh
    dh = dh2 // 2
    dv = dh2
    assert dh2 == LANES and s % ATT_CK == 0
    assert s <= POS_SPLIT * 256
    lam_init = 0.8 - 0.6 * math.exp(-0.3 * layer_idx)
    slopes = np.asarray([2.0 ** (-8.0 * (i + 1) / nh) for i in range(nh)], np.float64)
    cvals = slopes * LOG2E
    qa, ka = _aug_tables(nh, dh2, cvals)
    qscale = dh ** -0.5 * LOG2E

    tm = ATT_TQ
    const2 = lambda i, j: (0, 0)
    nt = s // tm
    norm_spec = pl.BlockSpec((None, nh, None, 1, dh2), lambda i, j: (i, 0, j, 0, 0))
    norm_shape = jax.ShapeDtypeStruct((b, nh, nt, 1, dh2), jnp.float32)
    q, k, v, qn, kn = pl.pallas_call(
        functools.partial(_qkv_kernel, d=d, nh=nh, tm=tm, qscale=qscale),
        grid=(b, s // tm),
        in_specs=[
            pl.BlockSpec((None, tm, d), lambda i, j: (i, j, 0)),
            pl.BlockSpec((1, d), const2),
            pl.BlockSpec((d, 3 * d), const2, pipeline_mode=pl.Buffered(1)),
            pl.BlockSpec((nh, 2, 3, dh2), lambda i, j: (0, 0, 0, 0)),
            pl.BlockSpec((nh, 2, 3, dh2), lambda i, j: (0, 0, 0, 0)),
        ],
        out_specs=[
            pl.BlockSpec((None, nh, 2, tm, dh2), lambda i, j: (i, 0, 0, j, 0)),
            pl.BlockSpec((None, nh, 2, tm, dh2), lambda i, j: (i, 0, 0, j, 0)),
            pl.BlockSpec((None, nh, tm, 2 * dv), lambda i, j: (i, 0, j, 0)),
            norm_spec,
            norm_spec,
        ],
        out_shape=[
            jax.ShapeDtypeStruct((b, nh, 2, s, dh2), jnp.bfloat16),
            jax.ShapeDtypeStruct((b, nh, 2, s, dh2), jnp.bfloat16),
            jax.ShapeDtypeStruct((b, nh, s, 2 * dv), jnp.bfloat16),
            norm_shape,
            norm_shape,
        ],
        compiler_params=pltpu.CompilerParams(
            dimension_semantics=("parallel", "parallel"), vmem_limit_bytes=VMEM_LIMIT_BYTES),
        name="qkv_proj",
    )(hin, g_pre.reshape(1, d), w_qkv.astype(jnp.bfloat16), qa, ka)

    tq, ck = ATT_TQ, ATT_CK
    tstep = tq * ATT_TILES_PER_STEP
    assert s % tstep == 0
    wdt = ck + (ck - tq)
    dist = np.abs(np.arange(tq)[:, None] - np.arange(wdt)[None, :] + (ck - tq))
    bias = jnp.asarray(-cvals[:, None, None] * dist[None], jnp.float32)
    lam_p = jnp.stack([lq1, lk1, lq2, lk2]).astype(jnp.float32)

    full = pl.BlockSpec((None, nh, nt, 1, dh2), lambda i: (i, 0, 0, 0, 0))
    shift = pl.pallas_call(
        _shift_kernel,
        grid=(b,),
        in_specs=[full, full],
        out_specs=full,
        out_shape=jax.ShapeDtypeStruct((b, nh, nt, 1, dh2), jnp.float32),
        name="softmax_shift",
    )(qn, kn)
    shift = shift[..., 0, 0].reshape(-1)
    reach = jnp.asarray(np.minimum(np.floor(SKIP_EXP2_ZERO / cvals) + 1, s), jnp.int32)

    o = pl.pallas_call(
        functools.partial(_attn_kernel, s=s, dv=dv, lam_init=lam_init),
        grid=(b, nh, s // tstep),
        in_specs=[
            pl.BlockSpec(memory_space=pltpu.SMEM),
            pl.BlockSpec(memory_space=pltpu.SMEM),
            pl.BlockSpec((None, None, 2, tstep, dh2), lambda i, h, j: (i, h, 0, j, 0)),
            pl.BlockSpec((None, None, 2, s, dh2), lambda i, h, j: (i, h, 0, 0, 0)),
            pl.BlockSpec((None, None, s, 2 * dv), lambda i, h, j: (i, h, 0, 0)),
            pl.BlockSpec((None, tq, wdt), lambda i, h, j: (h, 0, 0)),
            pl.BlockSpec((4, dh), lambda i, h, j: (0, 0)),
            pl.BlockSpec((1, dv), lambda i, h, j: (0, 0)),
        ],
        out_specs=pl.BlockSpec((None, tstep, dv), lambda i, h, j: (i, j, h)),
        out_shape=jax.ShapeDtypeStruct((b, s, nh * dv), jnp.bfloat16),
        scratch_shapes=[
            pltpu.VMEM((2, tq, 1), jnp.float32),
            pltpu.VMEM((2, tq, 2 * dv), jnp.float32),
        ],
        compiler_params=pltpu.CompilerParams(
            dimension_semantics=("parallel", "parallel", "arbitrary"),
            vmem_limit_bytes=VMEM_LIMIT_BYTES),
        name="diff_attention",
    )(reach, shift, q, k, v, bias, lam_p, subln_g.reshape(1, dv))
    return o


def kernel(x, norm_mix_pre, norm_mix_post, norm_ffn_pre, norm_ffn_post, fourier_w_o, diff_w_qkv,
           diff_lambda_q1, diff_lambda_k1, diff_lambda_q2, diff_lambda_k2, diff_subln_g, diff_w_o,
           ffn_w_gate, ffn_w_up, ffn_w_down):
    depth = norm_mix_pre.shape[0]
    h = x
    for i in range(depth):
        j = i // N_MIXERS
        mixer = None
        if i % N_MIXERS == 0:
            h = _fourier_layer(h, norm_mix_pre[i], norm_mix_post[i], fourier_w_o[j])
        else:
            a = _diff_attention_mixer(h, norm_mix_pre[i], diff_w_qkv[j], diff_lambda_q1[j],
                                      diff_lambda_k1[j], diff_lambda_q2[j], diff_lambda_k2[j],
                                      diff_subln_g[j], i)
            mixer = (a, diff_w_o[j], norm_mix_post[i])
        h = _ffn_layer(h, norm_ffn_pre[i], norm_ffn_post[i], ffn_w_gate[i], ffn_w_up[i], ffn_w_down[i],
                       mixer=mixer)
    return h
```

```python
import functools
import math

import numpy as np
import jax
import jax.numpy as jnp
from jax import lax
from jax.experimental import pallas as pl
from jax.experimental.pallas import tpu as pltpu

N_FOURIER_GROUPS = 8
N_DIFF_HEADS = 8
RMS_EPS = 1e-6
N_MIXERS = 2

LANES = 128
VMEM_LIMIT_BYTES = 56 * 1024 * 1024

FFT_N2 = 128
FFT_S2_PER_STEP = 16
FFT_K1_PER_STEP = 8
FFN_TOKEN_TILE = 1024
FFN_CHUNK = 256
ATT_TQ = 512
ATT_TILES_PER_STEP = 4
ATT_CK = 2048
POS_SPLIT = 256
LOG2E = 1.4426950408889634
SKIP_EXP2_ZERO = 153.0
SKIP_SLACK = 1.01
SHIFT_MAX_LOG2_RANGE = 100.0


def _rms_scale(x):
    return lax.rsqrt(jnp.mean(x * x, axis=-1, keepdims=True) + RMS_EPS)


def _np_bf16(v):
    return np.asarray(v, np.float32).astype(jnp.bfloat16)


def _bf16_split3(v):
    v = np.asarray(v, np.float64)
    hi = _np_bf16(v).astype(np.float64)
    mid = _np_bf16(v - hi).astype(np.float64)
    lo = _np_bf16(v - hi - mid).astype(np.float64)
    return hi, mid, lo


def _fft_stage1_kernel(x_ref, g_ref, f1_ref, ar_ref, ai_ref, *, n1, d, nb):
    x = pltpu.einshape("sjd->jsd", x_ref[...]).reshape(nb * n1, d)
    xn = (x * _rms_scale(x) * g_ref[...]).astype(jnp.bfloat16)
    f1 = f1_ref[...]
    a = jnp.stack([jnp.dot(f1, xn[j * n1:(j + 1) * n1], preferred_element_type=jnp.float32)
                   for j in range(nb)], axis=0)
    a = pltpu.einshape("jkd->kjd", a)
    ar_ref[...] = a[:n1].astype(jnp.bfloat16)
    ai_ref[...] = a[n1:].astype(jnp.bfloat16)


def _fft_stage2_kernel(ar_ref, ai_ref, gm_ref, cs_ref, wo_ref, gpost_ref, xres_ref, out_ref,
                       *, n2, d, cg, kb):
    cs = cs_ref[...]
    zs = []
    for i in range(kb):
        a = jnp.concatenate([ar_ref[i], ai_ref[i]], axis=0)
        y = jnp.dot(gm_ref[i], a, preferred_element_type=jnp.float32)
        yr = y[:n2].astype(jnp.bfloat16)
        yi = y[n2:].astype(jnp.bfloat16)
        cols = []
        for gi in range(d // cg):
            lhs = jnp.concatenate([yr[:, gi * cg:(gi + 1) * cg], yi[:, gi * cg:(gi + 1) * cg]], axis=1)
            cols.append(jnp.dot(lhs, cs, preferred_element_type=jnp.float32))
        zs.append(jnp.concatenate(cols, axis=1).astype(jnp.bfloat16))
    z = jnp.concatenate(zs, axis=0)
    m = jnp.dot(z, wo_ref[...], preferred_element_type=jnp.float32)
    r = m * _rms_scale(m) * gpost_ref[...]
    out_ref[...] = xres_ref[...] + pltpu.einshape("ikd->kid", r.reshape(kb, n2, d))


def _fourier_layer(x, g_pre, g_post, w_o):
    b, s, d = x.shape
    n2 = FFT_N2
    n1 = s // n2
    cg = d // N_FOURIER_GROUPS
    nb = FFT_S2_PER_STEP
    assert n1 * n2 == s and n2 % nb == 0

    s1 = np.arange(n1)
    th1 = 2.0 * np.pi * ((s1[:, None] * s1[None, :]) % n1) / n1
    f1 = np.concatenate([np.cos(th1), -np.sin(th1)], axis=0)
    s2 = np.arange(n2)
    kk = (np.arange(n1)[:, None, None] + n1 * np.arange(n2)[None, :, None])
    th2 = 2.0 * np.pi * ((kk * s2[None, None, :]) % s) / s
    gr, gi = np.cos(th2), -np.sin(th2)
    gm = np.concatenate([np.concatenate([gr, -gi], axis=2),
                         np.concatenate([gi, gr], axis=2)], axis=1)
    c = np.arange(cg)
    thc = 2.0 * np.pi * ((c[:, None] * c[None, :]) % cg) / cg
    norm = 1.0 / math.sqrt(s * cg)
    cs = np.concatenate([np.cos(thc), np.sin(thc)], axis=0) * norm

    f1 = jnp.asarray(_np_bf16(f1))
    gm = jnp.asarray(_np_bf16(gm))
    cs = jnp.asarray(_np_bf16(cs))

    x4 = x.reshape(b, n1, n2, d)
    ar, ai = pl.pallas_call(
        functools.partial(_fft_stage1_kernel, n1=n1, d=d, nb=nb),
        grid=(b, n2 // nb),
        in_specs=[
            pl.BlockSpec((None, n1, nb, d), lambda i, j: (i, 0, j, 0)),
            pl.BlockSpec((1, d), lambda i, j: (0, 0)),
            pl.BlockSpec((2 * n1, n1), lambda i, j: (0, 0)),
        ],
        out_specs=[
            pl.BlockSpec((None, n1, nb, d), lambda i, j: (i, 0, j, 0)),
            pl.BlockSpec((None, n1, nb, d), lambda i, j: (i, 0, j, 0)),
        ],
        out_shape=[jax.ShapeDtypeStruct((b, n1, n2, d), jnp.bfloat16)] * 2,
        compiler_params=pltpu.CompilerParams(
            dimension_semantics=("parallel", "parallel"), vmem_limit_bytes=VMEM_LIMIT_BYTES),
        name="fft_stage1",
    )(x4, g_pre.reshape(1, d), f1)

    kb = min(FFT_K1_PER_STEP, n1)
    assert n1 % kb == 0
    xk = x.reshape(b, n2, n1, d)
    out = pl.pallas_call(
        functools.partial(_fft_stage2_kernel, n2=n2, d=d, cg=cg, kb=kb),
        grid=(b, n1 // kb),
        in_specs=[
            pl.BlockSpec((None, kb, n2, d), lambda i, j: (i, j, 0, 0)),
            pl.BlockSpec((None, kb, n2, d), lambda i, j: (i, j, 0, 0)),
            pl.BlockSpec((kb, 2 * n2, 2 * n2), lambda i, j: (j, 0, 0)),
            pl.BlockSpec((2 * cg, cg), lambda i, j: (0, 0)),
            pl.BlockSpec((d, d), lambda i, j: (0, 0)),
            pl.BlockSpec((1, d), lambda i, j: (0, 0)),
            pl.BlockSpec((None, n2, kb, d), lambda i, j: (i, 0, j, 0)),
        ],
        out_specs=pl.BlockSpec((None, n2, kb, d), lambda i, j: (i, 0, j, 0)),
        out_shape=jax.ShapeDtypeStruct((b, n2, n1, d), jnp.float32),
        compiler_params=pltpu.CompilerParams(
            dimension_semantics=("parallel", "parallel"), vmem_limit_bytes=VMEM_LIMIT_BYTES),
        name="fft_stage2",
    )(ar, ai, gm, cs, w_o.astype(jnp.bfloat16), g_post.reshape(1, d), xk)
    return out.reshape(b, s, d)


def _ffn_kernel(*refs, f, chunk, mixer_proj):
    if mixer_proj:
        a_ref, wo_ref, gmix_ref, x_ref, gpre_ref, gpost_ref, wgu_ref, wd_ref, out_ref = refs
        m = jnp.dot(a_ref[...], wo_ref[...], preferred_element_type=jnp.float32)
        x = x_ref[...] + m * _rms_scale(m) * gmix_ref[...]
    else:
        x_ref, gpre_ref, gpost_ref, wgu_ref, wd_ref, out_ref = refs
        x = x_ref[...]
    xn = (x * _rms_scale(x) * gpre_ref[...]).astype(jnp.bfloat16)
    acc = jnp.zeros(x.shape, jnp.float32)
    for c0 in range(0, f, chunk):
        gu = jnp.dot(xn, wgu_ref[:, 2 * c0:2 * (c0 + chunk)], preferred_element_type=jnp.float32)
        gt = gu[:, :chunk]
        up = gu[:, chunk:]
        hh = (gt * jax.nn.sigmoid(gt) * up).astype(jnp.bfloat16)
        acc = acc + jnp.dot(hh, wd_ref[c0:c0 + chunk, :], preferred_element_type=jnp.float32)
    out_ref[...] = x + acc * _rms_scale(acc) * gpost_ref[...]


def _ffn_layer(h, g_pre, g_post, w_gate, w_up, w_down, mixer=None):
    b, s, d = h.shape
    f = w_gate.shape[1]
    t = b * s
    tm = min(FFN_TOKEN_TILE, t)
    assert t % tm == 0
    const = lambda i: (0, 0)
    rows = pl.BlockSpec((tm, d), lambda i: (i, 0))
    vec = pl.BlockSpec((1, d), const)
    assert f % FFN_CHUNK == 0
    nc = f // FFN_CHUNK
    w_gu = jnp.concatenate([w_gate.astype(jnp.bfloat16).reshape(d, nc, FFN_CHUNK),
                            w_up.astype(jnp.bfloat16).reshape(d, nc, FFN_CHUNK)], axis=2).reshape(d, 2 * f)
    in_specs = [rows, vec, vec,
                pl.BlockSpec((d, 2 * f), const, pipeline_mode=pl.Buffered(1)),
                pl.BlockSpec((f, d), const, pipeline_mode=pl.Buffered(1))]
    args = [h.reshape(t, d), g_pre.reshape(1, d), g_post.reshape(1, d), w_gu, w_down.astype(jnp.bfloat16)]
    if mixer is not None:
        a, w_o, g_mix = mixer
        in_specs = [rows, pl.BlockSpec((d, d), const, pipeline_mode=pl.Buffered(1)), vec] + in_specs
        args = [a.reshape(t, d), w_o.astype(jnp.bfloat16), g_mix.reshape(1, d)] + args
    out = pl.pallas_call(
        functools.partial(_ffn_kernel, f=f, chunk=FFN_CHUNK, mixer_proj=mixer is not None),
        grid=(t // tm,),
        in_specs=in_specs,
        out_specs=rows,
        out_shape=jax.ShapeDtypeStruct((t, d), jnp.float32),
        compiler_params=pltpu.CompilerParams(
            dimension_semantics=("parallel",), vmem_limit_bytes=VMEM_LIMIT_BYTES),
        name="swiglu_ffn",
    )(*args)
    return out.reshape(b, s, d)


def _qkv_kernel(x_ref, gpre_ref, w_ref, qa_ref, ka_ref, q_ref, k_ref, v_ref, qn_ref, kn_ref,
                *, d, nh, tm, qscale):
    x = x_ref[...]
    xn = (x * _rms_scale(x) * gpre_ref[...]).astype(jnp.bfloat16)
    dh2 = d // nh
    row = pl.program_id(1) * tm + lax.broadcasted_iota(jnp.int32, (tm, 1), 0)
    pos_lo = (row & (POS_SPLIT - 1)).astype(jnp.float32)
    pos_hi = (row & (-POS_SPLIT)).astype(jnp.float32)
    lane = lax.broadcasted_iota(jnp.int32, (1, dh2), 1)
    half = dh2 // 2
    q_all = jnp.dot(xn, w_ref[:, 0:d], preferred_element_type=jnp.float32) * qscale
    k_all = jnp.dot(xn, w_ref[:, d:2 * d], preferred_element_type=jnp.float32)
    v_all = jnp.dot(xn, w_ref[:, 2 * d:3 * d], preferred_element_type=jnp.float32)
    ones_col = jnp.where(lane == 0, 1.0, 0.0).astype(jnp.bfloat16)
    lane_col = lax.broadcasted_iota(jnp.int32, (dh2, 1), 0)
    comp_sum = jnp.where((lane_col >= half).astype(jnp.int32) == lane, 1.0, 0.0).astype(jnp.bfloat16)
    for h in range(nh):
        qh = q_all[:, h * dh2:(h + 1) * dh2]
        kh = k_all[:, h * dh2:(h + 1) * dh2]
        for src, dst in ((qh, qn_ref), (kh, kn_ref)):
            vb = src.astype(jnp.bfloat16).astype(jnp.float32)
            sq = jnp.dot((vb * vb).astype(jnp.bfloat16), comp_sum, preferred_element_type=jnp.float32)
            dst[h] = jnp.max(sq, axis=0, keepdims=True)
        for comp in range(2):
            data = (lane < half) if comp == 0 else (lane >= half)
            qa = qa_ref[h, comp]
            ka = ka_ref[h, comp]
            q_aug = qa[0:1] + qa[1:2] * pos_lo + qa[2:3] * pos_hi
            k_aug = ka[0:1] + ka[1:2] * pos_lo + ka[2:3] * pos_hi
            q_ref[h, comp] = jnp.where(data, qh, q_aug).astype(jnp.bfloat16)
            k_ref[h, comp] = jnp.where(data, kh, k_aug).astype(jnp.bfloat16)
        vh = v_all[:, h * dh2:(h + 1) * dh2].astype(jnp.bfloat16)
        v_ref[h] = jnp.concatenate([vh, jnp.broadcast_to(ones_col, (tm, dh2))], axis=1)


def _aug_tables(nh, dh2, cvals):
    half = dh2 // 2
    qa = np.zeros((nh, 2, 3, dh2), np.float64)
    ka = np.zeros((nh, 2, 3, dh2), np.float64)
    for h in range(nh):
        pieces = _bf16_split3(cvals[h])
        for comp in range(2):
            base = half if comp == 0 else 0
            for p in range(3):
                for part in (1, 2):
                    qa[h, comp, 0, base + 3 * (part - 1) + p] = pieces[p]
                    ka[h, comp, part, base + 3 * (part - 1) + p] = 1.0
                    qa[h, comp, part, base + 6 + 3 * (part - 1) + p] = -1.0
                    ka[h, comp, 0, base + 6 + 3 * (part - 1) + p] = pieces[p]
    return jnp.asarray(qa, jnp.float32), jnp.asarray(ka, jnp.float32)


def _shift_kernel(qn_ref, kn_ref, shift_ref):
    kmax = jnp.max(kn_ref[...], axis=1, keepdims=True)
    prod = qn_ref[...] * kmax
    u = jnp.sqrt(jnp.maximum(prod[..., 0:1], prod[..., 1:2]))
    shift = jnp.where(2.0 * SKIP_SLACK * u <= SHIFT_MAX_LOG2_RANGE, SKIP_SLACK * u, -1.0)
    shift_ref[...] = jnp.broadcast_to(shift, shift_ref.shape)


def _attn_kernel(reach_ref, shift_ref, q_ref, k_ref, v_ref, bias_ref, lam_ref, g_ref, o_ref,
                 m_scr, acc_scr, *, s, dv, lam_init):
    lax.fori_loop(0, ATT_TILES_PER_STEP,
                  functools.partial(_attn_tile, reach_ref, shift_ref, q_ref, k_ref, v_ref, bias_ref,
                                    lam_ref, g_ref, o_ref, m_scr, acc_scr, s, dv, lam_init), 0)


def _attn_tile(reach_ref, shift_ref, q_ref, k_ref, v_ref, bias_ref, lam_ref, g_ref, o_ref,
               m_scr, acc_scr, s, dv, lam_init, t, carry):
    tq, ck = ATT_TQ, ATT_CK
    nch = s // ck
    hd = pl.program_id(1)
    qi = pl.program_id(2) * ATT_TILES_PER_STEP + t
    nq = pl.num_programs(2) * ATT_TILES_PER_STEP
    i0 = qi * tq
    rows = pl.ds(pl.multiple_of(t * tq, tq), tq)
    lane = lax.broadcasted_iota(jnp.int32, (1, LANES), 1)
    half = LANES // 2

    q_left, q_right, q_diag = [], [], []
    for comp in range(2):
        data = (lane < half) if comp == 0 else (lane >= half)
        qc = q_ref[comp, rows, :]
        q_left.append(qc)
        q_right.append(jnp.where(data, qc, -qc))
        q_diag.append(jnp.where(data, qc, jnp.zeros_like(qc)))

    def scores(j0, table_off, nk):
        j0 = pl.multiple_of(j0, POS_SPLIT)
        ts = []
        for comp in range(2):
            kc = k_ref[comp, pl.ds(j0, nk), :]
            if table_off is None:
                qv = jnp.where(j0 < i0, q_left[comp], q_right[comp])
            else:
                qv = q_diag[comp]
            t = lax.dot_general(qv, kc, (((1,), (1,)), ((), ())),
                                preferred_element_type=jnp.float32)
            if table_off is not None:
                t = t + bias_ref[:, pl.ds(pl.multiple_of(table_off, POS_SPLIT), nk)]
            ts.append(t)
        return j0, ts, nk

    def accumulate(j0, ts, nk, first):
        vc = v_ref[pl.ds(j0, nk), :]
        for comp in range(2):
            p = jnp.exp2(ts[comp] - shift).astype(jnp.bfloat16)
            pv = jnp.dot(p, vc, preferred_element_type=jnp.float32)
            acc_scr[comp] = pv if first else acc_scr[comp] + pv

    def accumulate_online(j0, ts, nk, first):
        vc = v_ref[pl.ds(j0, nk), :]
        for comp in range(2):
            t = ts[comp]
            m_new = jnp.max(t, axis=-1, keepdims=True)
            if not first:
                m_old = m_scr[comp]
                m_new = jnp.maximum(m_old, m_new)
            p = jnp.exp2(t - m_new).astype(jnp.bfloat16)
            pv = jnp.dot(p, vc, preferred_element_type=jnp.float32)
            if first:
                acc_scr[comp] = pv
            else:
                acc_scr[comp] = jnp.exp2(m_old - m_new) * acc_scr[comp] + pv
            m_scr[comp] = m_new

    shift = shift_ref[(pl.program_id(0) * pl.num_programs(1) + hd) * nq + qi]

    reach = reach_ref[hd]
    lo_key = jnp.maximum(i0 - reach, 0)
    hi_key = jnp.minimum(i0 + tq + reach, s)
    start0 = lo_key & (-POS_SPLIT)
    start0 = jnp.where(((i0 - start0) & (ck - 1)) + tq > ck, start0 - POS_SPLIT, start0)
    width_need = (hi_key - start0 + (ck - 1)) // ck

    def sweep(width):
        def run():
            start = jnp.minimum(start0, s - width * ck)
            rel = i0 - start
            own = rel // ck
            pending = scores(start + own * ck, (ck - tq) - (rel & (ck - 1)), ck)
            for n in range(1, width):
                r = own + n
                nxt = scores(start + jnp.where(r >= width, r - width, r) * ck, None, ck)
                accumulate(*pending, first=(n == 1))
                pending = nxt
            accumulate(*pending, first=(width == 1))
        return run

    def dispatch(width):
        if width == nch:
            sweep(nch)()
        else:
            lax.cond(width_need <= width, sweep(width), lambda: dispatch(width + 1))

    def online_sweep():
        own = i0 // ck
        accumulate_online(*scores(own * ck, (ck - tq) - (i0 & (ck - 1)), ck), first=True)

        def body(n, carry):
            r = own + n
            accumulate_online(*scores(jnp.where(r >= nch, r - nch, r) * ck, None, ck), first=False)
            return carry

        lax.fori_loop(1, nch, body, 0)

    lax.cond(shift >= 0.0, lambda: dispatch(1), online_sweep)

    lam_p = lam_ref[...]
    lam = (jnp.exp(jnp.sum(lam_p[0:1] * lam_p[1:2], axis=-1, keepdims=True))
           - jnp.exp(jnp.sum(lam_p[2:3] * lam_p[3:4], axis=-1, keepdims=True)) + lam_init)
    a0 = acc_scr[0]
    a1 = acc_scr[1]
    o = a0[:, :dv] / a0[:, dv:dv + 1] - lam * (a1[:, :dv] / a1[:, dv:dv + 1])
    o = o * _rms_scale(o) * g_ref[...] * (1.0 - lam_init)
    o_ref[rows, :] = o.astype(jnp.bfloat16)
    return carry


def _diff_attention_mixer(hin, g_pre, w_qkv, lq1, lk1, lq2, lk2, subln_g, layer_idx):
    b, s, d = hin.shape
    nh = N_DIFF_HEADS
    dh2 = d // nh
    dh = dh2 // 2
    dv = dh2
    assert dh2 == LANES and s % ATT_CK == 0
    assert s <= POS_SPLIT * 256
    lam_init = 0.8 - 0.6 * math.exp(-0.3 * layer_idx)
    slopes = np.asarray([2.0 ** (-8.0 * (i + 1) / nh) for i in range(nh)], np.float64)
    cvals = slopes * LOG2E
    qa, ka = _aug_tables(nh, dh2, cvals)
    qscale = dh ** -0.5 * LOG2E

    tm = ATT_TQ
    const2 = lambda i, j: (0, 0)
    nt = s // tm
    norm_spec = pl.BlockSpec((None, nh, None, 1, dh2), lambda i, j: (i, 0, j, 0, 0))
    norm_shape = jax.ShapeDtypeStruct((b, nh, nt, 1, dh2), jnp.float32)
    q, k, v, qn, kn = pl.pallas_call(
        functools.partial(_qkv_kernel, d=d, nh=nh, tm=tm, qscale=qscale),
        grid=(b, s // tm),
        in_specs=[
            pl.BlockSpec((None, tm, d), lambda i, j: (i, j, 0)),
            pl.BlockSpec((1, d), const2),
            pl.BlockSpec((d, 3 * d), const2, pipeline_mode=pl.Buffered(1)),
            pl.BlockSpec((nh, 2, 3, dh2), lambda i, j: (0, 0, 0, 0)),
            pl.BlockSpec((nh, 2, 3, dh2), lambda i, j: (0, 0, 0, 0)),
        ],
        out_specs=[
            pl.BlockSpec((None, nh, 2, tm, dh2), lambda i, j: (i, 0, 0, j, 0)),
            pl.BlockSpec((None, nh, 2, tm, dh2), lambda i, j: (i, 0, 0, j, 0)),
            pl.BlockSpec((None, nh, tm, 2 * dv), lambda i, j: (i, 0, j, 0)),
            norm_spec,
            norm_spec,
        ],
        out_shape=[
            jax.ShapeDtypeStruct((b, nh, 2, s, dh2), jnp.bfloat16),
            jax.ShapeDtypeStruct((b, nh, 2, s, dh2), jnp.bfloat16),
            jax.ShapeDtypeStruct((b, nh, s, 2 * dv), jnp.bfloat16),
            norm_shape,
            norm_shape,
        ],
        compiler_params=pltpu.CompilerParams(
            dimension_semantics=("parallel", "parallel"), vmem_limit_bytes=VMEM_LIMIT_BYTES),
        name="qkv_proj",
    )(hin, g_pre.reshape(1, d), w_qkv.astype(jnp.bfloat16), qa, ka)

    tq, ck = ATT_TQ, ATT_CK
    tstep = tq * ATT_TILES_PER_STEP
    assert s % tstep == 0
    wdt = ck + (ck - tq)
    dist = np.abs(np.arange(tq)[:, None] - np.arange(wdt)[None, :] + (ck - tq))
    bias = jnp.asarray(-cvals[:, None, None] * dist[None], jnp.float32)
    lam_p = jnp.stack([lq1, lk1, lq2, lk2]).astype(jnp.float32)

    full = pl.BlockSpec((None, nh, nt, 1, dh2), lambda i: (i, 0, 0, 0, 0))
    shift = pl.pallas_call(
        _shift_kernel,
        grid=(b,),
        in_specs=[full, full],
        out_specs=full,
        out_shape=jax.ShapeDtypeStruct((b, nh, nt, 1, dh2), jnp.float32),
        name="softmax_shift",
    )(qn, kn)
    shift = shift[..., 0, 0].reshape(-1)
    reach = jnp.asarray(np.minimum(np.floor(SKIP_EXP2_ZERO / cvals) + 1, s), jnp.int32)

    o = pl.pallas_call(
        functools.partial(_attn_kernel, s=s, dv=dv, lam_init=lam_init),
        grid=(b, nh, s // tstep),
        in_specs=[
            pl.BlockSpec(memory_space=pltpu.SMEM),
            pl.BlockSpec(memory_space=pltpu.SMEM),
            pl.BlockSpec((None, None, 2, tstep, dh2), lambda i, h, j: (i, h, 0, j, 0)),
            pl.BlockSpec((None, None, 2, s, dh2), lambda i, h, j: (i, h, 0, 0, 0)),
            pl.BlockSpec((None, None, s, 2 * dv), lambda i, h, j: (i, h, 0, 0)),
            pl.BlockSpec((None, tq, wdt), lambda i, h, j: (h, 0, 0)),
            pl.BlockSpec((4, dh), lambda i, h, j: (0, 0)),
            pl.BlockSpec((1, dv), lambda i, h, j: (0, 0)),
        ],
        out_specs=pl.BlockSpec((None, tstep, dv), lambda i, h, j: (i, j, h)),
        out_shape=jax.ShapeDtypeStruct((b, s, nh * dv), jnp.bfloat16),
        scratch_shapes=[
            pltpu.VMEM((2, tq, 1), jnp.float32),
            pltpu.VMEM((2, tq, 2 * dv), jnp.float32),
        ],
        compiler_params=pltpu.CompilerParams(
            dimension_semantics=("parallel", "parallel", "arbitrary"),
            vmem_limit_bytes=VMEM_LIMIT_BYTES),
        name="diff_attention",
    )(reach, shift, q, k, v, bias, lam_p, subln_g.reshape(1, dv))
    return o


def kernel(x, norm_mix_pre, norm_mix_post, norm_ffn_pre, norm_ffn_post, fourier_w_o, diff_w_qkv,
           diff_lambda_q1, diff_lambda_k1, diff_lambda_q2, diff_lambda_k2, diff_subln_g, diff_w_o,
           ffn_w_gate, ffn_w_up, ffn_w_down):
    depth = norm_mix_pre.shape[0]
    h = x
    for i in range(depth):
        j = i // N_MIXERS
        mixer = None
        if i % N_MIXERS == 0:
            h = _fourier_layer(h, norm_mix_pre[i], norm_mix_post[i], fourier_w_o[j])
        else:
            a = _diff_attention_mixer(h, norm_mix_pre[i], diff_w_qkv[j], diff_lambda_q1[j],
                                      diff_lambda_k1[j], diff_lambda_q2[j], diff_lambda_k2[j],
                                      diff_subln_g[j], i)
            mixer = (a, diff_w_o[j], norm_mix_post[i])
        h = _ffn_layer(h, norm_ffn_pre[i], norm_ffn_post[i], ffn_w_gate[i], ffn_w_up[i], ffn_w_down[i],
                       mixer=mixer)
    return h
```

```python
import functools
import math

import numpy as np
import jax
import jax.numpy as jnp
from jax import lax
from jax.experimental import pallas as pl
from jax.experimental.pallas import tpu as pltpu

N_FOURIER_GROUPS = 8
N_DIFF_HEADS = 8
RMS_EPS = 1e-6
N_MIXERS = 2

LANES = 128
VMEM_LIMIT_BYTES = 56 * 1024 * 1024

FFT_N2 = 128
FFT_S2_PER_STEP = 16
FFT_K1_PER_STEP = 8
FFN_TOKEN_TILE = 1024
FFN_CHUNK = 256
ATT_TQ = 512
ATT_TILES_PER_STEP = 4
ATT_CK = 2048
POS_SPLIT = 256
LOG2E = 1.4426950408889634
SKIP_EXP2_ZERO = 153.0
SKIP_SLACK = 1.01
SHIFT_MAX_LOG2_RANGE = 100.0


def _rms_scale(x):
    return lax.rsqrt(jnp.mean(x * x, axis=-1, keepdims=True) + RMS_EPS)


def _np_bf16(v):
    return np.asarray(v, np.float32).astype(jnp.bfloat16)


def _bf16_split3(v):
    v = np.asarray(v, np.float64)
    hi = _np_bf16(v).astype(np.float64)
    mid = _np_bf16(v - hi).astype(np.float64)
    lo = _np_bf16(v - hi - mid).astype(np.float64)
    return hi, mid, lo


def _fft_stage1_kernel(x_ref, g_ref, f1_ref, ar_ref, ai_ref, *, n1, d, nb):
    x = pltpu.einshape("sjd->jsd", x_ref[...]).reshape(nb * n1, d)
    xn = (x * _rms_scale(x) * g_ref[...]).astype(jnp.bfloat16)
    f1 = f1_ref[...]
    a = jnp.stack([jnp.dot(f1, xn[j * n1:(j + 1) * n1], preferred_element_type=jnp.float32)
                   for j in range(nb)], axis=0)
    a = pltpu.einshape("jkd->kjd", a)
    ar_ref[...] = a[:n1].astype(jnp.bfloat16)
    ai_ref[...] = a[n1:].astype(jnp.bfloat16)


def _fft_stage2_kernel(ar_ref, ai_ref, gm_ref, cs_ref, wo_ref, gpost_ref, xres_ref, out_ref,
                       *, n2, d, cg, kb):
    cs = cs_ref[...]
    zs = []
    for i in range(kb):
        a = jnp.concatenate([ar_ref[i], ai_ref[i]], axis=0)
        y = jnp.dot(gm_ref[i], a, preferred_element_type=jnp.float32)
        yr = y[:n2].astype(jnp.bfloat16)
        yi = y[n2:].astype(jnp.bfloat16)
        cols = []
        for gi in range(d // cg):
            lhs = jnp.concatenate([yr[:, gi * cg:(gi + 1) * cg], yi[:, gi * cg:(gi + 1) * cg]], axis=1)
            cols.append(jnp.dot(lhs, cs, preferred_element_type=jnp.float32))
        zs.append(jnp.concatenate(cols, axis=1).astype(jnp.bfloat16))
    z = jnp.concatenate(zs, axis=0)
    m = jnp.dot(z, wo_ref[...], preferred_element_type=jnp.float32)
    r = m * _rms_scale(m) * gpost_ref[...]
    out_ref[...] = xres_ref[...] + pltpu.einshape("ikd->kid", r.reshape(kb, n2, d))


def _fourier_layer(x, g_pre, g_post, w_o):
    b, s, d = x.shape
    n2 = FFT_N2
    n1 = s // n2
    cg = d // N_FOURIER_GROUPS
    nb = FFT_S2_PER_STEP
    assert n1 * n2 == s and n2 % nb == 0

    s1 = np.arange(n1)
    th1 = 2.0 * np.pi * ((s1[:, None] * s1[None, :]) % n1) / n1
    f1 = np.concatenate([np.cos(th1), -np.sin(th1)], axis=0)
    s2 = np.arange(n2)
    kk = (np.arange(n1)[:, None, None] + n1 * np.arange(n2)[None, :, None])
    th2 = 2.0 * np.pi * ((kk * s2[None, None, :]) % s) / s
    gr, gi = np.cos(th2), -np.sin(th2)
    gm = np.concatenate([np.concatenate([gr, -gi], axis=2),
                         np.concatenate([gi, gr], axis=2)], axis=1)
    c = np.arange(cg)
    thc = 2.0 * np.pi * ((c[:, None] * c[None, :]) % cg) / cg
    norm = 1.0 / math.sqrt(s * cg)
    cs = np.concatenate([np.cos(thc), np.sin(thc)], axis=0) * norm

    f1 = jnp.asarray(_np_bf16(f1))
    gm = jnp.asarray(_np_bf16(gm))
    cs = jnp.asarray(_np_bf16(cs))

    x4 = x.reshape(b, n1, n2, d)
    ar, ai = pl.pallas_call(
        functools.partial(_fft_stage1_kernel, n1=n1, d=d, nb=nb),
        grid=(b, n2 // nb),
        in_specs=[
            pl.BlockSpec((None, n1, nb, d), lambda i, j: (i, 0, j, 0)),
            pl.BlockSpec((1, d), lambda i, j: (0, 0)),
            pl.BlockSpec((2 * n1, n1), lambda i, j: (0, 0)),
        ],
        out_specs=[
            pl.BlockSpec((None, n1, nb, d), lambda i, j: (i, 0, j, 0)),
            pl.BlockSpec((None, n1, nb, d), lambda i, j: (i, 0, j, 0)),
        ],
        out_shape=[jax.ShapeDtypeStruct((b, n1, n2, d), jnp.bfloat16)] * 2,
        compiler_params=pltpu.CompilerParams(
            dimension_semantics=("parallel", "parallel"), vmem_limit_bytes=VMEM_LIMIT_BYTES),
        name="fft_stage1",
    )(x4, g_pre.reshape(1, d), f1)

    kb = min(FFT_K1_PER_STEP, n1)
    assert n1 % kb == 0
    xk = x.reshape(b, n2, n1, d)
    out = pl.pallas_call(
        functools.partial(_fft_stage2_kernel, n2=n2, d=d, cg=cg, kb=kb),
        grid=(b, n1 // kb),
        in_specs=[
            pl.BlockSpec((None, kb, n2, d), lambda i, j: (i, j, 0, 0)),
            pl.BlockSpec((None, kb, n2, d), lambda i, j: (i, j, 0, 0)),
            pl.BlockSpec((kb, 2 * n2, 2 * n2), lambda i, j: (j, 0, 0)),
            pl.BlockSpec((2 * cg, cg), lambda i, j: (0, 0)),
            pl.BlockSpec((d, d), lambda i, j: (0, 0)),
            pl.BlockSpec((1, d), lambda i, j: (0, 0)),
            pl.BlockSpec((None, n2, kb, d), lambda i, j: (i, 0, j, 0)),
        ],
        out_specs=pl.BlockSpec((None, n2, kb, d), lambda i, j: (i, 0, j, 0)),
        out_shape=jax.ShapeDtypeStruct((b, n2, n1, d), jnp.float32),
        compiler_params=pltpu.CompilerParams(
            dimension_semantics=("parallel", "parallel"), vmem_limit_bytes=VMEM_LIMIT_BYTES),
        name="fft_stage2",
    )(ar, ai, gm, cs, w_o.astype(jnp.bfloat16), g_post.reshape(1, d), xk)
    return out.reshape(b, s, d)


def _ffn_kernel(*refs, f, chunk, mixer_proj):
    if mixer_proj:
        a_ref, wo_ref, gmix_ref, x_ref, gpre_ref, gpost_ref, wg_ref, wu_ref, wd_ref, out_ref = refs
        m = jnp.dot(a_ref[...], wo_ref[...], preferred_element_type=jnp.float32)
        x = x_ref[...] + m * _rms_scale(m) * gmix_ref[...]
    else:
        x_ref, gpre_ref, gpost_ref, wg_ref, wu_ref, wd_ref, out_ref = refs
        x = x_ref[...]
    xn = (x * _rms_scale(x) * gpre_ref[...]).astype(jnp.bfloat16)
    acc = jnp.zeros(x.shape, jnp.float32)
    for c0 in range(0, f, chunk):
        c1 = min(c0 + chunk, f)
        gt = jnp.dot(xn, wg_ref[:, c0:c1], preferred_element_type=jnp.float32)
        up = jnp.dot(xn, wu_ref[:, c0:c1], preferred_element_type=jnp.float32)
        hh = (gt * jax.nn.sigmoid(gt) * up).astype(jnp.bfloat16)
        acc = acc + jnp.dot(hh, wd_ref[c0:c1, :], preferred_element_type=jnp.float32)
    out_ref[...] = x + acc * _rms_scale(acc) * gpost_ref[...]


def _ffn_layer(h, g_pre, g_post, w_gate, w_up, w_down, mixer=None):
    b, s, d = h.shape
    f = w_gate.shape[1]
    t = b * s
    tm = min(FFN_TOKEN_TILE, t)
    assert t % tm == 0
    const = lambda i: (0, 0)
    rows = pl.BlockSpec((tm, d), lambda i: (i, 0))
    vec = pl.BlockSpec((1, d), const)
    in_specs = [rows, vec, vec,
                pl.BlockSpec((d, f), const, pipeline_mode=pl.Buffered(1)),
                pl.BlockSpec((d, f), const, pipeline_mode=pl.Buffered(1)),
                pl.BlockSpec((f, d), const, pipeline_mode=pl.Buffered(1))]
    args = [h.reshape(t, d), g_pre.reshape(1, d), g_post.reshape(1, d),
            w_gate.astype(jnp.bfloat16), w_up.astype(jnp.bfloat16), w_down.astype(jnp.bfloat16)]
    if mixer is not None:
        a, w_o, g_mix = mixer
        in_specs = [rows, pl.BlockSpec((d, d), const, pipeline_mode=pl.Buffered(1)), vec] + in_specs
        args = [a.reshape(t, d), w_o.astype(jnp.bfloat16), g_mix.reshape(1, d)] + args
    out = pl.pallas_call(
        functools.partial(_ffn_kernel, f=f, chunk=FFN_CHUNK, mixer_proj=mixer is not None),
        grid=(t // tm,),
        in_specs=in_specs,
        out_specs=rows,
        out_shape=jax.ShapeDtypeStruct((t, d), jnp.float32),
        compiler_params=pltpu.CompilerParams(
            dimension_semantics=("parallel",), vmem_limit_bytes=VMEM_LIMIT_BYTES),
        name="swiglu_ffn",
    )(*args)
    return out.reshape(b, s, d)


def _qkv_kernel(x_ref, gpre_ref, w_ref, qa_ref, ka_ref, q_ref, k_ref, v_ref, qn_ref, kn_ref,
                *, d, nh, tm, qscale):
    x = x_ref[...]
    xn = (x * _rms_scale(x) * gpre_ref[...]).astype(jnp.bfloat16)
    dh2 = d // nh
    row = pl.program_id(1) * tm + lax.broadcasted_iota(jnp.int32, (tm, 1), 0)
    pos_lo = (row & (POS_SPLIT - 1)).astype(jnp.float32)
    pos_hi = (row & (-POS_SPLIT)).astype(jnp.float32)
    lane = lax.broadcasted_iota(jnp.int32, (1, dh2), 1)
    half = dh2 // 2
    q_all = jnp.dot(xn, w_ref[:, 0:d], preferred_element_type=jnp.float32) * qscale
    k_all = jnp.dot(xn, w_ref[:, d:2 * d], preferred_element_type=jnp.float32)
    v_all = jnp.dot(xn, w_ref[:, 2 * d:3 * d], preferred_element_type=jnp.float32)
    ones_col = jnp.where(lane == 0, 1.0, 0.0).astype(jnp.bfloat16)
    lane_col = lax.broadcasted_iota(jnp.int32, (dh2, 1), 0)
    comp_sum = jnp.where((lane_col >= half).astype(jnp.int32) == lane, 1.0, 0.0).astype(jnp.bfloat16)
    for h in range(nh):
        qh = q_all[:, h * dh2:(h + 1) * dh2]
        kh = k_all[:, h * dh2:(h + 1) * dh2]
        for src, dst in ((qh, qn_ref), (kh, kn_ref)):
            vb = src.astype(jnp.bfloat16).astype(jnp.float32)
            sq = jnp.dot((vb * vb).astype(jnp.bfloat16), comp_sum, preferred_element_type=jnp.float32)
            dst[h] = jnp.max(sq, axis=0, keepdims=True)
        for comp in range(2):
            data = (lane < half) if comp == 0 else (lane >= half)
            qa = qa_ref[h, comp]
            ka = ka_ref[h, comp]
            q_aug = qa[0:1] + qa[1:2] * pos_lo + qa[2:3] * pos_hi
            k_aug = ka[0:1] + ka[1:2] * pos_lo + ka[2:3] * pos_hi
            q_ref[h, comp] = jnp.where(data, qh, q_aug).astype(jnp.bfloat16)
            k_ref[h, comp] = jnp.where(data, kh, k_aug).astype(jnp.bfloat16)
        vh = v_all[:, h * dh2:(h + 1) * dh2].astype(jnp.bfloat16)
        v_ref[h] = jnp.concatenate([vh, jnp.broadcast_to(ones_col, (tm, dh2))], axis=1)


def _aug_tables(nh, dh2, cvals):
    half = dh2 // 2
    qa = np.zeros((nh, 2, 3, dh2), np.float64)
    ka = np.zeros((nh, 2, 3, dh2), np.float64)
    for h in range(nh):
        pieces = _bf16_split3(cvals[h])
        for comp in range(2):
            base = half if comp == 0 else 0
            for p in range(3):
                for part in (1, 2):
                    qa[h, comp, 0, base + 3 * (part - 1) + p] = pieces[p]
                    ka[h, comp, part, base + 3 * (part - 1) + p] = 1.0
                    qa[h, comp, part, base + 6 + 3 * (part - 1) + p] = -1.0
                    ka[h, comp, 0, base + 6 + 3 * (part - 1) + p] = pieces[p]
    return jnp.asarray(qa, jnp.float32), jnp.asarray(ka, jnp.float32)


def _shift_kernel(qn_ref, kn_ref, shift_ref):
    kmax = jnp.max(kn_ref[...], axis=1, keepdims=True)
    prod = qn_ref[...] * kmax
    u = jnp.sqrt(jnp.maximum(prod[..., 0:1], prod[..., 1:2]))
    shift = jnp.where(2.0 * SKIP_SLACK * u <= SHIFT_MAX_LOG2_RANGE, SKIP_SLACK * u, -1.0)
    shift_ref[...] = jnp.broadcast_to(shift, shift_ref.shape)


def _attn_kernel(reach_ref, shift_ref, q_ref, k_ref, v_ref, bias_ref, lam_ref, g_ref, o_ref,
                 m_scr, acc_scr, *, s, dv, lam_init):
    lam_p = lam_ref[...]
    lam = (jnp.exp(jnp.sum(lam_p[0:1] * lam_p[1:2], axis=-1, keepdims=True))
           - jnp.exp(jnp.sum(lam_p[2:3] * lam_p[3:4], axis=-1, keepdims=True)) + lam_init)
    gain = g_ref[...] * (1.0 - lam_init)
    lax.fori_loop(0, ATT_TILES_PER_STEP,
                  functools.partial(_attn_tile, reach_ref, shift_ref, q_ref, k_ref, v_ref, bias_ref,
                                    lam, gain, o_ref, m_scr, acc_scr, s, dv), 0)


def _attn_tile(reach_ref, shift_ref, q_ref, k_ref, v_ref, bias_ref, lam, gain, o_ref,
               m_scr, acc_scr, s, dv, t, carry):
    tq, ck = ATT_TQ, ATT_CK
    nch = s // ck
    hd = pl.program_id(1)
    qi = pl.program_id(2) * ATT_TILES_PER_STEP + t
    nq = pl.num_programs(2) * ATT_TILES_PER_STEP
    i0 = qi * tq
    rows = pl.ds(pl.multiple_of(t * tq, tq), tq)
    lane = lax.broadcasted_iota(jnp.int32, (1, LANES), 1)
    half = LANES // 2

    q_left, q_right, q_diag = [], [], []
    for comp in range(2):
        data = (lane < half) if comp == 0 else (lane >= half)
        qc = q_ref[comp, rows, :]
        q_left.append(qc)
        q_right.append(jnp.where(data, qc, -qc))
        q_diag.append(jnp.where(data, qc, jnp.zeros_like(qc)))

    def scores(j0, table_off, nk):
        j0 = pl.multiple_of(j0, POS_SPLIT)
        ts = []
        for comp in range(2):
            kc = k_ref[comp, pl.ds(j0, nk), :]
            if table_off is None:
                qv = jnp.where(j0 < i0, q_left[comp], q_right[comp])
            else:
                qv = q_diag[comp]
            t = lax.dot_general(qv, kc, (((1,), (1,)), ((), ())),
                                preferred_element_type=jnp.float32)
            if table_off is not None:
                t = t + bias_ref[:, pl.ds(pl.multiple_of(table_off, POS_SPLIT), nk)]
            ts.append(t)
        return j0, ts, nk

    def accumulate(j0, ts, nk, first):
        vc = v_ref[pl.ds(j0, nk), :]
        for comp in range(2):
            p = jnp.exp2(ts[comp] - shift).astype(jnp.bfloat16)
            pv = jnp.dot(p, vc, preferred_element_type=jnp.float32)
            acc_scr[comp] = pv if first else acc_scr[comp] + pv

    def accumulate_online(j0, ts, nk, first):
        vc = v_ref[pl.ds(j0, nk), :]
        for comp in range(2):
            t = ts[comp]
            m_new = jnp.max(t, axis=-1, keepdims=True)
            if not first:
                m_old = m_scr[comp]
                m_new = jnp.maximum(m_old, m_new)
            p = jnp.exp2(t - m_new).astype(jnp.bfloat16)
            pv = jnp.dot(p, vc, preferred_element_type=jnp.float32)
            if first:
                acc_scr[comp] = pv
            else:
                acc_scr[comp] = jnp.exp2(m_old - m_new) * acc_scr[comp] + pv
            m_scr[comp] = m_new

    shift = shift_ref[(pl.program_id(0) * pl.num_programs(1) + hd) * nq + qi]

    reach = reach_ref[hd]
    lo_key = jnp.maximum(i0 - reach, 0)
    hi_key = jnp.minimum(i0 + tq + reach, s)
    start0 = lo_key & (-POS_SPLIT)
    start0 = jnp.where(((i0 - start0) & (ck - 1)) + tq > ck, start0 - POS_SPLIT, start0)
    width_need = (hi_key - start0 + (ck - 1)) // ck

    def sweep(width):
        def run():
            start = jnp.minimum(start0, s - width * ck)
            rel = i0 - start
            own = rel // ck
            pending = scores(start + own * ck, (ck - tq) - (rel & (ck - 1)), ck)
            for n in range(1, width):
                r = own + n
                nxt = scores(start + jnp.where(r >= width, r - width, r) * ck, None, ck)
                accumulate(*pending, first=(n == 1))
                pending = nxt
            accumulate(*pending, first=(width == 1))
        return run

    def dispatch(width):
        if width == nch:
            sweep(nch)()
        else:
            lax.cond(width_need <= width, sweep(width), lambda: dispatch(width + 1))

    def online_sweep():
        own = i0 // ck
        accumulate_online(*scores(own * ck, (ck - tq) - (i0 & (ck - 1)), ck), first=True)

        def body(n, carry):
            r = own + n
            accumulate_online(*scores(jnp.where(r >= nch, r - nch, r) * ck, None, ck), first=False)
            return carry

        lax.fori_loop(1, nch, body, 0)

    lax.cond(shift >= 0.0, lambda: dispatch(1), online_sweep)

    a0 = acc_scr[0]
    a1 = acc_scr[1]
    o = a0[:, :dv] / a0[:, dv:dv + 1] - lam * (a1[:, :dv] / a1[:, dv:dv + 1])
    o = o * _rms_scale(o) * gain
    o_ref[rows, :] = o.astype(jnp.bfloat16)
    return carry


def _diff_attention_mixer(hin, g_pre, w_qkv, lq1, lk1, lq2, lk2, subln_g, layer_idx):
    b, s, d = hin.shape
    nh = N_DIFF_HEADS
    dh2 = d // nh
    dh = dh2 // 2
    dv = dh2
    assert dh2 == LANES and s % ATT_CK == 0
    assert s <= POS_SPLIT * 256
    lam_init = 0.8 - 0.6 * math.exp(-0.3 * layer_idx)
    slopes = np.asarray([2.0 ** (-8.0 * (i + 1) / nh) for i in range(nh)], np.float64)
    cvals = slopes * LOG2E
    qa, ka = _aug_tables(nh, dh2, cvals)
    qscale = dh ** -0.5 * LOG2E

    tm = ATT_TQ
    const2 = lambda i, j: (0, 0)
    nt = s // tm
    norm_spec = pl.BlockSpec((None, nh, None, 1, dh2), lambda i, j: (i, 0, j, 0, 0))
    norm_shape = jax.ShapeDtypeStruct((b, nh, nt, 1, dh2), jnp.float32)
    q, k, v, qn, kn = pl.pallas_call(
        functools.partial(_qkv_kernel, d=d, nh=nh, tm=tm, qscale=qscale),
        grid=(b, s // tm),
        in_specs=[
            pl.BlockSpec((None, tm, d), lambda i, j: (i, j, 0)),
            pl.BlockSpec((1, d), const2),
            pl.BlockSpec((d, 3 * d), const2, pipeline_mode=pl.Buffered(1)),
            pl.BlockSpec((nh, 2, 3, dh2), lambda i, j: (0, 0, 0, 0)),
            pl.BlockSpec((nh, 2, 3, dh2), lambda i, j: (0, 0, 0, 0)),
        ],
        out_specs=[
            pl.BlockSpec((None, nh, 2, tm, dh2), lambda i, j: (i, 0, 0, j, 0)),
            pl.BlockSpec((None, nh, 2, tm, dh2), lambda i, j: (i, 0, 0, j, 0)),
            pl.BlockSpec((None, nh, tm, 2 * dv), lambda i, j: (i, 0, j, 0)),
            norm_spec,
            norm_spec,
        ],
        out_shape=[
            jax.ShapeDtypeStruct((b, nh, 2, s, dh2), jnp.bfloat16),
            jax.ShapeDtypeStruct((b, nh, 2, s, dh2), jnp.bfloat16),
            jax.ShapeDtypeStruct((b, nh, s, 2 * dv), jnp.bfloat16),
            norm_shape,
            norm_shape,
        ],
        compiler_params=pltpu.CompilerParams(
            dimension_semantics=("parallel", "parallel"), vmem_limit_bytes=VMEM_LIMIT_BYTES),
        name="qkv_proj",
    )(hin, g_pre.reshape(1, d), w_qkv.astype(jnp.bfloat16), qa, ka)

    tq, ck = ATT_TQ, ATT_CK
    tstep = tq * ATT_TILES_PER_STEP
    assert s % tstep == 0
    wdt = ck + (ck - tq)
    dist = np.abs(np.arange(tq)[:, None] - np.arange(wdt)[None, :] + (ck - tq))
    bias = jnp.asarray(-cvals[:, None, None] * dist[None], jnp.float32)
    lam_p = jnp.stack([lq1, lk1, lq2, lk2]).astype(jnp.float32)

    full = pl.BlockSpec((None, nh, nt, 1, dh2), lambda i: (i, 0, 0, 0, 0))
    shift = pl.pallas_call(
        _shift_kernel,
        grid=(b,),
        in_specs=[full, full],
        out_specs=full,
        out_shape=jax.ShapeDtypeStruct((b, nh, nt, 1, dh2), jnp.float32),
        name="softmax_shift",
    )(qn, kn)
    shift = shift[..., 0, 0].reshape(-1)
    reach = jnp.asarray(np.minimum(np.floor(SKIP_EXP2_ZERO / cvals) + 1, s), jnp.int32)

    o = pl.pallas_call(
        functools.partial(_attn_kernel, s=s, dv=dv, lam_init=lam_init),
        grid=(b, nh, s // tstep),
        in_specs=[
            pl.BlockSpec(memory_space=pltpu.SMEM),
            pl.BlockSpec(memory_space=pltpu.SMEM),
            pl.BlockSpec((None, None, 2, tstep, dh2), lambda i, h, j: (i, h, 0, j, 0)),
            pl.BlockSpec((None, None, 2, s, dh2), lambda i, h, j: (i, h, 0, 0, 0)),
            pl.BlockSpec((None, None, s, 2 * dv), lambda i, h, j: (i, h, 0, 0)),
            pl.BlockSpec((None, tq, wdt), lambda i, h, j: (h, 0, 0)),
            pl.BlockSpec((4, dh), lambda i, h, j: (0, 0)),
            pl.BlockSpec((1, dv), lambda i, h, j: (0, 0)),
        ],
        out_specs=pl.BlockSpec((None, tstep, dv), lambda i, h, j: (i, j, h)),
        out_shape=jax.ShapeDtypeStruct((b, s, nh * dv), jnp.bfloat16),
        scratch_shapes=[
            pltpu.VMEM((2, tq, 1), jnp.float32),
            pltpu.VMEM((2, tq, 2 * dv), jnp.float32),
        ],
        compiler_params=pltpu.CompilerParams(
            dimension_semantics=("parallel", "parallel", "arbitrary"),
            vmem_limit_bytes=VMEM_LIMIT_BYTES),
        name="diff_attention",
    )(reach, shift, q, k, v, bias, lam_p, subln_g.reshape(1, dv))
    return o


def kernel(x, norm_mix_pre, norm_mix_post, norm_ffn_pre, norm_ffn_post, fourier_w_o, diff_w_qkv,
           diff_lambda_q1, diff_lambda_k1, diff_lambda_q2, diff_lambda_k2, diff_subln_g, diff_w_o,
           ffn_w_gate, ffn_w_up, ffn_w_down):
    depth = norm_mix_pre.shape[0]
    h = x
    for i in range(depth):
        j = i // N_MIXERS
        mixer = None
        if i % N_MIXERS == 0:
            h = _fourier_layer(h, norm_mix_pre[i], norm_mix_post[i], fourier_w_o[j])
        else:
            a = _diff_attention_mixer(h, norm_mix_pre[i], diff_w_qkv[j], diff_lambda_q1[j],
                                      diff_lambda_k1[j], diff_lambda_q2[j], diff_lambda_k2[j],
                                      diff_subln_g[j], i)
            mixer = (a, diff_w_o[j], norm_mix_post[i])
        h = _ffn_layer(h, norm_ffn_pre[i], norm_ffn_post[i], ffn_w_gate[i], ffn_w_up[i], ffn_w_down[i],
                       mixer=mixer)
    return h
```
